```python
import math
import jax, jax.numpy as jnp
from jax import lax
import numpy as np

D_MODEL = 1024
BATCH = 8
SEQ = 4096
DEPTH = 1

CHUNK = 64
Q_BLOCK = 128
EPS = 1e-6
NEG_INF = -1e30

ATT_HEADS = 8
ATT_KV_HEADS = 2
ATT_HEAD_DIM = 64
ATT_Q_WIDTH = ATT_HEADS * ATT_HEAD_DIM
ATT_KV_WIDTH = ATT_KV_HEADS * ATT_HEAD_DIM
IDX_HEADS = 8
IDX_DIM = 32
TOPK_MAX = 256
ROPE_THETA = 500000.0
ROPE_FRACTION = 4

MLSTM_HEADS = 4
MLSTM_HEAD_DIM = 128
MLSTM_WIDTH = MLSTM_HEADS * MLSTM_HEAD_DIM
CONV_WIDTH = 4

N_GROUPS = 4
EXPERTS_PER_GROUP = 8
N_EXPERTS = N_GROUPS * EXPERTS_PER_GROUP
TOP_K_EXPERTS = 2
D_EXPERT = 512
MOE_BLOCK = 128

SPLIT_SIZES = (ATT_Q_WIDTH, ATT_KV_WIDTH, ATT_KV_WIDTH,
               IDX_HEADS * IDX_DIM, IDX_DIM, IDX_HEADS,
               MLSTM_WIDTH, MLSTM_WIDTH, MLSTM_WIDTH, MLSTM_WIDTH,
               MLSTM_HEADS, MLSTM_HEADS,
               D_MODEL, D_MODEL)
D_IN_PROJ = sum(SPLIT_SIZES)

kernel_name = 'hybrid_dsa_mlstm_hmoe_block'


def rmsnorm(x, g):
    xf = x.astype(jnp.float32)
    y = xf * lax.rsqrt(jnp.mean(xf * xf, axis=-1, keepdims=True) + EPS)
    return (y * g.astype(jnp.float32)).astype(x.dtype)


def partial_rope(t, positions):
    d = t.shape[-1]
    rot = d // ROPE_FRACTION
    half = rot // 2
    inv_freq = jnp.float32(ROPE_THETA) ** (-jnp.arange(half, dtype=jnp.float32) * 2.0 / rot)
    ang = positions.astype(jnp.float32)[..., None] * inv_freq
    ang = ang.reshape(ang.shape[:2] + (1,) * (t.ndim - 3) + (half,))
    cos, sin = jnp.cos(ang), jnp.sin(ang)
    t1 = t[..., :half].astype(jnp.float32)
    t2 = t[..., half:rot].astype(jnp.float32)
    rotated = jnp.concatenate([t1 * cos - t2 * sin, t2 * cos + t1 * sin], axis=-1).astype(t.dtype)
    return jnp.concatenate([rotated, t[..., rot:]], axis=-1)


def causal_depthwise_conv(u, w, b):
    c = u.shape[-1]
    y = lax.conv_general_dilated(u, w.astype(u.dtype)[:, None, :], window_strides=(1,),
                                 padding=[(CONV_WIDTH - 1, 0)],
                                 dimension_numbers=('NWC', 'WIO', 'NWC'),
                                 feature_group_count=c)
    return y + b.astype(u.dtype)


def dsa_attention(q, k, v, q_idx, k_idx, w_idx):
    bsz, seq = q.shape[0], q.shape[1]
    groups = ATT_KV_HEADS
    rep = ATT_HEADS // ATT_KV_HEADS
    topk = min(TOPK_MAX, seq // 4)
    nb = seq // Q_BLOCK
    key_chunk = jnp.arange(seq) // CHUNK
    idx_scale = (IDX_DIM * IDX_HEADS) ** -0.5
    att_scale = ATT_HEAD_DIM ** -0.5
    qb = q.reshape(bsz, nb, Q_BLOCK, groups, rep, ATT_HEAD_DIM).swapaxes(0, 1)
    qib = q_idx.reshape(bsz, nb, Q_BLOCK, IDX_HEADS, IDX_DIM).swapaxes(0, 1)
    wib = w_idx.reshape(bsz, nb, Q_BLOCK, IDX_HEADS).swapaxes(0, 1)

    def block(args):
        blk, q_b, qi_b, wi_b = args
        q_chunk = (blk * Q_BLOCK + jnp.arange(Q_BLOCK)) // CHUNK
        visible = key_chunk[None, :] <= q_chunk[:, None]
        rel = jax.nn.relu(jnp.einsum('bqhd,bsd->bqhs', qi_b, k_idx))
        score = jnp.einsum('bqhs,bqh->bqs', rel, wi_b).astype(jnp.float32) * idx_scale
        score = jnp.where(visible[None], score, NEG_INF)
        top_val, top_idx = lax.top_k(score, topk)
        valid = top_val > NEG_INF * 0.5
        k_sel = jax.vmap(lambda kb, ib: kb[ib])(k, top_idx)
        v_sel = jax.vmap(lambda vb, ib: vb[ib])(v, top_idx)
        logits = jnp.einsum('bqgrd,bqkgd->bqgrk', q_b, k_sel).astype(jnp.float32) * att_scale
        logits = jnp.where(valid[:, :, None, None, :], logits, NEG_INF)
        p = jax.nn.softmax(logits, axis=-1)
        return jnp.einsum('bqgrk,bqkgd->bqgrd', p.astype(v.dtype), v_sel)

    out = lax.map(block, (jnp.arange(nb), qb, qib, wib))
    return out.swapaxes(0, 1).reshape(bsz, seq, ATT_Q_WIDTH)


def mlstm_chunkwise(q, k, v, i_pre, f_pre):
    bsz, seq, nh, d = q.shape
    nc = seq // CHUNK

    def to_chunks(t):
        return t.reshape(bsz, nc, CHUNK, nh, d).transpose(1, 0, 3, 2, 4).astype(jnp.float32)

    def gate_chunks(g):
        return g.reshape(bsz, nc, CHUNK, nh).transpose(1, 0, 3, 2)

    q_c = to_chunks(q)
    k_c = to_chunks(k) * (d ** -0.5)
    v_c = to_chunks(v)
    lf_c = gate_chunks(jax.nn.log_sigmoid(f_pre))
    i_c = gate_chunks(i_pre)
    causal = jnp.tril(jnp.ones((CHUNK, CHUNK), dtype=bool))

    def step(carry, inp):
        c_st, n_st, m_st = carry
        qc, kc, vc, lf, ig = inp
        b = jnp.cumsum(lf, axis=-1)
        dmat = jnp.where(causal, b[..., :, None] - b[..., None, :] + ig[..., None, :], NEG_INF)
        inter = b + m_st[..., None]
        m_t = jnp.maximum(inter, jnp.max(dmat, axis=-1))
        w_intra = jnp.exp(dmat - m_t[..., None])
        w_inter = jnp.exp(inter - m_t)
        s = jnp.einsum('bhtd,bhsd->bhts', qc, kc) * w_intra
        num = w_inter[..., None] * jnp.einsum('bhtd,bhde->bhte', qc, c_st) + jnp.einsum('bhts,bhse->bhte', s, vc)
        den = w_inter * jnp.einsum('bhtd,bhd->bht', qc, n_st) + jnp.sum(s, axis=-1)
        h = num / jnp.maximum(jnp.abs(den), jnp.exp(-m_t))[..., None]
        b_last = b[..., -1]
        g = b_last[..., None] - b + ig
        m_new = jnp.maximum(b_last + m_st, jnp.max(g, axis=-1))
        decay = jnp.exp(b_last + m_st - m_new)
        wk = jnp.exp(g - m_new[..., None])
        c_new = decay[..., None, None] * c_st + jnp.einsum('bhs,bhsd,bhse->bhde', wk, kc, vc)
        n_new = decay[..., None] * n_st + jnp.einsum('bhs,bhsd->bhd', wk, kc)
        return (c_new, n_new, m_new), h

    init = (jnp.zeros((bsz, nh, d, d), jnp.float32), jnp.zeros((bsz, nh, d), jnp.float32),
            jnp.zeros((bsz, nh), jnp.float32))
    _, hs = lax.scan(step, init, (q_c, k_c, v_c, lf_c, i_c))
    return hs.transpose(1, 0, 3, 2, 4).reshape(bsz, seq, nh, d).astype(q.dtype)


def routed_experts(xt, expert_id, gate, w_gate, w_up, w_down):
    t, dm = xt.shape
    m = t * TOP_K_EXPERTS
    e_flat = expert_id.reshape(m)
    tok = jnp.arange(m, dtype=jnp.int32) // TOP_K_EXPERTS
    w_flat = gate.reshape(m)
    order = jnp.argsort(e_flat)
    e_sorted = e_flat[order]
    counts = jax.ops.segment_sum(jnp.ones((m,), jnp.int32), e_flat, num_segments=N_EXPERTS)
    padded = ((counts + MOE_BLOCK - 1) // MOE_BLOCK) * MOE_BLOCK
    starts = jnp.cumsum(counts) - counts
    pends = jnp.cumsum(padded)
    pstarts = pends - padded
    dest = pstarts[e_sorted] + (jnp.arange(m, dtype=jnp.int32) - starts[e_sorted])
    total = ((m + N_EXPERTS * MOE_BLOCK + MOE_BLOCK - 1) // MOE_BLOCK) * MOE_BLOCK
    nblk = total // MOE_BLOCK
    buf_tok = jnp.zeros((total,), jnp.int32).at[dest].set(tok[order])
    buf_w = jnp.zeros((total,), xt.dtype).at[dest].set(w_flat[order])
    block_e = jnp.minimum(jnp.searchsorted(pends, jnp.arange(nblk) * MOE_BLOCK, side='right'),
                          N_EXPERTS - 1)
    xb = xt[buf_tok].reshape(nblk, MOE_BLOCK, dm)

    def expert_block(args):
        xblk, e = args
        hid = jax.nn.silu(xblk @ w_gate[e]) * (xblk @ w_up[e])
        return hid @ w_down[e]

    yb = lax.map(expert_block, (xb, block_e)).reshape(total, dm)
    return jnp.zeros((t, dm), xt.dtype).at[buf_tok].add(yb * buf_w[:, None])


def hierarchical_moe(xn, w_router_group, b_router_group, w_router_expert, b_router_expert,
                     w_exp_gate, w_exp_up, w_exp_down):
    bsz, seq, dm = xn.shape
    xt = xn.reshape(bsz * seq, dm)
    g_logits = (xt @ w_router_group).astype(jnp.float32) + b_router_group.astype(jnp.float32)
    g_prob = jax.nn.softmax(g_logits, axis=-1)
    g_top_val, g_sel = lax.top_k(g_prob, 1)
    e_logits = ((xt @ w_router_expert).astype(jnp.float32) + b_router_expert.astype(jnp.float32))
    e_logits = e_logits.reshape(-1, N_GROUPS, EXPERTS_PER_GROUP)
    in_group = jnp.take_along_axis(e_logits, g_sel[:, :, None], axis=1)[:, 0]
    top_val, top_loc = lax.top_k(in_group, TOP_K_EXPERTS)
    gate = (jax.nn.softmax(top_val, axis=-1) * g_top_val).astype(xt.dtype)
    expert_id = g_sel * EXPERTS_PER_GROUP + top_loc
    y = routed_experts(xt, expert_id, gate, w_exp_gate, w_exp_up, w_exp_down)
    return y.reshape(bsz, seq, dm)


def hybrid_layer(x, positions, g_norm_mix, w_in, g_q_att, g_k_att, w_conv_mlstm, b_conv_mlstm,
                 b_igate, b_fgate, g_norm_mlstm_out, w_proj_att, w_proj_mlstm, w_out,
                 g_norm_moe, w_router_group, b_router_group, w_router_expert, b_router_expert,
                 w_exp_gate, w_exp_up, w_exp_down):
    bsz, seq, _ = x.shape
    h = rmsnorm(x, g_norm_mix)
    proj = h @ w_in
    points = np.cumsum(SPLIT_SIZES)[:-1].tolist()
    (aq, ak, av, iq, ik, iw, mq, mk, mv, mo, mi, mf, ga, gm) = jnp.split(proj, points, axis=-1)

    aq = partial_rope(rmsnorm(aq.reshape(bsz, seq, ATT_HEADS, ATT_HEAD_DIM), g_q_att), positions)
    ak = partial_rope(rmsnorm(ak.reshape(bsz, seq, ATT_KV_HEADS, ATT_HEAD_DIM), g_k_att), positions)
    av = av.reshape(bsz, seq, ATT_KV_HEADS, ATT_HEAD_DIM)
    iq = partial_rope(iq.reshape(bsz, seq, IDX_HEADS, IDX_DIM), positions)
    ik = partial_rope(ik, positions)
    y_att = dsa_attention(aq, ak, av, iq, ik, iw)

    qk = jax.nn.silu(causal_depthwise_conv(jnp.concatenate([mq, mk], axis=-1), w_conv_mlstm, b_conv_mlstm))
    mq, mk = jnp.split(qk, 2, axis=-1)
    shp = (bsz, seq, MLSTM_HEADS, MLSTM_HEAD_DIM)
    i_pre = mi.astype(jnp.float32) + b_igate.astype(jnp.float32)
    f_pre = mf.astype(jnp.float32) + b_fgate.astype(jnp.float32)
    hm = mlstm_chunkwise(mq.reshape(shp), mk.reshape(shp), mv.reshape(shp), i_pre, f_pre)
    hm = rmsnorm(hm, g_norm_mlstm_out.reshape(MLSTM_HEADS, MLSTM_HEAD_DIM)).reshape(bsz, seq, MLSTM_WIDTH)
    y_mlstm = jax.nn.sigmoid(mo) * hm

    merged = jax.nn.sigmoid(ga) * (y_att @ w_proj_att) + jax.nn.sigmoid(gm) * (y_mlstm @ w_proj_mlstm)
    x = x + merged @ w_out

    xn = rmsnorm(x, g_norm_moe)
    return x + hierarchical_moe(xn, w_router_group, b_router_group, w_router_expert, b_router_expert,
                                w_exp_gate, w_exp_up, w_exp_down)


def setup_inputs(seed: int = 0) -> dict:
    key = jax.random.key(seed)
    ks = jax.random.split(key, 24)
    L = DEPTH

    def nrm(k, shape, fan_in):
        return jax.random.normal(k, shape, jnp.float32) * (fan_in ** -0.5)

    def gain(k, shape):
        return 1.0 + 0.02 * jax.random.normal(k, shape, jnp.float32)

    def small(k, shape, scale):
        return scale * jax.random.normal(k, shape, jnp.float32)

    x = jax.random.normal(ks[0], (BATCH, SEQ, D_MODEL), jnp.float32)
    offsets = jax.random.randint(ks[1], (BATCH, 1), 0, 100000, dtype=jnp.int32)
    positions = (offsets + jnp.arange(SEQ, dtype=jnp.int32)[None, :]).astype(jnp.int32)
    return {
        'x': x,
        'positions': positions,
        'g_norm_mix': gain(ks[2], (L, D_MODEL)),
        'w_in': nrm(ks[3], (L, D_MODEL, D_IN_PROJ), D_MODEL),
        'g_q_att': gain(ks[4], (L, ATT_HEAD_DIM)),
        'g_k_att': gain(ks[5], (L, ATT_HEAD_DIM)),
        'w_conv_mlstm': nrm(ks[6], (L, CONV_WIDTH, 2 * MLSTM_WIDTH), CONV_WIDTH),
        'b_conv_mlstm': small(ks[7], (L, 2 * MLSTM_WIDTH), 0.01),
        'b_igate': -1.0 + small(ks[8], (L, MLSTM_HEADS), 0.1),
        'b_fgate': 3.0 + small(ks[9], (L, MLSTM_HEADS), 0.5),
        'g_norm_mlstm_out': gain(ks[10], (L, MLSTM_WIDTH)),
        'w_proj_att': nrm(ks[11], (L, ATT_Q_WIDTH, D_MODEL), ATT_Q_WIDTH),
        'w_proj_mlstm': nrm(ks[12], (L, MLSTM_WIDTH, D_MODEL), MLSTM_WIDTH),
        'w_out': nrm(ks[13], (L, D_MODEL, D_MODEL), D_MODEL),
        'g_norm_moe': gain(ks[14], (L, D_MODEL)),
        'w_router_group': nrm(ks[15], (L, D_MODEL, N_GROUPS), D_MODEL),
        'b_router_group': small(ks[16], (L, N_GROUPS), 0.01),
        'w_router_expert': nrm(ks[17], (L, D_MODEL, N_EXPERTS), D_MODEL),
        'b_router_expert': small(ks[18], (L, N_EXPERTS), 0.01),
        'w_exp_gate': nrm(ks[19], (L, N_EXPERTS, D_MODEL, D_EXPERT), D_MODEL),
        'w_exp_up': nrm(ks[20], (L, N_EXPERTS, D_MODEL, D_EXPERT), D_MODEL),
        'w_exp_down': nrm(ks[21], (L, N_EXPERTS, D_EXPERT, D_MODEL), D_EXPERT),
    }


def reference(x, positions, g_norm_mix, w_in, g_q_att, g_k_att, w_conv_mlstm, b_conv_mlstm,
              b_igate, b_fgate, g_norm_mlstm_out, w_proj_att, w_proj_mlstm, w_out,
              g_norm_moe, w_router_group, b_router_group, w_router_expert, b_router_expert,
              w_exp_gate, w_exp_up, w_exp_down):
    for l in range(DEPTH):
        x = hybrid_layer(x, positions, g_norm_mix[l], w_in[l], g_q_att[l], g_k_att[l],
                         w_conv_mlstm[l], b_conv_mlstm[l], b_igate[l], b_fgate[l],
                         g_norm_mlstm_out[l], w_proj_att[l], w_proj_mlstm[l], w_out[l],
                         g_norm_moe[l], w_router_group[l], b_router_group[l],
                         w_router_expert[l], b_router_expert[l],
                         w_exp_gate[l], w_exp_up[l], w_exp_down[l])
    return x
```

```python
import functools

import numpy as np
import jax
import jax.numpy as jnp
from jax import lax
from jax.experimental import pallas as pl
from jax.experimental.pallas import tpu as pltpu

F32 = jnp.float32
BF16 = jnp.bfloat16
I32 = jnp.int32

D_MODEL = 1024
CHUNK = 64
Q_BLOCK = 128
EPS = 1e-6
NEG_INF = -1e30

ATT_HEADS = 8
ATT_KV_HEADS = 2
ATT_HEAD_DIM = 64
ATT_Q_WIDTH = ATT_HEADS * ATT_HEAD_DIM
ATT_KV_WIDTH = ATT_KV_HEADS * ATT_HEAD_DIM
IDX_HEADS = 8
IDX_DIM = 32
TOPK_MAX = 256
ROPE_THETA = 500000.0
ROPE_FRACTION = 4

MLSTM_HEADS = 4
MLSTM_HEAD_DIM = 128
MLSTM_WIDTH = MLSTM_HEADS * MLSTM_HEAD_DIM
CONV_WIDTH = 4

N_GROUPS = 4
EXPERTS_PER_GROUP = 8
N_EXPERTS = N_GROUPS * EXPERTS_PER_GROUP
TOP_K_EXPERTS = 2
D_EXPERT = 512

LANES = 128
SUBLANES = 8
VMEM_LIMIT_BYTES = 56 * 1024 * 1024

SMALL_IK = 0
SMALL_IW = IDX_DIM
SMALL_MI = SMALL_IW + IDX_HEADS
SMALL_MF = SMALL_MI + MLSTM_HEADS

COL_AQ = 0
COL_AK = COL_AQ + ATT_Q_WIDTH
COL_AV = COL_AK + ATT_KV_WIDTH
COL_IQ = COL_AV + ATT_KV_WIDTH
COL_SMALL = COL_IQ + IDX_HEADS * IDX_DIM
COL_MQK = COL_SMALL + LANES
COL_MV = COL_MQK + 2 * MLSTM_WIDTH
COL_MO = COL_MV + MLSTM_WIDTH
COL_GA = COL_MO + MLSTM_WIDTH
COL_GM = COL_GA + D_MODEL
COL_END = COL_GM + D_MODEL

IN_PROJ_ROWS = 512
DSA_KEY_TILE = 512
MLSTM_CHUNK = 256
MERGE_ROWS = 512
EXPERT_ROWS = 256
COMBINE_ROWS = 256

_INT_MIN = -(2 ** 31)


def _sortable_key_const(v):
    b = int(np.array(v, np.float32).view(np.int32))
    return b if b >= 0 else b ^ 0x7FFFFFFF


_KEY_NEG_INF = _sortable_key_const(NEG_INF)
_KEY_VALID = _sortable_key_const(NEG_INF * 0.5)


def _split3(a):
    hi = a.astype(BF16)
    r1 = a - hi.astype(F32)
    mid = r1.astype(BF16)
    lo = (r1 - mid.astype(F32)).astype(BF16)
    return hi, mid, lo


def _dot_exact_rhs(a, b_bf16):
    hi, mid, lo = _split3(a)
    d = functools.partial(jnp.dot, preferred_element_type=F32)
    return d(hi, b_bf16) + d(mid, b_bf16) + d(lo, b_bf16)


def _dot_exact_lhs(a_bf16, b):
    hi, mid, lo = _split3(b)
    d = functools.partial(jnp.dot, preferred_element_type=F32)
    return d(a_bf16, hi) + d(a_bf16, mid) + d(a_bf16, lo)


def _dot_nt(a, b):
    return lax.dot_general(a, b, (((1,), (1,)), ((), ())), preferred_element_type=F32)


def _sigmoid(x):
    return 1.0 / (1.0 + jnp.exp(-x))


def _in_proj_body(x_ref, pos_ref, g_ref, w_ref, rc_ref, bd_ref, gq_ref, gk_ref,
                  aq_ref, ak_ref, av_ref, iq_ref, ik_ref, small_ref, mqk_ref, mv_ref,
                  og_ref, ga_ref, gm_ref):
    x = x_ref[...]
    ms = jnp.mean(x * x, axis=-1, keepdims=True)
    h = (x * lax.rsqrt(ms + EPS) * g_ref[...]).astype(BF16)
    posf = pos_ref[...].astype(F32)

    def proj(lo, hi):
        return jnp.dot(h, w_ref[:, lo:hi], preferred_element_type=F32)

    def rope(t, cos, s_up, s_dn, half):
        n = t.shape[-1]
        return t * cos + pltpu.roll(t, half, 1) * s_up + pltpu.roll(t, n - half, 1) * s_dn

    def head_norm(t, gain):
        ssum = _dot_exact_rhs(t * t, bd_ref[...])
        return t * lax.rsqrt(ssum * (1.0 / ATT_HEAD_DIM) + EPS) * gain

    ang_a = posf * rc_ref[0:1, :]
    cos_a, sin_a = jnp.cos(ang_a), jnp.sin(ang_a)
    up_a, dn_a = sin_a * rc_ref[1:2, :], sin_a * rc_ref[2:3, :]
    ang_i = posf * rc_ref[3:4, :]
    cos_i, sin_i = jnp.cos(ang_i), jnp.sin(ang_i)
    up_i, dn_i = sin_i * rc_ref[4:5, :], sin_i * rc_ref[5:6, :]
    small_mask = rc_ref[6:7, :]
    cos_s = 1.0 + (cos_i - 1.0) * small_mask
    up_s, dn_s = up_i * small_mask, dn_i * small_mask

    att_half = ATT_HEAD_DIM // ROPE_FRACTION // 2
    idx_half = IDX_DIM // ROPE_FRACTION // 2
    att_scale = ATT_HEAD_DIM ** -0.5

    for c in range(ATT_Q_WIDTH // LANES):
        t = proj(COL_AQ + c * LANES, COL_AQ + (c + 1) * LANES)
        t = rope(head_norm(t, gq_ref[...]), cos_a, up_a, dn_a, att_half)
        aq_ref[:, c * LANES:(c + 1) * LANES] = (t * att_scale).astype(BF16)
    t = proj(COL_AK, COL_AK + LANES)
    ak_ref[...] = rope(head_norm(t, gk_ref[...]), cos_a, up_a, dn_a, att_half).astype(BF16)
    av_ref[...] = proj(COL_AV, COL_AV + LANES).astype(BF16)
    for c in range(IDX_HEADS * IDX_DIM // LANES):
        t = proj(COL_IQ + c * LANES, COL_IQ + (c + 1) * LANES)
        iq_ref[:, c * LANES:(c + 1) * LANES] = rope(t, cos_i, up_i, dn_i, idx_half).astype(BF16)
    t = rope(proj(COL_SMALL, COL_SMALL + LANES), cos_s, up_s, dn_s, idx_half)
    small_ref[...] = t
    ik_ref[...] = t[:, SMALL_IK:SMALL_IK + IDX_DIM].astype(BF16)
    for c in range(2 * MLSTM_WIDTH // 512):
        mqk_ref[:, c * 512:(c + 1) * 512] = proj(COL_MQK + c * 512, COL_MQK + (c + 1) * 512)
    mv_ref[...] = proj(COL_MV, COL_MV + MLSTM_WIDTH).astype(BF16)
    og_ref[...] = _sigmoid(proj(COL_MO, COL_MO + MLSTM_WIDTH)).astype(BF16)
    for c in range(D_MODEL // 512):
        ga_ref[:, c * 512:(c + 1) * 512] = _sigmoid(
            proj(COL_GA + c * 512, COL_GA + (c + 1) * 512)).astype(BF16)
        gm_ref[:, c * 512:(c + 1) * 512] = _sigmoid(
            proj(COL_GM + c * 512, COL_GM + (c + 1) * 512)).astype(BF16)


def _rope_consts():
    lane = np.arange(LANES)

    def inv_freq(rot):
        half = rot // 2
        return jnp.float32(ROPE_THETA) ** (-jnp.arange(half, dtype=F32) * 2.0 / rot)

    rot_a = ATT_HEAD_DIM // ROPE_FRACTION
    rot_i = IDX_DIM // ROPE_FRACTION
    fa, fi = inv_freq(rot_a), inv_freq(rot_i)
    ja, ji = lane % ATT_HEAD_DIM, lane % IDX_DIM
    rc = jnp.zeros((16, LANES), F32)
    rc = rc.at[0].set(jnp.where(ja < rot_a, fa[ja % (rot_a // 2)], 0.0))
    rc = rc.at[1].set(jnp.asarray(((ja >= rot_a // 2) & (ja < rot_a)).astype(np.float32)))
    rc = rc.at[2].set(jnp.asarray(-(ja < rot_a // 2).astype(np.float32)))
    rc = rc.at[3].set(jnp.where(ji < rot_i, fi[ji % (rot_i // 2)], 0.0))
    rc = rc.at[4].set(jnp.asarray(((ji >= rot_i // 2) & (ji < rot_i)).astype(np.float32)))
    rc = rc.at[5].set(jnp.asarray(-(ji < rot_i // 2).astype(np.float32)))
    rc = rc.at[6].set(jnp.asarray((lane < IDX_DIM).astype(np.float32)))
    return rc


def _pack_w_in(w_in):
    sizes = (ATT_Q_WIDTH, ATT_KV_WIDTH, ATT_KV_WIDTH, IDX_HEADS * IDX_DIM, IDX_DIM, IDX_HEADS,
             MLSTM_WIDTH, MLSTM_WIDTH, MLSTM_WIDTH, MLSTM_WIDTH, MLSTM_HEADS, MLSTM_HEADS,
             D_MODEL, D_MODEL)
    pts = np.cumsum(sizes)[:-1].tolist()
    (aq, ak, av, iq, ik, iw, mq, mk, mv, mo, mi, mf, ga, gm) = jnp.split(w_in, pts, axis=-1)
    pad = jnp.zeros((D_MODEL, LANES - (SMALL_MF + MLSTM_HEADS)), w_in.dtype)
    small = jnp.concatenate([ik, iw, mi, mf, pad], axis=-1)
    return jnp.concatenate([aq, ak, av, iq, small, mq, mk, mv, mo, ga, gm], axis=-1).astype(BF16)


def _in_proj(x2, pos2, g_norm_mix, w_in, g_q_att, g_k_att):
    t_rows = x2.shape[0]
    tm = min(IN_PROJ_ROWS, t_rows)
    w = _pack_w_in(w_in)
    lane = np.arange(LANES)
    bd = jnp.asarray((lane[:, None] // ATT_HEAD_DIM == lane[None, :] // ATT_HEAD_DIM)
                     .astype(np.float32)).astype(BF16)
    reps = LANES // ATT_HEAD_DIM
    gq = jnp.tile(g_q_att.astype(F32), reps)[None, :]
    gk = jnp.tile(g_k_att.astype(F32), reps)[None, :]

    def row(width):
        return pl.BlockSpec((tm, width), lambda i: (i, 0))

    def const(shape):
        return pl.BlockSpec(shape, lambda i: (0, 0))

    out_shapes = (
        jax.ShapeDtypeStruct((t_rows, ATT_Q_WIDTH), BF16),
        jax.ShapeDtypeStruct((t_rows, ATT_KV_WIDTH), BF16),
        jax.ShapeDtypeStruct((t_rows, ATT_KV_WIDTH), BF16),
        jax.ShapeDtypeStruct((t_rows, IDX_HEADS * IDX_DIM), BF16),
        jax.ShapeDtypeStruct((t_rows, IDX_DIM), BF16),
        jax.ShapeDtypeStruct((t_rows, LANES), F32),
        jax.ShapeDtypeStruct((t_rows, 2 * MLSTM_WIDTH), F32),
        jax.ShapeDtypeStruct((t_rows, MLSTM_WIDTH), BF16),
        jax.ShapeDtypeStruct((t_rows, MLSTM_WIDTH), BF16),
        jax.ShapeDtypeStruct((t_rows, D_MODEL), BF16),
        jax.ShapeDtypeStruct((t_rows, D_MODEL), BF16),
    )
    return pl.pallas_call(
        _in_proj_body,
        grid=(t_rows // tm,),
        in_specs=[row(D_MODEL), row(1), const((1, D_MODEL)), const((D_MODEL, COL_END)),
                  const((16, LANES)), const((LANES, LANES)), const((1, LANES)), const((1, LANES))],
        out_specs=[row(s.shape[1]) for s in out_shapes],
        out_shape=out_shapes,
        compiler_params=pltpu.CompilerParams(
            dimension_semantics=("arbitrary",), vmem_limit_bytes=VMEM_LIMIT_BYTES),
        name="in_proj",
    )(x2, pos2, g_norm_mix.astype(F32)[None, :], w, _rope_consts(), bd, gq, gk)


def _dsa_body(aq_ref, iq_ref, small_ref, k_ref, v_ref, ik_ref, out_ref,
              key_scr, qi_scr, q2_scr, p_scr, acc_scr, m_scr, l_scr, *, topk, tk):
    blk = pl.program_id(1)
    nq = Q_BLOCK
    rep = ATT_HEADS // ATT_KV_HEADS
    n_tiles = lax.shift_right_logical(blk * nq + nq + tk - 1, int(np.log2(tk)))
    idx_scale = (IDX_DIM * IDX_HEADS) ** -0.5

    for h in range(IDX_HEADS):
        qi_scr[h * nq:(h + 1) * nq, :] = iq_ref[:, h * IDX_DIM:(h + 1) * IDX_DIM]
    w_stack = jnp.concatenate(
        [small_ref[:, SMALL_IW + h:SMALL_IW + h + 1] for h in range(IDX_HEADS)], axis=0) * idx_scale

    row_id = lax.broadcasted_iota(I32, (nq, 1), 0)
    q_chunk = lax.shift_right_logical(blk * nq + row_id, int(np.log2(CHUNK)))
    lane_id = lax.broadcasted_iota(I32, (1, tk), 1)

    def score_tile(t, carry):
        start = pl.multiple_of(t * tk, tk)
        rel = _dot_nt(qi_scr[...], ik_ref[pl.ds(start, tk), :])
        rel = jnp.maximum(rel, 0.0) * w_stack
        sc = rel[0:nq]
        for h in range(1, IDX_HEADS):
            sc = sc + rel[h * nq:(h + 1) * nq]
        visible = lax.shift_right_logical(start + lane_id, int(np.log2(CHUNK))) <= q_chunk
        sc = jnp.where(visible, sc, NEG_INF)
        bits = lax.bitcast_convert_type(sc, I32)
        key_scr[t] = jnp.where(bits < 0, bits ^ 0x7FFFFFFF, bits)
        return carry

    lax.fori_loop(0, n_tiles, score_tile, 0)

    def count_ge(cand_s):
        def body(t, acc):
            hit = jnp.where(key_scr[t] >= cand_s, 1, 0)
            for c in range(tk // LANES):
                acc = acc + hit[:, c * LANES:(c + 1) * LANES]
            return acc
        acc = lax.fori_loop(0, n_tiles, body, jnp.zeros((nq, LANES), I32))
        return jnp.sum(acc.astype(F32), axis=-1, keepdims=True)

    def bit_body(i, carry):
        thr_u, cge = carry
        cand_u = thr_u | lax.shift_left(jnp.int32(1), 31 - i)
        cnt = count_ge(cand_u ^ _INT_MIN)
        take = cnt >= topk
        return jnp.where(take, cand_u, thr_u), jnp.where(take, cnt, cge)

    thr_u, cge = lax.fori_loop(
        0, 32, bit_body,
        (jnp.zeros((nq, 1), I32), jnp.full((nq, 1), n_tiles * tk, I32).astype(F32)))
    thr = thr_u ^ _INT_MIN

    q2_scr[...] = jnp.zeros_like(q2_scr)
    for h in range(ATT_HEADS):
        g = h // rep
        q2_scr[h * nq:(h + 1) * nq, g * ATT_HEAD_DIM:(g + 1) * ATT_HEAD_DIM] = (
            aq_ref[:, h * ATT_HEAD_DIM:(h + 1) * ATT_HEAD_DIM])

    def attend(with_ties):
        m_scr[...] = jnp.full_like(m_scr, NEG_INF)
        l_scr[...] = jnp.zeros_like(l_scr)
        acc_scr[...] = jnp.zeros_like(acc_scr)
        if with_ties:
            quota = topk - count_ge(thr + 1)
            rr = lax.broadcasted_iota(I32, (tk, tk), 0)
            cc = lax.broadcasted_iota(I32, (tk, tk), 1)
            before = jnp.where(rr < cc, 1.0, 0.0).astype(BF16)

        def tile(t, eq_seen):
            start = pl.multiple_of(t * tk, tk)
            key = key_scr[t]
            valid = key > _KEY_VALID
            if with_ties:
                eq = key == thr
                eqf = jnp.where(eq, 1.0, 0.0)
                rank = jnp.dot(eqf.astype(BF16), before, preferred_element_type=F32) + eq_seen
                sel = valid & ((key > thr) | (eq & (rank < quota)))
                eq_seen = eq_seen + jnp.sum(eqf, axis=-1, keepdims=True)
            else:
                sel = valid & (key >= thr)
            s = _dot_nt(q2_scr[...], k_ref[pl.ds(start, tk), :])
            for h in range(ATT_HEADS):
                rows = slice(h * nq, (h + 1) * nq)
                sh = s[rows]
                m_old = m_scr[rows]
                m_new = jnp.maximum(
                    m_old, jnp.max(jnp.where(sel, sh, NEG_INF), axis=-1, keepdims=True))
                p = jnp.where(sel, jnp.exp(sh - m_new), 0.0)
                alpha = jnp.exp(m_old - m_new)
                l_scr[rows] = alpha * l_scr[rows] + jnp.sum(p, axis=-1, keepdims=True)
                acc_scr[rows] = alpha * acc_scr[rows]
                p_scr[rows] = p.astype(BF16)
                m_scr[rows] = m_new
            acc_scr[...] += jnp.dot(p_scr[...], v_ref[pl.ds(start, tk), :],
                                    preferred_element_type=F32)
            return eq_seen

        lax.fori_loop(0, n_tiles, tile, jnp.zeros((nq, 1), F32))
        for h in range(ATT_HEADS):
            g = h // rep
            rows = slice(h * nq, (h + 1) * nq)
            o = acc_scr[rows, g * ATT_HEAD_DIM:(g + 1) * ATT_HEAD_DIM] / l_scr[rows]
            out_ref[:, h * ATT_HEAD_DIM:(h + 1) * ATT_HEAD_DIM] = o.astype(BF16)

    tie_rows = jnp.where((cge > topk) & (thr > _KEY_NEG_INF), 1.0, 0.0)
    need_ties = jnp.max(tie_rows) > 0.5

    @pl.when(need_ties)
    def _():
        attend(True)

    @pl.when(jnp.logical_not(need_ties))
    def _():
        attend(False)


def _dsa(aq, iq, small, ak, av, ik, bsz, seq):
    nb = seq // Q_BLOCK
    tk = min(DSA_KEY_TILE, seq)
    topk = min(TOPK_MAX, seq // 4)
    assert seq % tk == 0 and tk >= topk and tk & (tk - 1) == 0

    def qspec(width):
        return pl.BlockSpec((Q_BLOCK, width), lambda b, j: (b * nb + j, 0))

    def kspec(width):
        return pl.BlockSpec((seq, width), lambda b, j: (b, 0))

    rows = ATT_HEADS * Q_BLOCK
    return pl.pallas_call(
        functools.partial(_dsa_body, topk=topk, tk=tk),
        grid=(bsz, nb),
        in_specs=[qspec(ATT_Q_WIDTH), qspec(IDX_HEADS * IDX_DIM), qspec(LANES),
                  kspec(ATT_KV_WIDTH), kspec(ATT_KV_WIDTH), kspec(IDX_DIM)],
        out_specs=qspec(ATT_Q_WIDTH),
        out_shape=jax.ShapeDtypeStruct((bsz * seq, ATT_Q_WIDTH), BF16),
        scratch_shapes=[
            pltpu.VMEM((seq // tk, Q_BLOCK, tk), I32),
            pltpu.VMEM((rows, IDX_DIM), BF16),
            pltpu.VMEM((rows, ATT_KV_WIDTH), BF16),
            pltpu.VMEM((rows, tk), BF16),
            pltpu.VMEM((rows, ATT_KV_WIDTH), F32),
            pltpu.VMEM((rows, 1), F32),
            pltpu.VMEM((rows, 1), F32),
        ],
        compiler_params=pltpu.CompilerParams(
            dimension_semantics=("arbitrary", "arbitrary"), vmem_limit_bytes=VMEM_LIMIT_BYTES),
        name="dsa_attention",
    )(aq, iq, small, ak, av, ik)


def _mlstm_body(mqk_ref, mv_ref, og_ref, small_ref, cw_ref, cb_ref, gb_ref, gn_ref, y_ref,
                ubuf, c_scr, n_scr, m_scr, *, chunk):
    ln = chunk
    d = MLSTM_HEAD_DIM
    halo = SUBLANES

    @pl.when(pl.program_id(1) == 0)
    def _():
        ubuf[0:halo, :] = jnp.zeros((halo, 2 * MLSTM_WIDTH), F32)
        c_scr[...] = jnp.zeros_like(c_scr)
        n_scr[...] = jnp.zeros_like(n_scr)
        m_scr[...] = jnp.zeros_like(m_scr)

    ubuf[halo:halo + ln, :] = mqk_ref[...]
    conv = cb_ref[...]
    for j in range(CONV_WIDTH):
        off = halo - (CONV_WIDTH - 1) + j
        conv = conv + cw_ref[j:j + 1, :] * ubuf[off:off + ln, :]
    ubuf[0:halo, :] = ubuf[ln:ln + halo, :]
    qk = conv * _sigmoid(conv)
    q_all = qk[:, :MLSTM_WIDTH]
    k_all = qk[:, MLSTM_WIDTH:] * (d ** -0.5)

    sm = small_ref[...] + gb_ref[...]
    lf = jnp.minimum(sm, 0.0) - jnp.log(1.0 + jnp.exp(-jnp.abs(sm)))
    sm_t = sm.T
    lf_t = lf.T
    rr = lax.broadcasted_iota(I32, (ln, ln), 0)
    cc = lax.broadcasted_iota(I32, (ln, ln), 1)
    causal = cc <= rr
    tri = jnp.where(causal, 1.0, 0.0).astype(BF16)
    tri_t = jnp.where(rr <= cc, 1.0, 0.0).astype(BF16)
    b_cols = _dot_exact_lhs(tri, lf)
    b_rows = _dot_exact_rhs(lf_t, tri_t)

    for h in range(MLSTM_HEADS):
        hs = slice(h * d, (h + 1) * d)
        ig_col = sm[:, SMALL_MI + h:SMALL_MI + h + 1]
        ig_row = sm_t[SMALL_MI + h:SMALL_MI + h + 1, :]
        b_col = b_cols[:, SMALL_MF + h:SMALL_MF + h + 1]
        b_row = b_rows[SMALL_MF + h:SMALL_MF + h + 1, :]
        m_st = m_scr[h:h + 1, 0:1]
        c_st = c_scr[h]
        n_st = n_scr[h:h + 1, :]

        dmat = jnp.where(causal, b_col - b_row + ig_row, NEG_INF)
        inter = b_col + m_st
        m_t = jnp.maximum(inter, jnp.max(dmat, axis=-1, keepdims=True))
        w_intra = jnp.exp(dmat - m_t)
        w_inter = jnp.exp(inter - m_t)
        qf = q_all[:, hs]
        kf = k_all[:, hs]
        qh = qf.astype(BF16)
        vh = mv_ref[:, hs]
        s = _dot_nt(qh, kf.astype(BF16)) * w_intra
        num = (w_inter * jnp.dot(qh, c_st.astype(BF16), preferred_element_type=F32)
               + jnp.dot(s.astype(BF16), vh, preferred_element_type=F32))
        den = (w_inter * jnp.sum(qf * n_st, axis=-1, keepdims=True)
               + jnp.sum(s, axis=-1, keepdims=True))
        hh = num / jnp.maximum(jnp.abs(den), jnp.exp(-m_t))

        b_last = b_col[ln - 1:ln, :]
        gdec = b_last - b_col + ig_col
        m_new = jnp.maximum(b_last + m_st, jnp.max(gdec, axis=0, keepdims=True))
        decay = jnp.exp(b_last + m_st - m_new)
        kw = kf * jnp.exp(gdec - m_new)
        c_scr[h] = decay * c_st + jnp.dot(kw.T.astype(BF16), vh, preferred_element_type=F32)
        n_scr[h:h + 1, :] = decay * n_st + jnp.sum(kw, axis=0, keepdims=True)
        m_scr[h:h + 1, :] = jnp.broadcast_to(m_new, (1, LANES))

        ms = jnp.mean(hh * hh, axis=-1, keepdims=True)
        yh = hh * lax.rsqrt(ms + EPS) * gn_ref[:, hs]
        y_ref[:, hs] = (og_ref[:, hs].astype(F32) * yh).astype(BF16)


def _mlstm(mqk, mv, og, small, w_conv, b_conv, b_igate, b_fgate, g_norm_out, bsz, seq):
    ln = min(MLSTM_CHUNK, seq)
    nc = seq // ln
    gbias = jnp.zeros((1, LANES), F32)
    gbias = gbias.at[0, SMALL_MI:SMALL_MI + MLSTM_HEADS].set(b_igate.astype(F32))
    gbias = gbias.at[0, SMALL_MF:SMALL_MF + MLSTM_HEADS].set(b_fgate.astype(F32))

    def row(width):
        return pl.BlockSpec((ln, width), lambda b, c: (b * nc + c, 0))

    def const(shape):
        return pl.BlockSpec(shape, lambda b, c: (0, 0))

    return pl.pallas_call(
        functools.partial(_mlstm_body, chunk=ln),
        grid=(bsz, nc),
        in_specs=[row(2 * MLSTM_WIDTH), row(MLSTM_WIDTH), row(MLSTM_WIDTH), row(LANES),
                  const((CONV_WIDTH, 2 * MLSTM_WIDTH)), const((1, 2 * MLSTM_WIDTH)),
                  const((1, LANES)), const((1, MLSTM_WIDTH))],
        out_specs=row(MLSTM_WIDTH),
        out_shape=jax.ShapeDtypeStruct((bsz * seq, MLSTM_WIDTH), BF16),
        scratch_shapes=[
            pltpu.VMEM((ln + SUBLANES, 2 * MLSTM_WIDTH), F32),
            pltpu.VMEM((MLSTM_HEADS, MLSTM_HEAD_DIM, MLSTM_HEAD_DIM), F32),
            pltpu.VMEM((SUBLANES, MLSTM_HEAD_DIM), F32),
            pltpu.VMEM((SUBLANES, LANES), F32),
        ],
        compiler_params=pltpu.CompilerParams(
            dimension_semantics=("arbitrary", "arbitrary"), vmem_limit_bytes=VMEM_LIMIT_BYTES),
        name="mlstm",
    )(mqk, mv, og, small, w_conv.astype(F32), b_conv.astype(F32)[None, :], gbias,
      g_norm_out.astype(F32)[None, :])


def _merge_body(x_ref, ya_ref, ym_ref, ga_ref, gm_ref, wpa_ref, wpm_ref, wo_ref, gmoe_ref,
                wr_ref, br_ref, x1_ref, xn_ref, ri_ref, rf_ref):
    pa = jnp.dot(ya_ref[...], wpa_ref[...], preferred_element_type=F32)
    pm = jnp.dot(ym_ref[...], wpm_ref[...], preferred_element_type=F32)
    merged = ga_ref[...].astype(F32) * pa + gm_ref[...].astype(F32) * pm
    x1 = x_ref[...] + jnp.dot(merged.astype(BF16), wo_ref[...], preferred_element_type=F32)
    x1_ref[...] = x1
    ms = jnp.mean(x1 * x1, axis=-1, keepdims=True)
    xn = x1 * lax.rsqrt(ms + EPS) * gmoe_ref[...]
    xn_ref[...] = xn

    logits = jnp.dot(xn, wr_ref[...], preferred_element_type=F32,
                     precision=lax.Precision.HIGHEST) + br_ref[...]
    lane = lax.broadcasted_iota(I32, logits.shape, 1).astype(F32)

    def first_argmax(vals, mask):
        v = jnp.where(mask, vals, -jnp.inf)
        top = jnp.max(v, axis=-1, keepdims=True)
        idx = jnp.min(jnp.where(mask & (v == top), lane, float(LANES)), axis=-1, keepdims=True)
        return top, idx

    is_group = lane < N_GROUPS
    g_max, g_sel = first_argmax(logits, is_group)
    g_den = jnp.sum(jnp.where(is_group, jnp.exp(logits - g_max), 0.0), axis=-1, keepdims=True)
    g_top = 1.0 / g_den
    lo = N_GROUPS + g_sel * EXPERTS_PER_GROUP
    in_group = (lane >= lo) & (lane < lo + EXPERTS_PER_GROUP)
    v1, i1 = first_argmax(logits, in_group)
    v2, i2 = first_argmax(logits, in_group & (lane != i1))
    e2 = jnp.exp(v2 - v1)
    p1 = 1.0 / (1.0 + e2)
    gate1 = p1 * g_top
    gate2 = (e2 * p1) * g_top
    ri_ref[...] = jnp.where(lane == 0, i1 - N_GROUPS,
                            jnp.where(lane == 1, i2 - N_GROUPS, 0.0)).astype(I32)
    rf_ref[...] = jnp.where(lane == 0, gate1, jnp.where(lane == 1, gate2, 0.0))


def _merge(x2, y_att, y_ml, ga, gm, w_proj_att, w_proj_mlstm, w_out, g_norm_moe,
           w_router_group, b_router_group, w_router_expert, b_router_expert):
    t_rows = x2.shape[0]
    tm = min(MERGE_ROWS, t_rows)
    n_r = N_GROUPS + N_EXPERTS
    wr = jnp.concatenate([w_router_group, w_router_expert,
                          jnp.zeros((D_MODEL, LANES - n_r), F32)], axis=-1).astype(F32)
    br = jnp.concatenate([b_router_group, b_router_expert,
                          jnp.zeros((LANES - n_r,), F32)]).astype(F32)[None, :]

    def row(width):
        return pl.BlockSpec((tm, width), lambda i: (i, 0))

    def const(shape):
        return pl.BlockSpec(shape, lambda i: (0, 0))

    out_shapes = (
        jax.ShapeDtypeStruct((t_rows, D_MODEL), F32),
        jax.ShapeDtypeStruct((t_rows, D_MODEL), F32),
        jax.ShapeDtypeStruct((t_rows, LANES), I32),
        jax.ShapeDtypeStruct((t_rows, LANES), F32),
    )
    return pl.pallas_call(
        _merge_body,
        grid=(t_rows // tm,),
        in_specs=[row(D_MODEL), row(ATT_Q_WIDTH), row(MLSTM_WIDTH), row(D_MODEL), row(D_MODEL),
                  const((ATT_Q_WIDTH, D_MODEL)), const((MLSTM_WIDTH, D_MODEL)),
                  const((D_MODEL, D_MODEL)), const((1, D_MODEL)), const((D_MODEL, LANES)),
                  const((1, LANES))],
        out_specs=[row(s.shape[1]) for s in out_shapes],
        out_shape=out_shapes,
        compiler_params=pltpu.CompilerParams(
            dimension_semantics=("arbitrary",), vmem_limit_bytes=VMEM_LIMIT_BYTES),
        name="merge_route",
    )(x2, y_att, y_ml, ga, gm, w_proj_att.astype(BF16), w_proj_mlstm.astype(BF16),
      w_out.astype(BF16), g_norm_moe.astype(F32)[None, :], wr, br)


def _expert_body(be_ref, bv_ref, tok_cur_ref, tok_nxt_ref, xn_hbm, wg_ref, wu_ref, wd_ref,
                 ys_ref, xbuf, sem, *, rows):
    i = pl.program_id(0)
    nblk = pl.num_programs(0)
    slot = lax.rem(i, 2)

    def row_copy(tok, dst_slot, r):
        return pltpu.make_async_copy(
            xn_hbm.at[pl.ds(tok, 1), :], xbuf.at[dst_slot, pl.ds(r, 1), :], sem.at[dst_slot])

    def gather(tok_ref, dst_slot):
        def body(r, carry):
            row_copy(tok_ref[0, 0, r], dst_slot, r).start()
            return carry
        lax.fori_loop(0, rows, body, 0)

    @pl.when((i == 0) & (bv_ref[0] > 0))
    def _():
        gather(tok_cur_ref, 0)

    nxt = jnp.minimum(i + 1, nblk - 1)

    @pl.when((i + 1 < nblk) & (bv_ref[nxt] > 0))
    def _():
        gather(tok_nxt_ref, 1 - slot)

    @pl.when(bv_ref[i] > 0)
    def _():
        def wait_body(r, carry):
            row_copy(0, slot, r).wait()
            return carry
        lax.fori_loop(0, rows, wait_body, 0)
        xb = xbuf[slot].astype(BF16)
        a = jnp.dot(xb, wg_ref[0], preferred_element_type=F32)
        u = jnp.dot(xb, wu_ref[0], preferred_element_type=F32)
        hid = (a * _sigmoid(a) * u).astype(BF16)
        ys_ref[...] = jnp.dot(hid, wd_ref[0], preferred_element_type=F32)

    @pl.when(bv_ref[i] == 0)
    def _():
        ys_ref[...] = jnp.zeros_like(ys_ref)


def _experts(xn, buf_tok, block_e, block_valid, w_gate, w_up, w_down):
    rows = EXPERT_ROWS
    cap = buf_tok.shape[0]
    nblk = cap // rows
    tok3 = buf_tok.reshape(nblk, 1, rows)
    grid_spec = pltpu.PrefetchScalarGridSpec(
        num_scalar_prefetch=2,
        grid=(nblk,),
        in_specs=[
            pl.BlockSpec((1, 1, rows), lambda i, be, bv: (i, 0, 0), memory_space=pltpu.SMEM),
            pl.BlockSpec((1, 1, rows), lambda i, be, bv: (jnp.minimum(i + 1, nblk - 1), 0, 0),
                         memory_space=pltpu.SMEM),
            pl.BlockSpec(memory_space=pl.ANY),
            pl.BlockSpec((1, D_MODEL, D_EXPERT), lambda i, be, bv: (be[i], 0, 0)),
            pl.BlockSpec((1, D_MODEL, D_EXPERT), lambda i, be, bv: (be[i], 0, 0)),
            pl.BlockSpec((1, D_EXPERT, D_MODEL), lambda i, be, bv: (be[i], 0, 0)),
        ],
        out_specs=pl.BlockSpec((rows, D_MODEL), lambda i, be, bv: (i, 0)),
        scratch_shapes=[pltpu.VMEM((2, rows, D_MODEL), F32), pltpu.SemaphoreType.DMA((2,))],
    )
    return pl.pallas_call(
        functools.partial(_expert_body, rows=rows),
        grid_spec=grid_spec,
        out_shape=jax.ShapeDtypeStruct((cap, D_MODEL), F32),
        compiler_params=pltpu.CompilerParams(
            dimension_semantics=("arbitrary",), vmem_limit_bytes=VMEM_LIMIT_BYTES),
        name="moe_experts",
    )(block_e, block_valid, tok3, tok3, xn, w_gate.astype(BF16), w_up.astype(BF16),
      w_down.astype(BF16))


def _combine_body(dest_ref, x1_ref, rf_ref, ys_hbm, out_ref, ybuf, sem, *, rows):
    def row_copy(src, k, r):
        return pltpu.make_async_copy(
            ys_hbm.at[pl.ds(src, 1), :], ybuf.at[k, pl.ds(r, 1), :], sem.at[0])

    def issue(r, carry):
        for k in range(TOP_K_EXPERTS):
            row_copy(dest_ref[0, 0, TOP_K_EXPERTS * r + k], k, r).start()
        return carry

    lax.fori_loop(0, rows, issue, 0)

    def drain(r, carry):
        for k in range(TOP_K_EXPERTS):
            row_copy(0, k, r).wait()
        return carry

    lax.fori_loop(0, rows, drain, 0)
    gates = rf_ref[...]
    out = x1_ref[...]
    for k in range(TOP_K_EXPERTS):
        out = out + gates[:, k:k + 1] * ybuf[k]
    out_ref[...] = out


def _combine(x1, rf, ys, dest):
    t_rows = x1.shape[0]
    rows = min(COMBINE_ROWS, t_rows)
    nt = t_rows // rows
    dest3 = dest.reshape(nt, 1, rows * TOP_K_EXPERTS)
    return pl.pallas_call(
        functools.partial(_combine_body, rows=rows),
        grid=(nt,),
        in_specs=[
            pl.BlockSpec((1, 1, rows * TOP_K_EXPERTS), lambda i: (i, 0, 0),
                         memory_space=pltpu.SMEM),
            pl.BlockSpec((rows, D_MODEL), lambda i: (i, 0)),
            pl.BlockSpec((rows, LANES), lambda i: (i, 0)),
            pl.BlockSpec(memory_space=pl.ANY),
        ],
        out_specs=pl.BlockSpec((rows, D_MODEL), lambda i: (i, 0)),
        out_shape=jax.ShapeDtypeStruct((t_rows, D_MODEL), F32),
        scratch_shapes=[pltpu.VMEM((TOP_K_EXPERTS, rows, D_MODEL), F32),
                        pltpu.SemaphoreType.DMA((1,))],
        compiler_params=pltpu.CompilerParams(
            dimension_semantics=("arbitrary",), vmem_limit_bytes=VMEM_LIMIT_BYTES),
        name="moe_combine",
    )(dest3, x1, rf, ys)


def _route_plan(expert_id):
    t_rows = expert_id.shape[0]
    m = t_rows * TOP_K_EXPERTS
    rows = EXPERT_ROWS
    e_flat = expert_id.reshape(m)
    order = jnp.argsort(e_flat, stable=True).astype(I32)
    counts = jnp.sum((e_flat[:, None] == jnp.arange(N_EXPERTS, dtype=I32)[None, :]).astype(I32), axis=0)
    padded = ((counts + rows - 1) // rows) * rows
    starts = jnp.cumsum(counts) - counts
    pends = jnp.cumsum(padded)
    pstarts = pends - padded
    cap = ((m + N_EXPERTS * rows + rows - 1) // rows) * rows
    nblk = cap // rows
    blk_start = jnp.arange(nblk, dtype=I32) * rows
    block_e = jnp.minimum(jnp.searchsorted(pends, blk_start, side='right'), N_EXPERTS - 1).astype(I32)
    block_valid = (blk_start < pends[-1]).astype(I32)
    slot = jnp.arange(cap, dtype=I32)
    slot_e = jnp.repeat(block_e, rows)
    off = slot - pstarts[slot_e]
    live = (off < counts[slot_e]) & (jnp.repeat(block_valid, rows) > 0)
    src = jnp.clip(starts[slot_e] + off, 0, m - 1)
    buf_tok = jnp.where(live, order[src] // TOP_K_EXPERTS, 0).astype(I32)
    rank_sorted = jnp.arange(m, dtype=I32) - starts[e_flat[order]]
    dest_sorted = pstarts[e_flat[order]] + rank_sorted
    dest = dest_sorted[jnp.argsort(order)].astype(I32)
    return buf_tok, block_e, block_valid, dest


def kernel(x, positions, g_norm_mix, w_in, g_q_att, g_k_att, w_conv_mlstm, b_conv_mlstm, b_igate,
           b_fgate, g_norm_mlstm_out, w_proj_att, w_proj_mlstm, w_out, g_norm_moe, w_router_group,
           b_router_group, w_router_expert, b_router_expert, w_exp_gate, w_exp_up, w_exp_down):
    bsz, seq, dm = x.shape
    depth = g_norm_mix.shape[0]
    x2 = x.reshape(bsz * seq, dm)
    pos2 = positions.reshape(bsz * seq, 1).astype(I32)
    for l in range(depth):
        aq, ak, av, iq, ik, small, mqk, mv, og, ga, gm = _in_proj(
            x2, pos2, g_norm_mix[l], w_in[l], g_q_att[l], g_k_att[l])
        y_att = _dsa(aq, iq, small, ak, av, ik, bsz, seq)
        y_ml = _mlstm(mqk, mv, og, small, w_conv_mlstm[l], b_conv_mlstm[l], b_igate[l], b_fgate[l],
                      g_norm_mlstm_out[l], bsz, seq)
        x1, xn, ri, rf = _merge(x2, y_att, y_ml, ga, gm, w_proj_att[l], w_proj_mlstm[l], w_out[l],
                                g_norm_moe[l], w_router_group[l], b_router_group[l],
                                w_router_expert[l], b_router_expert[l])
        buf_tok, block_e, block_valid, dest = _route_plan(ri[:, :TOP_K_EXPERTS])
        ys = _experts(xn, buf_tok, block_e, block_valid, w_exp_gate[l], w_exp_up[l], w_exp_down[l])
        x2 = _combine(x1, rf, ys, dest)
    return x2.reshape(bsz, seq, dm)
```

```python
import functools

import numpy as np
import jax
import jax.numpy as jnp
from jax import lax
from jax.experimental import pallas as pl
from jax.experimental.pallas import tpu as pltpu

F32 = jnp.float32
BF16 = jnp.bfloat16
I32 = jnp.int32

D_MODEL = 1024
CHUNK = 64
Q_BLOCK = 128
EPS = 1e-6
NEG_INF = -1e30

ATT_HEADS = 8
ATT_KV_HEADS = 2
ATT_HEAD_DIM = 64
ATT_Q_WIDTH = ATT_HEADS * ATT_HEAD_DIM
ATT_KV_WIDTH = ATT_KV_HEADS * ATT_HEAD_DIM
IDX_HEADS = 8
IDX_DIM = 32
TOPK_MAX = 256
ROPE_THETA = 500000.0
ROPE_FRACTION = 4

MLSTM_HEADS = 4
MLSTM_HEAD_DIM = 128
MLSTM_WIDTH = MLSTM_HEADS * MLSTM_HEAD_DIM
CONV_WIDTH = 4

N_GROUPS = 4
EXPERTS_PER_GROUP = 8
N_EXPERTS = N_GROUPS * EXPERTS_PER_GROUP
TOP_K_EXPERTS = 2
D_EXPERT = 512

LANES = 128
SUBLANES = 8
VMEM_LIMIT_BYTES = 56 * 1024 * 1024

SMALL_IK = 0
SMALL_IW = IDX_DIM
SMALL_MI = SMALL_IW + IDX_HEADS
SMALL_MF = SMALL_MI + MLSTM_HEADS

COL_AQ = 0
COL_AK = COL_AQ + ATT_Q_WIDTH
COL_AV = COL_AK + ATT_KV_WIDTH
COL_IQ = COL_AV + ATT_KV_WIDTH
COL_SMALL = COL_IQ + IDX_HEADS * IDX_DIM
COL_MQK = COL_SMALL + LANES
COL_MV = COL_MQK + 2 * MLSTM_WIDTH
COL_MO = COL_MV + MLSTM_WIDTH
COL_GA = COL_MO + MLSTM_WIDTH
COL_GM = COL_GA + D_MODEL
COL_END = COL_GM + D_MODEL

KEY_TILE = 512
MLSTM_CHUNK = 256
MERGE_ROWS = 512
EXPERT_ROWS = 256
COMBINE_ROWS = 256
WORD_BITS = 32
PLANE_GROUP = WORD_BITS * SUBLANES

LOG2_E = 1.4426950408889634
_INT_MIN = -(2 ** 31)


def _sortable_key_const(v):
    b = int(np.array(v, np.float32).view(np.int32))
    return b if b >= 0 else b ^ 0x7FFFFFFF


_KEY_NEG_INF = _sortable_key_const(NEG_INF)
_KEY_VALID = _sortable_key_const(NEG_INF * 0.5)


def _split3(a):
    hi = a.astype(BF16)
    r1 = a - hi.astype(F32)
    mid = r1.astype(BF16)
    lo = (r1 - mid.astype(F32)).astype(BF16)
    return hi, mid, lo


def _dot_exact_rhs(a, b_bf16):
    hi, mid, lo = _split3(a)
    d = functools.partial(jnp.dot, preferred_element_type=F32)
    return d(hi, b_bf16) + d(mid, b_bf16) + d(lo, b_bf16)


def _dot_exact_lhs(a_bf16, b):
    hi, mid, lo = _split3(b)
    d = functools.partial(jnp.dot, preferred_element_type=F32)
    return d(a_bf16, hi) + d(a_bf16, mid) + d(a_bf16, lo)


def _dot_nt(a, b):
    return lax.dot_general(a, b, (((1,), (1,)), ((), ())), preferred_element_type=F32)


def _sigmoid(x):
    return 1.0 / (1.0 + jnp.exp(-x))


def _in_proj_body(x_ref, pos_ref, g_ref, w_ref, rc_ref, bd_ref, gq_ref, gk_ref,
                  aq_ref, ak_ref, av_ref, iq_ref, ik_ref, small_ref, mqk_ref, mv_ref,
                  og_ref, ga_ref, gm_ref):
    x = x_ref[...]
    ms = jnp.mean(x * x, axis=-1, keepdims=True)
    h = (x * lax.rsqrt(ms + EPS) * g_ref[...]).astype(BF16)
    posf = pos_ref[...].astype(F32)

    def proj(lo, hi):
        return jnp.dot(h, w_ref[:, lo:hi], preferred_element_type=F32)

    def rope(t, cos, s_up, s_dn, half):
        n = t.shape[-1]
        return t * cos + pltpu.roll(t, half, 1) * s_up + pltpu.roll(t, n - half, 1) * s_dn

    def head_norm(t, gain):
        ssum = _dot_exact_rhs(t * t, bd_ref[...])
        return t * lax.rsqrt(ssum * (1.0 / ATT_HEAD_DIM) + EPS) * gain

    ang_a = posf * rc_ref[0:1, :]
    cos_a, sin_a = jnp.cos(ang_a), jnp.sin(ang_a)
    up_a, dn_a = sin_a * rc_ref[1:2, :], sin_a * rc_ref[2:3, :]
    ang_i = posf * rc_ref[3:4, :]
    cos_i, sin_i = jnp.cos(ang_i), jnp.sin(ang_i)
    up_i, dn_i = sin_i * rc_ref[4:5, :], sin_i * rc_ref[5:6, :]
    small_mask = rc_ref[6:7, :]
    cos_s = 1.0 + (cos_i - 1.0) * small_mask
    up_s, dn_s = up_i * small_mask, dn_i * small_mask

    att_half = ATT_HEAD_DIM // ROPE_FRACTION // 2
    idx_half = IDX_DIM // ROPE_FRACTION // 2
    q_scale = ATT_HEAD_DIM ** -0.5 * LOG2_E
    tm = x.shape[0]

    def store_qblock_t(ref, c, t):
        for r in range(tm // Q_BLOCK):
            ref[r, c * LANES:(c + 1) * LANES, :] = t[r * Q_BLOCK:(r + 1) * Q_BLOCK, :].T.astype(BF16)

    for c in range(ATT_Q_WIDTH // LANES):
        t = proj(COL_AQ + c * LANES, COL_AQ + (c + 1) * LANES)
        t = rope(head_norm(t, gq_ref[...]), cos_a, up_a, dn_a, att_half)
        store_qblock_t(aq_ref, c, t * q_scale)
    t = proj(COL_AK, COL_AK + LANES)
    ak_ref[...] = rope(head_norm(t, gk_ref[...]), cos_a, up_a, dn_a, att_half).astype(BF16)
    av_ref[0] = proj(COL_AV, COL_AV + LANES).T.astype(BF16)
    for c in range(IDX_HEADS * IDX_DIM // LANES):
        t = proj(COL_IQ + c * LANES, COL_IQ + (c + 1) * LANES)
        store_qblock_t(iq_ref, c, rope(t, cos_i, up_i, dn_i, idx_half))
    t = rope(proj(COL_SMALL, COL_SMALL + LANES), cos_s, up_s, dn_s, idx_half)
    small_ref[...] = t
    ik_ref[...] = t[:, SMALL_IK:SMALL_IK + IDX_DIM].astype(BF16)
    for c in range(2 * MLSTM_WIDTH // 512):
        mqk_ref[:, c * 512:(c + 1) * 512] = proj(COL_MQK + c * 512, COL_MQK + (c + 1) * 512)
    mv_ref[...] = proj(COL_MV, COL_MV + MLSTM_WIDTH).astype(BF16)
    og_ref[...] = _sigmoid(proj(COL_MO, COL_MO + MLSTM_WIDTH)).astype(BF16)
    for c in range(D_MODEL // 512):
        ga_ref[:, c * 512:(c + 1) * 512] = _sigmoid(
            proj(COL_GA + c * 512, COL_GA + (c + 1) * 512)).astype(BF16)
        gm_ref[:, c * 512:(c + 1) * 512] = _sigmoid(
            proj(COL_GM + c * 512, COL_GM + (c + 1) * 512)).astype(BF16)


def _rope_consts():
    lane = np.arange(LANES)

    def inv_freq(rot):
        half = rot // 2
        return jnp.float32(ROPE_THETA) ** (-jnp.arange(half, dtype=F32) * 2.0 / rot)

    rot_a = ATT_HEAD_DIM // ROPE_FRACTION
    rot_i = IDX_DIM // ROPE_FRACTION
    fa, fi = inv_freq(rot_a), inv_freq(rot_i)
    ja, ji = lane % ATT_HEAD_DIM, lane % IDX_DIM
    rc = jnp.zeros((16, LANES), F32)
    rc = rc.at[0].set(jnp.where(ja < rot_a, fa[ja % (rot_a // 2)], 0.0))
    rc = rc.at[1].set(jnp.asarray(((ja >= rot_a // 2) & (ja < rot_a)).astype(np.float32)))
    rc = rc.at[2].set(jnp.asarray(-(ja < rot_a // 2).astype(np.float32)))
    rc = rc.at[3].set(jnp.where(ji < rot_i, fi[ji % (rot_i // 2)], 0.0))
    rc = rc.at[4].set(jnp.asarray(((ji >= rot_i // 2) & (ji < rot_i)).astype(np.float32)))
    rc = rc.at[5].set(jnp.asarray(-(ji < rot_i // 2).astype(np.float32)))
    rc = rc.at[6].set(jnp.asarray((lane < IDX_DIM).astype(np.float32)))
    return rc


def _pack_w_in(w_in):
    sizes = (ATT_Q_WIDTH, ATT_KV_WIDTH, ATT_KV_WIDTH, IDX_HEADS * IDX_DIM, IDX_DIM, IDX_HEADS,
             MLSTM_WIDTH, MLSTM_WIDTH, MLSTM_WIDTH, MLSTM_WIDTH, MLSTM_HEADS, MLSTM_HEADS,
             D_MODEL, D_MODEL)
    pts = np.cumsum(sizes)[:-1].tolist()
    (aq, ak, av, iq, ik, iw, mq, mk, mv, mo, mi, mf, ga, gm) = jnp.split(w_in, pts, axis=-1)
    pad = jnp.zeros((D_MODEL, LANES - (SMALL_MF + MLSTM_HEADS)), w_in.dtype)
    small = jnp.concatenate([ik, iw, mi, mf, pad], axis=-1)
    return jnp.concatenate([aq, ak, av, iq, small, mq, mk, mv, mo, ga, gm], axis=-1).astype(BF16)


def _in_proj(x2, pos2, g_norm_mix, w_in, g_q_att, g_k_att):
    t_rows = x2.shape[0]
    tm = KEY_TILE
    w = _pack_w_in(w_in)
    lane = np.arange(LANES)
    bd = jnp.asarray((lane[:, None] // ATT_HEAD_DIM == lane[None, :] // ATT_HEAD_DIM)
                     .astype(np.float32)).astype(BF16)
    reps = LANES // ATT_HEAD_DIM
    gq = jnp.tile(g_q_att.astype(F32), reps)[None, :]
    gk = jnp.tile(g_k_att.astype(F32), reps)[None, :]

    def row(width):
        return pl.BlockSpec((tm, width), lambda i: (i, 0))

    def const(shape):
        return pl.BlockSpec(shape, lambda i: (0, 0))

    qb = tm // Q_BLOCK

    def qblock_t(width):
        return pl.BlockSpec((qb, width, Q_BLOCK), lambda i: (i, 0, 0))

    out_shapes = (
        jax.ShapeDtypeStruct((t_rows // Q_BLOCK, ATT_Q_WIDTH, Q_BLOCK), BF16),
        jax.ShapeDtypeStruct((t_rows, ATT_KV_WIDTH), BF16),
        jax.ShapeDtypeStruct((t_rows // tm, ATT_KV_WIDTH, tm), BF16),
        jax.ShapeDtypeStruct((t_rows // Q_BLOCK, IDX_HEADS * IDX_DIM, Q_BLOCK), BF16),
        jax.ShapeDtypeStruct((t_rows, IDX_DIM), BF16),
        jax.ShapeDtypeStruct((t_rows, LANES), F32),
        jax.ShapeDtypeStruct((t_rows, 2 * MLSTM_WIDTH), F32),
        jax.ShapeDtypeStruct((t_rows, MLSTM_WIDTH), BF16),
        jax.ShapeDtypeStruct((t_rows, MLSTM_WIDTH), BF16),
        jax.ShapeDtypeStruct((t_rows, D_MODEL), BF16),
        jax.ShapeDtypeStruct((t_rows, D_MODEL), BF16),
    )
    return pl.pallas_call(
        _in_proj_body,
        grid=(t_rows // tm,),
        in_specs=[row(D_MODEL), row(1), const((1, D_MODEL)), const((D_MODEL, COL_END)),
                  const((16, LANES)), const((LANES, LANES)), const((1, LANES)), const((1, LANES))],
        out_specs=[qblock_t(ATT_Q_WIDTH), row(ATT_KV_WIDTH),
                   pl.BlockSpec((1, ATT_KV_WIDTH, tm), lambda i: (i, 0, 0)),
                   qblock_t(IDX_HEADS * IDX_DIM)] + [row(s.shape[1]) for s in out_shapes[4:]],
        out_shape=out_shapes,
        compiler_params=pltpu.CompilerParams(
            dimension_semantics=("arbitrary",), vmem_limit_bytes=VMEM_LIMIT_BYTES),
        name="in_proj",
    )(x2, pos2, g_norm_mix.astype(F32)[None, :], w, _rope_consts(), bd, gq, gk)


def _bit_transpose32(words):
    a = list(words)
    j, m = 16, 0x0000FFFF
    while j:
        k = 0
        while k < WORD_BITS:
            t = (a[k] ^ lax.shift_right_logical(a[k + j], jnp.int32(j))) & m
            a[k] = a[k] ^ t
            a[k + j] = a[k + j] ^ lax.shift_left(t, jnp.int32(j))
            k = (k + j + 1) & ~j
        j >>= 1
        m = m ^ (m << j)
    return a


def _dsa_body(aq_ref, iq_ref, small_ref, k_ref, vt_ref, ik_ref, out_ref,
              key_scr, plane_scr, qi_scr, q2_scr, p_scr, acc_scr, *, topk, tk):
    blk = pl.program_id(1)
    nq = Q_BLOCK
    rep = ATT_HEADS // ATT_KV_HEADS
    groups_per_tile = tk // PLANE_GROUP
    n_tiles = lax.shift_right_logical(blk * nq + nq + tk - 1, int(np.log2(tk)))
    idx_scale = (IDX_DIM * IDX_HEADS) ** -0.5
    plane_rows = plane_scr.shape[1]

    @pl.when((pl.program_id(0) == 0) & (blk == 0))
    def _():
        plane_scr[...] = jnp.zeros_like(plane_scr)

    for h in range(IDX_HEADS):
        qi_scr[:, h * nq:(h + 1) * nq] = iq_ref[0, h * IDX_DIM:(h + 1) * IDX_DIM, :]
    w_rows = small_ref[...].T[SMALL_IW:SMALL_IW + IDX_HEADS, :] * idx_scale

    q_chunk = lax.shift_right_logical(
        blk * nq + lax.broadcasted_iota(I32, (1, nq), 1), int(np.log2(CHUNK)))
    key_row = lax.broadcasted_iota(I32, (tk, 1), 0)

    def score_tile(t, carry):
        start = pl.multiple_of(t * tk, tk)
        rel = jnp.dot(ik_ref[pl.ds(start, tk), :], qi_scr[...], preferred_element_type=F32)
        sc = jnp.maximum(rel[:, 0:nq], 0.0) * w_rows[0:1]
        for h in range(1, IDX_HEADS):
            sc = sc + jnp.maximum(rel[:, h * nq:(h + 1) * nq], 0.0) * w_rows[h:h + 1]
        visible = lax.shift_right_logical(start + key_row, int(np.log2(CHUNK))) <= q_chunk
        sc = jnp.where(visible, sc, NEG_INF)
        bits = lax.bitcast_convert_type(sc, I32)
        key = jnp.where(bits < 0, bits ^ 0x7FFFFFFF, bits)
        key_scr[pl.ds(start, tk), :] = key
        biased = key ^ _INT_MIN
        for g in range(groups_per_tile):
            words = [biased[g * PLANE_GROUP + i * SUBLANES:g * PLANE_GROUP + (i + 1) * SUBLANES, :]
                     for i in range(WORD_BITS)]
            planes = _bit_transpose32(words)
            row0 = pl.multiple_of((t * groups_per_tile + g) * SUBLANES, SUBLANES)
            for p in range(WORD_BITS):
                plane_scr[p, pl.ds(row0, SUBLANES), :] = planes[p]
        return carry

    lax.fori_loop(0, n_tiles, score_tile, 0)

    group_of_row = lax.shift_right_logical(
        lax.broadcasted_iota(I32, (plane_rows, nq), 0), int(np.log2(SUBLANES)))
    alive = jnp.where(group_of_row < n_tiles * groups_per_tile, -1, 0)
    k_rem = jnp.full((1, nq), float(topk), F32)
    thr_u = jnp.zeros((1, nq), I32)

    def lane_count(mask_words):
        pc = lax.population_count(mask_words)
        part = jnp.sum(pc.reshape(plane_rows // SUBLANES, SUBLANES, nq), axis=0)
        return jnp.sum(part.astype(F32), axis=0, keepdims=True)

    for p in range(WORD_BITS):
        bit_value = _INT_MIN if p == 0 else 1 << (WORD_BITS - 1 - p)
        ones = alive & plane_scr[p]
        c_ones = lane_count(ones)
        take = c_ones >= k_rem
        alive = jnp.where(take, ones, alive ^ ones)
        k_rem = jnp.where(take, k_rem, k_rem - c_ones)
        thr_u = jnp.where(take, thr_u | bit_value, thr_u)
    thr = thr_u ^ _INT_MIN
    quota = k_rem

    q2_scr[...] = jnp.zeros_like(q2_scr)
    for h in range(ATT_HEADS):
        g = h // rep
        q2_scr[g * ATT_HEAD_DIM:(g + 1) * ATT_HEAD_DIM, h * nq:(h + 1) * nq] = (
            aq_ref[0, h * ATT_HEAD_DIM:(h + 1) * ATT_HEAD_DIM, :])

    def attend(with_ties):
        acc_scr[...] = jnp.zeros_like(acc_scr)
        if with_ties:
            rr = lax.broadcasted_iota(I32, (tk, tk), 0)
            cc = lax.broadcasted_iota(I32, (tk, tk), 1)
            earlier = jnp.where(cc < rr, 1.0, 0.0).astype(BF16)

        def tile(t, carry):
            m_all, l_all, eq_seen = carry
            start = pl.multiple_of(t * tk, tk)
            key = key_scr[pl.ds(start, tk), :]
            valid = key > _KEY_VALID
            if with_ties:
                eq = key == thr
                eqf = jnp.where(eq, 1.0, 0.0)
                rank = jnp.dot(earlier, eqf.astype(BF16), preferred_element_type=F32) + eq_seen
                sel = valid & ((key > thr) | (eq & (rank < quota)))
                eq_seen = eq_seen + jnp.sum(eqf, axis=0, keepdims=True)
            else:
                sel = valid & (key >= thr)
            bias = jnp.where(sel, 0.0, NEG_INF)
            s = jnp.dot(k_ref[pl.ds(start, tk), :], q2_scr[...], preferred_element_type=F32)
            m_rows, l_rows, a_rows = [], [], []
            for h in range(ATT_HEADS):
                cols = slice(h * nq, (h + 1) * nq)
                sh = s[:, cols] + bias
                m_old = m_all[h:h + 1]
                m_new = jnp.maximum(m_old, jnp.max(sh, axis=0, keepdims=True))
                p = jnp.exp2(sh - m_new)
                alpha = jnp.exp2(m_old - m_new)
                l_rows.append(alpha * l_all[h:h + 1] + jnp.sum(p, axis=0, keepdims=True))
                p_scr[:, cols] = p.astype(BF16)
                m_rows.append(m_new)
                a_rows.append(alpha)
            acc_scr[...] = (acc_scr[...] * jnp.concatenate(a_rows, axis=1)
                            + jnp.dot(vt_ref[t], p_scr[...], preferred_element_type=F32))
            return jnp.concatenate(m_rows, axis=0), jnp.concatenate(l_rows, axis=0), eq_seen

        m_all, l_all, _ = lax.fori_loop(
            0, n_tiles, tile,
            (jnp.full((ATT_HEADS, nq), NEG_INF, F32), jnp.zeros((ATT_HEADS, nq), F32),
             jnp.zeros((1, nq), F32)))
        inv_l = 1.0 / l_all
        for j in range(ATT_HEADS // 2):
            g = (2 * j) // rep
            halves = [acc_scr[g * ATT_HEAD_DIM:(g + 1) * ATT_HEAD_DIM, h * nq:(h + 1) * nq]
                      * inv_l[h:h + 1] for h in (2 * j, 2 * j + 1)]
            out_ref[:, j * LANES:(j + 1) * LANES] = jnp.concatenate(halves, axis=0).T.astype(BF16)

    tie_lanes = jnp.where((lane_count(alive) > quota) & (thr > _KEY_NEG_INF), 1.0, 0.0)
    need_ties = jnp.max(tie_lanes) > 0.5

    @pl.when(need_ties)
    def _():
        attend(True)

    @pl.when(jnp.logical_not(need_ties))
    def _():
        attend(False)


def _dsa(aq_t, iq_t, small, ak, av_t, ik, bsz, seq):
    nb = seq // Q_BLOCK
    tk = KEY_TILE
    topk = min(TOPK_MAX, seq // 4)
    assert seq % tk == 0 and tk >= topk and tk & (tk - 1) == 0 and tk % PLANE_GROUP == 0
    nkt = seq // tk

    def qspec_t(width):
        return pl.BlockSpec((1, width, Q_BLOCK), lambda b, j: (b * nb + j, 0, 0))

    def qspec(width):
        return pl.BlockSpec((Q_BLOCK, width), lambda b, j: (b * nb + j, 0))

    def kspec(width):
        return pl.BlockSpec((seq, width), lambda b, j: (b, 0))

    cols = ATT_HEADS * Q_BLOCK
    return pl.pallas_call(
        functools.partial(_dsa_body, topk=topk, tk=tk),
        grid=(bsz, nb),
        in_specs=[qspec_t(ATT_Q_WIDTH), qspec_t(IDX_HEADS * IDX_DIM), qspec(LANES),
                  kspec(ATT_KV_WIDTH),
                  pl.BlockSpec((nkt, ATT_KV_WIDTH, tk), lambda b, j: (b, 0, 0)),
                  kspec(IDX_DIM)],
        out_specs=qspec(ATT_Q_WIDTH),
        out_shape=jax.ShapeDtypeStruct((bsz * seq, ATT_Q_WIDTH), BF16),
        scratch_shapes=[
            pltpu.VMEM((seq, Q_BLOCK), I32),
            pltpu.VMEM((WORD_BITS, seq // WORD_BITS, Q_BLOCK), I32),
            pltpu.VMEM((IDX_DIM, cols), BF16),
            pltpu.VMEM((ATT_KV_WIDTH, cols), BF16),
            pltpu.VMEM((tk, cols), BF16),
            pltpu.VMEM((ATT_KV_WIDTH, cols), F32),
        ],
        compiler_params=pltpu.CompilerParams(
            dimension_semantics=("arbitrary", "arbitrary"), vmem_limit_bytes=VMEM_LIMIT_BYTES),
        name="dsa_attention",
    )(aq_t, iq_t, small, ak, av_t, ik)


def _mlstm_body(mqk_ref, mv_ref, og_ref, small_ref, cw_ref, cb_ref, gb_ref, gn_ref, y_ref,
                ubuf, c_scr, n_scr, m_scr, *, chunk):
    ln = chunk
    d = MLSTM_HEAD_DIM
    halo = SUBLANES

    @pl.when(pl.program_id(1) == 0)
    def _():
        ubuf[0:halo, :] = jnp.zeros((halo, 2 * MLSTM_WIDTH), F32)
        c_scr[...] = jnp.zeros_like(c_scr)
        n_scr[...] = jnp.zeros_like(n_scr)
        m_scr[...] = jnp.zeros_like(m_scr)

    ubuf[halo:halo + ln, :] = mqk_ref[...]
    conv = cb_ref[...]
    for j in range(CONV_WIDTH):
        off = halo - (CONV_WIDTH - 1) + j
        conv = conv + cw_ref[j:j + 1, :] * ubuf[off:off + ln, :]
    ubuf[0:halo, :] = ubuf[ln:ln + halo, :]
    qk = conv * _sigmoid(conv)
    q_all = qk[:, :MLSTM_WIDTH]
    k_all = qk[:, MLSTM_WIDTH:] * (d ** -0.5)

    sm = small_ref[...] + gb_ref[...]
    lf = jnp.minimum(sm, 0.0) - jnp.log(1.0 + jnp.exp(-jnp.abs(sm)))
    sm_t = sm.T
    lf_t = lf.T
    rr = lax.broadcasted_iota(I32, (ln, ln), 0)
    cc = lax.broadcasted_iota(I32, (ln, ln), 1)
    causal = cc <= rr
    tri = jnp.where(causal, 1.0, 0.0).astype(BF16)
    tri_t = jnp.where(rr <= cc, 1.0, 0.0).astype(BF16)
    b_cols = _dot_exact_lhs(tri, lf)
    b_rows = _dot_exact_rhs(lf_t, tri_t)

    for h in range(MLSTM_HEADS):
        hs = slice(h * d, (h + 1) * d)
        ig_col = sm[:, SMALL_MI + h:SMALL_MI + h + 1]
        ig_row = sm_t[SMALL_MI + h:SMALL_MI + h + 1, :]
        b_col = b_cols[:, SMALL_MF + h:SMALL_MF + h + 1]
        b_row = b_rows[SMALL_MF + h:SMALL_MF + h + 1, :]
        m_st = m_scr[h:h + 1, 0:1]
        c_st = c_scr[h]
        n_st = n_scr[h:h + 1, :]

        dmat = jnp.where(causal, b_col - b_row + ig_row, NEG_INF)
        inter = b_col + m_st
        m_t = jnp.maximum(inter, jnp.max(dmat, axis=-1, keepdims=True))
        w_intra = jnp.exp(dmat - m_t)
        w_inter = jnp.exp(inter - m_t)
        qf = q_all[:, hs]
        kf = k_all[:, hs]
        qh = qf.astype(BF16)
        vh = mv_ref[:, hs]
        s = _dot_nt(qh, kf.astype(BF16)) * w_intra
        num = (w_inter * jnp.dot(qh, c_st.astype(BF16), preferred_element_type=F32)
               + jnp.dot(s.astype(BF16), vh, preferred_element_type=F32))
        den = (w_inter * jnp.sum(qf * n_st, axis=-1, keepdims=True)
               + jnp.sum(s, axis=-1, keepdims=True))
        hh = num / jnp.maximum(jnp.abs(den), jnp.exp(-m_t))

        b_last = b_col[ln - 1:ln, :]
        gdec = b_last - b_col + ig_col
        m_new = jnp.maximum(b_last + m_st, jnp.max(gdec, axis=0, keepdims=True))
        decay = jnp.exp(b_last + m_st - m_new)
        kw = kf * jnp.exp(gdec - m_new)
        c_scr[h] = decay * c_st + jnp.dot(kw.T.astype(BF16), vh, preferred_element_type=F32)
        n_scr[h:h + 1, :] = decay * n_st + jnp.sum(kw, axis=0, keepdims=True)
        m_scr[h:h + 1, :] = jnp.broadcast_to(m_new, (1, LANES))

        ms = jnp.mean(hh * hh, axis=-1, keepdims=True)
        yh = hh * lax.rsqrt(ms + EPS) * gn_ref[:, hs]
        y_ref[:, hs] = (og_ref[:, hs].astype(F32) * yh).astype(BF16)


def _mlstm(mqk, mv, og, small, w_conv, b_conv, b_igate, b_fgate, g_norm_out, bsz, seq):
    ln = min(MLSTM_CHUNK, seq)
    nc = seq // ln
    gbias = jnp.zeros((1, LANES), F32)
    gbias = gbias.at[0, SMALL_MI:SMALL_MI + MLSTM_HEADS].set(b_igate.astype(F32))
    gbias = gbias.at[0, SMALL_MF:SMALL_MF + MLSTM_HEADS].set(b_fgate.astype(F32))

    def row(width):
        return pl.BlockSpec((ln, width), lambda b, c: (b * nc + c, 0))

    def const(shape):
        return pl.BlockSpec(shape, lambda b, c: (0, 0))

    return pl.pallas_call(
        functools.partial(_mlstm_body, chunk=ln),
        grid=(bsz, nc),
        in_specs=[row(2 * MLSTM_WIDTH), row(MLSTM_WIDTH), row(MLSTM_WIDTH), row(LANES),
                  const((CONV_WIDTH, 2 * MLSTM_WIDTH)), const((1, 2 * MLSTM_WIDTH)),
                  const((1, LANES)), const((1, MLSTM_WIDTH))],
        out_specs=row(MLSTM_WIDTH),
        out_shape=jax.ShapeDtypeStruct((bsz * seq, MLSTM_WIDTH), BF16),
        scratch_shapes=[
            pltpu.VMEM((ln + SUBLANES, 2 * MLSTM_WIDTH), F32),
            pltpu.VMEM((MLSTM_HEADS, MLSTM_HEAD_DIM, MLSTM_HEAD_DIM), F32),
            pltpu.VMEM((SUBLANES, MLSTM_HEAD_DIM), F32),
            pltpu.VMEM((SUBLANES, LANES), F32),
        ],
        compiler_params=pltpu.CompilerParams(
            dimension_semantics=("arbitrary", "arbitrary"), vmem_limit_bytes=VMEM_LIMIT_BYTES),
        name="mlstm",
    )(mqk, mv, og, small, w_conv.astype(F32), b_conv.astype(F32)[None, :], gbias,
      g_norm_out.astype(F32)[None, :])


def _merge_body(x_ref, ya_ref, ym_ref, ga_ref, gm_ref, wpa_ref, wpm_ref, wo_ref, gmoe_ref,
                wr_ref, br_ref, x1_ref, xn_ref, ri_ref, rf_ref):
    pa = jnp.dot(ya_ref[...], wpa_ref[...], preferred_element_type=F32)
    pm = jnp.dot(ym_ref[...], wpm_ref[...], preferred_element_type=F32)
    merged = ga_ref[...].astype(F32) * pa + gm_ref[...].astype(F32) * pm
    x1 = x_ref[...] + jnp.dot(merged.astype(BF16), wo_ref[...], preferred_element_type=F32)
    x1_ref[...] = x1
    ms = jnp.mean(x1 * x1, axis=-1, keepdims=True)
    xn = x1 * lax.rsqrt(ms + EPS) * gmoe_ref[...]
    xn_ref[...] = xn

    logits = jnp.dot(xn, wr_ref[...], preferred_element_type=F32,
                     precision=lax.Precision.HIGHEST) + br_ref[...]
    lane = lax.broadcasted_iota(I32, logits.shape, 1).astype(F32)

    def first_argmax(vals, mask):
        v = jnp.where(mask, vals, -jnp.inf)
        top = jnp.max(v, axis=-1, keepdims=True)
        idx = jnp.min(jnp.where(mask & (v == top), lane, float(LANES)), axis=-1, keepdims=True)
        return top, idx

    is_group = lane < N_GROUPS
    g_max, g_sel = first_argmax(logits, is_group)
    g_den = jnp.sum(jnp.where(is_group, jnp.exp(logits - g_max), 0.0), axis=-1, keepdims=True)
    g_top = 1.0 / g_den
    lo = N_GROUPS + g_sel * EXPERTS_PER_GROUP
    in_group = (lane >= lo) & (lane < lo + EXPERTS_PER_GROUP)
    v1, i1 = first_argmax(logits, in_group)
    v2, i2 = first_argmax(logits, in_group & (lane != i1))
    e2 = jnp.exp(v2 - v1)
    p1 = 1.0 / (1.0 + e2)
    gate1 = p1 * g_top
    gate2 = (e2 * p1) * g_top
    ri_ref[...] = jnp.where(lane == 0, i1 - N_GROUPS,
                            jnp.where(lane == 1, i2 - N_GROUPS, 0.0)).astype(I32)
    rf_ref[...] = jnp.where(lane == 0, gate1, jnp.where(lane == 1, gate2, 0.0))


def _merge(x2, y_att, y_ml, ga, gm, w_proj_att, w_proj_mlstm, w_out, g_norm_moe,
           w_router_group, b_router_group, w_router_expert, b_router_expert):
    t_rows = x2.shape[0]
    tm = min(MERGE_ROWS, t_rows)
    n_r = N_GROUPS + N_EXPERTS
    wr = jnp.concatenate([w_router_group, w_router_expert,
                          jnp.zeros((D_MODEL, LANES - n_r), F32)], axis=-1).astype(F32)
    br = jnp.concatenate([b_router_group, b_router_expert,
                          jnp.zeros((LANES - n_r,), F32)]).astype(F32)[None, :]

    def row(width):
        return pl.BlockSpec((tm, width), lambda i: (i, 0))

    def const(shape):
        return pl.BlockSpec(shape, lambda i: (0, 0))

    out_shapes = (
        jax.ShapeDtypeStruct((t_rows, D_MODEL), F32),
        jax.ShapeDtypeStruct((t_rows, D_MODEL), F32),
        jax.ShapeDtypeStruct((t_rows, LANES), I32),
        jax.ShapeDtypeStruct((t_rows, LANES), F32),
    )
    return pl.pallas_call(
        _merge_body,
        grid=(t_rows // tm,),
        in_specs=[row(D_MODEL), row(ATT_Q_WIDTH), row(MLSTM_WIDTH), row(D_MODEL), row(D_MODEL),
                  const((ATT_Q_WIDTH, D_MODEL)), const((MLSTM_WIDTH, D_MODEL)),
                  const((D_MODEL, D_MODEL)), const((1, D_MODEL)), const((D_MODEL, LANES)),
                  const((1, LANES))],
        out_specs=[row(s.shape[1]) for s in out_shapes],
        out_shape=out_shapes,
        compiler_params=pltpu.CompilerParams(
            dimension_semantics=("arbitrary",), vmem_limit_bytes=VMEM_LIMIT_BYTES),
        name="merge_route",
    )(x2, y_att, y_ml, ga, gm, w_proj_att.astype(BF16), w_proj_mlstm.astype(BF16),
      w_out.astype(BF16), g_norm_moe.astype(F32)[None, :], wr, br)


def _expert_body(be_ref, bv_ref, tok_cur_ref, tok_nxt_ref, xn_hbm, wg_ref, wu_ref, wd_ref,
                 ys_ref, xbuf, sem, *, rows):
    i = pl.program_id(0)
    nblk = pl.num_programs(0)
    slot = lax.rem(i, 2)

    def row_copy(tok, dst_slot, r):
        return pltpu.make_async_copy(
            xn_hbm.at[pl.ds(tok, 1), :], xbuf.at[dst_slot, pl.ds(r, 1), :], sem.at[dst_slot])

    def gather(tok_ref, dst_slot):
        def body(r, carry):
            row_copy(tok_ref[0, 0, r], dst_slot, r).start()
            return carry
        lax.fori_loop(0, rows, body, 0)

    @pl.when((i == 0) & (bv_ref[0] > 0))
    def _():
        gather(tok_cur_ref, 0)

    nxt = jnp.minimum(i + 1, nblk - 1)

    @pl.when((i + 1 < nblk) & (bv_ref[nxt] > 0))
    def _():
        gather(tok_nxt_ref, 1 - slot)

    @pl.when(bv_ref[i] > 0)
    def _():
        def wait_body(r, carry):
            row_copy(0, slot, r).wait()
            return carry
        lax.fori_loop(0, rows, wait_body, 0)
        xb = xbuf[slot].astype(BF16)
        a = jnp.dot(xb, wg_ref[0], preferred_element_type=F32)
        u = jnp.dot(xb, wu_ref[0], preferred_element_type=F32)
        hid = (a * _sigmoid(a) * u).astype(BF16)
        ys_ref[...] = jnp.dot(hid, wd_ref[0], preferred_element_type=F32)

    @pl.when(bv_ref[i] == 0)
    def _():
        ys_ref[...] = jnp.zeros_like(ys_ref)


def _experts(xn, buf_tok, block_e, block_valid, w_gate, w_up, w_down):
    rows = EXPERT_ROWS
    cap = buf_tok.shape[0]
    nblk = cap // rows
    tok3 = buf_tok.reshape(nblk, 1, rows)
    grid_spec = pltpu.PrefetchScalarGridSpec(
        num_scalar_prefetch=2,
        grid=(nblk,),
        in_specs=[
            pl.BlockSpec((1, 1, rows), lambda i, be, bv: (i, 0, 0), memory_space=pltpu.SMEM),
            pl.BlockSpec((1, 1, rows), lambda i, be, bv: (jnp.minimum(i + 1, nblk - 1), 0, 0),
                         memory_space=pltpu.SMEM),
            pl.BlockSpec(memory_space=pl.ANY),
            pl.BlockSpec((1, D_MODEL, D_EXPERT), lambda i, be, bv: (be[i], 0, 0)),
            pl.BlockSpec((1, D_MODEL, D_EXPERT), lambda i, be, bv: (be[i], 0, 0)),
            pl.BlockSpec((1, D_EXPERT, D_MODEL), lambda i, be, bv: (be[i], 0, 0)),
        ],
        out_specs=pl.BlockSpec((rows, D_MODEL), lambda i, be, bv: (i, 0)),
        scratch_shapes=[pltpu.VMEM((2, rows, D_MODEL), F32), pltpu.SemaphoreType.DMA((2,))],
    )
    return pl.pallas_call(
        functools.partial(_expert_body, rows=rows),
        grid_spec=grid_spec,
        out_shape=jax.ShapeDtypeStruct((cap, D_MODEL), F32),
        compiler_params=pltpu.CompilerParams(
            dimension_semantics=("arbitrary",), vmem_limit_bytes=VMEM_LIMIT_BYTES),
        name="moe_experts",
    )(block_e, block_valid, tok3, tok3, xn, w_gate.astype(BF16), w_up.astype(BF16),
      w_down.astype(BF16))


def _combine_body(dest_ref, x1_ref, rf_ref, ys_hbm, out_ref, ybuf, sem, *, rows):
    def row_copy(src, k, r):
        return pltpu.make_async_copy(
            ys_hbm.at[pl.ds(src, 1), :], ybuf.at[k, pl.ds(r, 1), :], sem.at[0])

    def issue(r, carry):
        for k in range(TOP_K_EXPERTS):
            row_copy(dest_ref[0, 0, TOP_K_EXPERTS * r + k], k, r).start()
        return carry

    lax.fori_loop(0, rows, issue, 0)

    def drain(r, carry):
        for k in range(TOP_K_EXPERTS):
            row_copy(0, k, r).wait()
        return carry

    lax.fori_loop(0, rows, drain, 0)
    gates = rf_ref[...]
    out = x1_ref[...]
    for k in range(TOP_K_EXPERTS):
        out = out + gates[:, k:k + 1] * ybuf[k]
    out_ref[...] = out


def _combine(x1, rf, ys, dest):
    t_rows = x1.shape[0]
    rows = min(COMBINE_ROWS, t_rows)
    nt = t_rows // rows
    dest3 = dest.reshape(nt, 1, rows * TOP_K_EXPERTS)
    return pl.pallas_call(
        functools.partial(_combine_body, rows=rows),
        grid=(nt,),
        in_specs=[
            pl.BlockSpec((1, 1, rows * TOP_K_EXPERTS), lambda i: (i, 0, 0),
                         memory_space=pltpu.SMEM),
            pl.BlockSpec((rows, D_MODEL), lambda i: (i, 0)),
            pl.BlockSpec((rows, LANES), lambda i: (i, 0)),
            pl.BlockSpec(memory_space=pl.ANY),
        ],
        out_specs=pl.BlockSpec((rows, D_MODEL), lambda i: (i, 0)),
        out_shape=jax.ShapeDtypeStruct((t_rows, D_MODEL), F32),
        scratch_shapes=[pltpu.VMEM((TOP_K_EXPERTS, rows, D_MODEL), F32),
                        pltpu.SemaphoreType.DMA((1,))],
        compiler_params=pltpu.CompilerParams(
            dimension_semantics=("arbitrary",), vmem_limit_bytes=VMEM_LIMIT_BYTES),
        name="moe_combine",
    )(dest3, x1, rf, ys)


def _route_plan(expert_id):
    t_rows = expert_id.shape[0]
    m = t_rows * TOP_K_EXPERTS
    rows = EXPERT_ROWS
    e_flat = expert_id.reshape(m)
    order = jnp.argsort(e_flat, stable=True).astype(I32)
    counts = jnp.sum((e_flat[:, None] == jnp.arange(N_EXPERTS, dtype=I32)[None, :]).astype(I32), axis=0)
    padded = ((counts + rows - 1) // rows) * rows
    starts = jnp.cumsum(counts) - counts
    pends = jnp.cumsum(padded)
    pstarts = pends - padded
    cap = ((m + N_EXPERTS * rows + rows - 1) // rows) * rows
    nblk = cap // rows
    blk_start = jnp.arange(nblk, dtype=I32) * rows
    block_e = jnp.minimum(jnp.searchsorted(pends, blk_start, side='right'), N_EXPERTS - 1).astype(I32)
    block_valid = (blk_start < pends[-1]).astype(I32)
    slot = jnp.arange(cap, dtype=I32)
    slot_e = jnp.repeat(block_e, rows)
    off = slot - pstarts[slot_e]
    live = (off < counts[slot_e]) & (jnp.repeat(block_valid, rows) > 0)
    src = jnp.clip(starts[slot_e] + off, 0, m - 1)
    buf_tok = jnp.where(live, order[src] // TOP_K_EXPERTS, 0).astype(I32)
    rank_sorted = jnp.arange(m, dtype=I32) - starts[e_flat[order]]
    dest_sorted = pstarts[e_flat[order]] + rank_sorted
    dest = dest_sorted[jnp.argsort(order)].astype(I32)
    return buf_tok, block_e, block_valid, dest


def kernel(x, positions, g_norm_mix, w_in, g_q_att, g_k_att, w_conv_mlstm, b_conv_mlstm, b_igate,
           b_fgate, g_norm_mlstm_out, w_proj_att, w_proj_mlstm, w_out, g_norm_moe, w_router_group,
           b_router_group, w_router_expert, b_router_expert, w_exp_gate, w_exp_up, w_exp_down):
    bsz, seq, dm = x.shape
    depth = g_norm_mix.shape[0]
    x2 = x.reshape(bsz * seq, dm)
    pos2 = positions.reshape(bsz * seq, 1).astype(I32)
    for l in range(depth):
        aq, ak, av, iq, ik, small, mqk, mv, og, ga, gm = _in_proj(
            x2, pos2, g_norm_mix[l], w_in[l], g_q_att[l], g_k_att[l])
        y_att = _dsa(aq, iq, small, ak, av, ik, bsz, seq)
        y_ml = _mlstm(mqk, mv, og, small, w_conv_mlstm[l], b_conv_mlstm[l], b_igate[l], b_fgate[l],
                      g_norm_mlstm_out[l], bsz, seq)
        x1, xn, ri, rf = _merge(x2, y_att, y_ml, ga, gm, w_proj_att[l], w_proj_mlstm[l], w_out[l],
                                g_norm_moe[l], w_router_group[l], b_router_group[l],
                                w_router_expert[l], b_router_expert[l])
        buf_tok, block_e, block_valid, dest = _route_plan(ri[:, :TOP_K_EXPERTS])
        ys = _experts(xn, buf_tok, block_e, block_valid, w_exp_gate[l], w_exp_up[l], w_exp_down[l])
        x2 = _combine(x1, rf, ys, dest)
    return x2.reshape(bsz, seq, dm)
```

```python
import functools

import numpy as np
import jax
import jax.numpy as jnp
from jax import lax
from jax.experimental import pallas as pl
from jax.experimental.pallas import tpu as pltpu

F32 = jnp.float32
BF16 = jnp.bfloat16
I32 = jnp.int32

D_MODEL = 1024
CHUNK = 64
Q_BLOCK = 128
EPS = 1e-6
NEG_INF = -1e30

ATT_HEADS = 8
ATT_KV_HEADS = 2
ATT_HEAD_DIM = 64
ATT_Q_WIDTH = ATT_HEADS * ATT_HEAD_DIM
ATT_KV_WIDTH = ATT_KV_HEADS * ATT_HEAD_DIM
IDX_HEADS = 8
IDX_DIM = 32
TOPK_MAX = 256
ROPE_THETA = 500000.0
ROPE_FRACTION = 4

MLSTM_HEADS = 4
MLSTM_HEAD_DIM = 128
MLSTM_WIDTH = MLSTM_HEADS * MLSTM_HEAD_DIM
CONV_WIDTH = 4

N_GROUPS = 4
EXPERTS_PER_GROUP = 8
N_EXPERTS = N_GROUPS * EXPERTS_PER_GROUP
TOP_K_EXPERTS = 2
D_EXPERT = 512

LANES = 128
SUBLANES = 8
VMEM_LIMIT_BYTES = 56 * 1024 * 1024

SMALL_IK = 0
SMALL_IW = IDX_DIM
SMALL_MI = SMALL_IW + IDX_HEADS
SMALL_MF = SMALL_MI + MLSTM_HEADS

COL_AQ = 0
COL_AK = COL_AQ + ATT_Q_WIDTH
COL_AV = COL_AK + ATT_KV_WIDTH
COL_IQ = COL_AV + ATT_KV_WIDTH
COL_SMALL = COL_IQ + IDX_HEADS * IDX_DIM
COL_MQK = COL_SMALL + LANES
COL_MV = COL_MQK + 2 * MLSTM_WIDTH
COL_MO = COL_MV + MLSTM_WIDTH
COL_GA = COL_MO + MLSTM_WIDTH
COL_GM = COL_GA + D_MODEL
COL_END = COL_GM + D_MODEL

KEY_TILE = 512
MLSTM_CHUNK = 256
MERGE_ROWS = 512
EXPERT_ROWS = 256
COMBINE_ROWS = 256
WORD_BITS = 32
PLANE_GROUP = WORD_BITS * SUBLANES

ROW_TILE_SUBLANES = D_MODEL // LANES
DMA_PRIORITIES = 2

LOG2_E = 1.4426950408889634
_INT_MIN = -(2 ** 31)


def _sortable_key_const(v):
    b = int(np.array(v, np.float32).view(np.int32))
    return b if b >= 0 else b ^ 0x7FFFFFFF


_KEY_NEG_INF = _sortable_key_const(NEG_INF)
_KEY_VALID = _sortable_key_const(NEG_INF * 0.5)


def _split3(a):
    hi = a.astype(BF16)
    r1 = a - hi.astype(F32)
    mid = r1.astype(BF16)
    lo = (r1 - mid.astype(F32)).astype(BF16)
    return hi, mid, lo


def _dot_exact_rhs(a, b_bf16):
    hi, mid, lo = _split3(a)
    d = functools.partial(jnp.dot, preferred_element_type=F32)
    return d(hi, b_bf16) + d(mid, b_bf16) + d(lo, b_bf16)


def _dot_exact_lhs(a_bf16, b):
    hi, mid, lo = _split3(b)
    d = functools.partial(jnp.dot, preferred_element_type=F32)
    return d(a_bf16, hi) + d(a_bf16, mid) + d(a_bf16, lo)


def _dot_nt(a, b):
    return lax.dot_general(a, b, (((1,), (1,)), ((), ())), preferred_element_type=F32)


def _sigmoid(x):
    return 1.0 / (1.0 + jnp.exp(-x))


def _in_proj_body(x_ref, pos_ref, g_ref, w_ref, rc_ref, bd_ref, gq_ref, gk_ref,
                  aq_ref, ak_ref, av_ref, iq_ref, ik_ref, small_ref, mqk_ref, mv_ref,
                  og_ref, ga_ref, gm_ref):
    x = x_ref[...]
    ms = jnp.mean(x * x, axis=-1, keepdims=True)
    h = (x * lax.rsqrt(ms + EPS) * g_ref[...]).astype(BF16)
    posf = pos_ref[...].astype(F32)

    def proj(lo, hi):
        return jnp.dot(h, w_ref[:, lo:hi], preferred_element_type=F32)

    def rope(t, cos, s_up, s_dn, half):
        n = t.shape[-1]
        return t * cos + pltpu.roll(t, half, 1) * s_up + pltpu.roll(t, n - half, 1) * s_dn

    def head_norm(t, gain):
        ssum = _dot_exact_rhs(t * t, bd_ref[...])
        return t * lax.rsqrt(ssum * (1.0 / ATT_HEAD_DIM) + EPS) * gain

    ang_a = posf * rc_ref[0:1, :]
    cos_a, sin_a = jnp.cos(ang_a), jnp.sin(ang_a)
    up_a, dn_a = sin_a * rc_ref[1:2, :], sin_a * rc_ref[2:3, :]
    ang_i = posf * rc_ref[3:4, :]
    cos_i, sin_i = jnp.cos(ang_i), jnp.sin(ang_i)
    up_i, dn_i = sin_i * rc_ref[4:5, :], sin_i * rc_ref[5:6, :]
    small_mask = rc_ref[6:7, :]
    cos_s = 1.0 + (cos_i - 1.0) * small_mask
    up_s, dn_s = up_i * small_mask, dn_i * small_mask

    att_half = ATT_HEAD_DIM // ROPE_FRACTION // 2
    idx_half = IDX_DIM // ROPE_FRACTION // 2
    q_scale = ATT_HEAD_DIM ** -0.5 * LOG2_E
    tm = x.shape[0]

    def store_qblock_t(ref, c, t):
        for r in range(tm // Q_BLOCK):
            ref[r, c * LANES:(c + 1) * LANES, :] = t[r * Q_BLOCK:(r + 1) * Q_BLOCK, :].T.astype(BF16)

    for c in range(ATT_Q_WIDTH // LANES):
        t = proj(COL_AQ + c * LANES, COL_AQ + (c + 1) * LANES)
        t = rope(head_norm(t, gq_ref[...]), cos_a, up_a, dn_a, att_half)
        store_qblock_t(aq_ref, c, t * q_scale)
    t = proj(COL_AK, COL_AK + LANES)
    ak_ref[...] = rope(head_norm(t, gk_ref[...]), cos_a, up_a, dn_a, att_half).astype(BF16)
    av_ref[0] = proj(COL_AV, COL_AV + LANES).T.astype(BF16)
    for c in range(IDX_HEADS * IDX_DIM // LANES):
        t = proj(COL_IQ + c * LANES, COL_IQ + (c + 1) * LANES)
        store_qblock_t(iq_ref, c, rope(t, cos_i, up_i, dn_i, idx_half))
    t = rope(proj(COL_SMALL, COL_SMALL + LANES), cos_s, up_s, dn_s, idx_half)
    small_ref[...] = t
    ik_ref[...] = t[:, SMALL_IK:SMALL_IK + IDX_DIM].astype(BF16)
    for c in range(2 * MLSTM_WIDTH // 512):
        mqk_ref[:, c * 512:(c + 1) * 512] = proj(COL_MQK + c * 512, COL_MQK + (c + 1) * 512)
    mv_ref[...] = proj(COL_MV, COL_MV + MLSTM_WIDTH).astype(BF16)
    og_ref[...] = _sigmoid(proj(COL_MO, COL_MO + MLSTM_WIDTH)).astype(BF16)
    for c in range(D_MODEL // 512):
        ga_ref[:, c * 512:(c + 1) * 512] = _sigmoid(
            proj(COL_GA + c * 512, COL_GA + (c + 1) * 512)).astype(BF16)
        gm_ref[:, c * 512:(c + 1) * 512] = _sigmoid(
            proj(COL_GM + c * 512, COL_GM + (c + 1) * 512)).astype(BF16)


def _rope_consts():
    lane = np.arange(LANES)

    def inv_freq(rot):
        half = rot // 2
        return jnp.float32(ROPE_THETA) ** (-jnp.arange(half, dtype=F32) * 2.0 / rot)

    rot_a = ATT_HEAD_DIM // ROPE_FRACTION
    rot_i = IDX_DIM // ROPE_FRACTION
    fa, fi = inv_freq(rot_a), inv_freq(rot_i)
    ja, ji = lane % ATT_HEAD_DIM, lane % IDX_DIM
    rc = jnp.zeros((16, LANES), F32)
    rc = rc.at[0].set(jnp.where(ja < rot_a, fa[ja % (rot_a // 2)], 0.0))
    rc = rc.at[1].set(jnp.asarray(((ja >= rot_a // 2) & (ja < rot_a)).astype(np.float32)))
    rc = rc.at[2].set(jnp.asarray(-(ja < rot_a // 2).astype(np.float32)))
    rc = rc.at[3].set(jnp.where(ji < rot_i, fi[ji % (rot_i // 2)], 0.0))
    rc = rc.at[4].set(jnp.asarray(((ji >= rot_i // 2) & (ji < rot_i)).astype(np.float32)))
    rc = rc.at[5].set(jnp.asarray(-(ji < rot_i // 2).astype(np.float32)))
    rc = rc.at[6].set(jnp.asarray((lane < IDX_DIM).astype(np.float32)))
    return rc


def _pack_w_in(w_in):
    sizes = (ATT_Q_WIDTH, ATT_KV_WIDTH, ATT_KV_WIDTH, IDX_HEADS * IDX_DIM, IDX_DIM, IDX_HEADS,
             MLSTM_WIDTH, MLSTM_WIDTH, MLSTM_WIDTH, MLSTM_WIDTH, MLSTM_HEADS, MLSTM_HEADS,
             D_MODEL, D_MODEL)
    pts = np.cumsum(sizes)[:-1].tolist()
    (aq, ak, av, iq, ik, iw, mq, mk, mv, mo, mi, mf, ga, gm) = jnp.split(w_in, pts, axis=-1)
    pad = jnp.zeros((D_MODEL, LANES - (SMALL_MF + MLSTM_HEADS)), w_in.dtype)
    small = jnp.concatenate([ik, iw, mi, mf, pad], axis=-1)
    return jnp.concatenate([aq, ak, av, iq, small, mq, mk, mv, mo, ga, gm], axis=-1).astype(BF16)


def _in_proj(x2, pos2, g_norm_mix, w_in, g_q_att, g_k_att):
    t_rows = x2.shape[0]
    tm = KEY_TILE
    w = _pack_w_in(w_in)
    lane = np.arange(LANES)
    bd = jnp.asarray((lane[:, None] // ATT_HEAD_DIM == lane[None, :] // ATT_HEAD_DIM)
                     .astype(np.float32)).astype(BF16)
    reps = LANES // ATT_HEAD_DIM
    gq = jnp.tile(g_q_att.astype(F32), reps)[None, :]
    gk = jnp.tile(g_k_att.astype(F32), reps)[None, :]

    def row(width):
        return pl.BlockSpec((tm, width), lambda i: (i, 0))

    def const(shape):
        return pl.BlockSpec(shape, lambda i: (0, 0))

    qb = tm // Q_BLOCK

    def qblock_t(width):
        return pl.BlockSpec((qb, width, Q_BLOCK), lambda i: (i, 0, 0))

    out_shapes = (
        jax.ShapeDtypeStruct((t_rows // Q_BLOCK, ATT_Q_WIDTH, Q_BLOCK), BF16),
        jax.ShapeDtypeStruct((t_rows, ATT_KV_WIDTH), BF16),
        jax.ShapeDtypeStruct((t_rows // tm, ATT_KV_WIDTH, tm), BF16),
        jax.ShapeDtypeStruct((t_rows // Q_BLOCK, IDX_HEADS * IDX_DIM, Q_BLOCK), BF16),
        jax.ShapeDtypeStruct((t_rows, IDX_DIM), BF16),
        jax.ShapeDtypeStruct((t_rows, LANES), F32),
        jax.ShapeDtypeStruct((t_rows, 2 * MLSTM_WIDTH), F32),
        jax.ShapeDtypeStruct((t_rows, MLSTM_WIDTH), BF16),
        jax.ShapeDtypeStruct((t_rows, MLSTM_WIDTH), BF16),
        jax.ShapeDtypeStruct((t_rows, D_MODEL), BF16),
        jax.ShapeDtypeStruct((t_rows, D_MODEL), BF16),
    )
    return pl.pallas_call(
        _in_proj_body,
        grid=(t_rows // tm,),
        in_specs=[row(D_MODEL), row(1), const((1, D_MODEL)), const((D_MODEL, COL_END)),
                  const((16, LANES)), const((LANES, LANES)), const((1, LANES)), const((1, LANES))],
        out_specs=[qblock_t(ATT_Q_WIDTH), row(ATT_KV_WIDTH),
                   pl.BlockSpec((1, ATT_KV_WIDTH, tm), lambda i: (i, 0, 0)),
                   qblock_t(IDX_HEADS * IDX_DIM)] + [row(s.shape[1]) for s in out_shapes[4:]],
        out_shape=out_shapes,
        compiler_params=pltpu.CompilerParams(
            dimension_semantics=("arbitrary",), vmem_limit_bytes=VMEM_LIMIT_BYTES),
        name="in_proj",
    )(x2, pos2, g_norm_mix.astype(F32)[None, :], w, _rope_consts(), bd, gq, gk)


def _bit_transpose32(words):
    a = list(words)
    j, m = 16, 0x0000FFFF
    while j:
        k = 0
        while k < WORD_BITS:
            t = (a[k] ^ lax.shift_right_logical(a[k + j], jnp.int32(j))) & m
            a[k] = a[k] ^ t
            a[k + j] = a[k + j] ^ lax.shift_left(t, jnp.int32(j))
            k = (k + j + 1) & ~j
        j >>= 1
        m = m ^ (m << j)
    return a


def _dsa_body(aq_ref, iq_ref, small_ref, k_ref, vt_ref, ik_ref, out_ref,
              key_scr, plane_scr, qi_scr, q2_scr, p_scr, acc_scr, *, topk, tk):
    blk = pl.program_id(1)
    nq = Q_BLOCK
    rep = ATT_HEADS // ATT_KV_HEADS
    groups_per_tile = tk // PLANE_GROUP
    n_tiles = lax.shift_right_logical(blk * nq + nq + tk - 1, int(np.log2(tk)))
    idx_scale = (IDX_DIM * IDX_HEADS) ** -0.5
    plane_rows = plane_scr.shape[1]

    @pl.when((pl.program_id(0) == 0) & (blk == 0))
    def _():
        plane_scr[...] = jnp.zeros_like(plane_scr)

    for h in range(IDX_HEADS):
        qi_scr[:, h * nq:(h + 1) * nq] = iq_ref[0, h * IDX_DIM:(h + 1) * IDX_DIM, :]
    w_rows = small_ref[...].T[SMALL_IW:SMALL_IW + IDX_HEADS, :] * idx_scale

    q_chunk = lax.shift_right_logical(
        blk * nq + lax.broadcasted_iota(I32, (1, nq), 1), int(np.log2(CHUNK)))
    key_row = lax.broadcasted_iota(I32, (tk, 1), 0)

    def score_tile(t, carry):
        start = pl.multiple_of(t * tk, tk)
        rel = jnp.dot(ik_ref[pl.ds(start, tk), :], qi_scr[...], preferred_element_type=F32)
        sc = jnp.maximum(rel[:, 0:nq], 0.0) * w_rows[0:1]
        for h in range(1, IDX_HEADS):
            sc = sc + jnp.maximum(rel[:, h * nq:(h + 1) * nq], 0.0) * w_rows[h:h + 1]
        visible = lax.shift_right_logical(start + key_row, int(np.log2(CHUNK))) <= q_chunk
        sc = jnp.where(visible, sc, NEG_INF)
        bits = lax.bitcast_convert_type(sc, I32)
        key = jnp.where(bits < 0, bits ^ 0x7FFFFFFF, bits)
        key_scr[pl.ds(start, tk), :] = key
        biased = key ^ _INT_MIN
        for g in range(groups_per_tile):
            words = [biased[g * PLANE_GROUP + i * SUBLANES:g * PLANE_GROUP + (i + 1) * SUBLANES, :]
                     for i in range(WORD_BITS)]
            planes = _bit_transpose32(words)
            row0 = pl.multiple_of((t * groups_per_tile + g) * SUBLANES, SUBLANES)
            for p in range(WORD_BITS):
                plane_scr[p, pl.ds(row0, SUBLANES), :] = planes[p]
        return carry

    lax.fori_loop(0, n_tiles, score_tile, 0)

    group_of_row = lax.shift_right_logical(
        lax.broadcasted_iota(I32, (plane_rows, nq), 0), int(np.log2(SUBLANES)))
    alive = jnp.where(group_of_row < n_tiles * groups_per_tile, -1, 0)
    k_rem = jnp.full((1, nq), float(topk), F32)
    thr_u = jnp.zeros((1, nq), I32)

    def lane_count(mask_words):
        pc = lax.population_count(mask_words)
        part = jnp.sum(pc.reshape(plane_rows // SUBLANES, SUBLANES, nq), axis=0)
        return jnp.sum(part.astype(F32), axis=0, keepdims=True)

    for p in range(WORD_BITS):
        bit_value = _INT_MIN if p == 0 else 1 << (WORD_BITS - 1 - p)
        ones = alive & plane_scr[p]
        c_ones = lane_count(ones)
        take = c_ones >= k_rem
        alive = jnp.where(take, ones, alive ^ ones)
        k_rem = jnp.where(take, k_rem, k_rem - c_ones)
        thr_u = jnp.where(take, thr_u | bit_value, thr_u)
    thr = thr_u ^ _INT_MIN
    quota = k_rem

    q2_scr[...] = jnp.zeros_like(q2_scr)
    for h in range(ATT_HEADS):
        g = h // rep
        q2_scr[g * ATT_HEAD_DIM:(g + 1) * ATT_HEAD_DIM, h * nq:(h + 1) * nq] = (
            aq_ref[0, h * ATT_HEAD_DIM:(h + 1) * ATT_HEAD_DIM, :])

    def attend(with_ties):
        acc_scr[...] = jnp.zeros_like(acc_scr)
        if with_ties:
            rr = lax.broadcasted_iota(I32, (tk, tk), 0)
            cc = lax.broadcasted_iota(I32, (tk, tk), 1)
            earlier = jnp.where(cc < rr, 1.0, 0.0).astype(BF16)

        def tile(t, carry):
            m_all, l_all, eq_seen = carry
            start = pl.multiple_of(t * tk, tk)
            key = key_scr[pl.ds(start, tk), :]
            valid = key > _KEY_VALID
            if with_ties:
                eq = key == thr
                eqf = jnp.where(eq, 1.0, 0.0)
                rank = jnp.dot(earlier, eqf.astype(BF16), preferred_element_type=F32) + eq_seen
                sel = valid & ((key > thr) | (eq & (rank < quota)))
                eq_seen = eq_seen + jnp.sum(eqf, axis=0, keepdims=True)
            else:
                sel = valid & (key >= thr)
            bias = jnp.where(sel, 0.0, NEG_INF)
            s = jnp.dot(k_ref[pl.ds(start, tk), :], q2_scr[...], preferred_element_type=F32)
            m_rows, l_rows, a_rows = [], [], []
            for h in range(ATT_HEADS):
                cols = slice(h * nq, (h + 1) * nq)
                sh = s[:, cols] + bias
                m_old = m_all[h:h + 1]
                m_new = jnp.maximum(m_old, jnp.max(sh, axis=0, keepdims=True))
                p = jnp.exp2(sh - m_new)
                alpha = jnp.exp2(m_old - m_new)
                l_rows.append(alpha * l_all[h:h + 1] + jnp.sum(p, axis=0, keepdims=True))
                p_scr[:, cols] = p.astype(BF16)
                m_rows.append(m_new)
                a_rows.append(alpha)
            acc_scr[...] = (acc_scr[...] * jnp.concatenate(a_rows, axis=1)
                            + jnp.dot(vt_ref[t], p_scr[...], preferred_element_type=F32))
            return jnp.concatenate(m_rows, axis=0), jnp.concatenate(l_rows, axis=0), eq_seen

        m_all, l_all, _ = lax.fori_loop(
            0, n_tiles, tile,
            (jnp.full((ATT_HEADS, nq), NEG_INF, F32), jnp.zeros((ATT_HEADS, nq), F32),
             jnp.zeros((1, nq), F32)))
        inv_l = 1.0 / l_all
        for j in range(ATT_HEADS // 2):
            g = (2 * j) // rep
            halves = [acc_scr[g * ATT_HEAD_DIM:(g + 1) * ATT_HEAD_DIM, h * nq:(h + 1) * nq]
                      * inv_l[h:h + 1] for h in (2 * j, 2 * j + 1)]
            out_ref[:, j * LANES:(j + 1) * LANES] = jnp.concatenate(halves, axis=0).T.astype(BF16)

    tie_lanes = jnp.where((lane_count(alive) > quota) & (thr > _KEY_NEG_INF), 1.0, 0.0)
    need_ties = jnp.max(tie_lanes) > 0.5

    @pl.when(need_ties)
    def _():
        attend(True)

    @pl.when(jnp.logical_not(need_ties))
    def _():
        attend(False)


def _dsa(aq_t, iq_t, small, ak, av_t, ik, bsz, seq):
    nb = seq // Q_BLOCK
    tk = KEY_TILE
    topk = min(TOPK_MAX, seq // 4)
    assert seq % tk == 0 and tk >= topk and tk & (tk - 1) == 0 and tk % PLANE_GROUP == 0
    nkt = seq // tk

    def qspec_t(width):
        return pl.BlockSpec((1, width, Q_BLOCK), lambda b, j: (b * nb + j, 0, 0))

    def qspec(width):
        return pl.BlockSpec((Q_BLOCK, width), lambda b, j: (b * nb + j, 0))

    def kspec(width):
        return pl.BlockSpec((seq, width), lambda b, j: (b, 0))

    cols = ATT_HEADS * Q_BLOCK
    return pl.pallas_call(
        functools.partial(_dsa_body, topk=topk, tk=tk),
        grid=(bsz, nb),
        in_specs=[qspec_t(ATT_Q_WIDTH), qspec_t(IDX_HEADS * IDX_DIM), qspec(LANES),
                  kspec(ATT_KV_WIDTH),
                  pl.BlockSpec((nkt, ATT_KV_WIDTH, tk), lambda b, j: (b, 0, 0)),
                  kspec(IDX_DIM)],
        out_specs=qspec(ATT_Q_WIDTH),
        out_shape=jax.ShapeDtypeStruct((bsz * seq, ATT_Q_WIDTH), BF16),
        scratch_shapes=[
            pltpu.VMEM((seq, Q_BLOCK), I32),
            pltpu.VMEM((WORD_BITS, seq // WORD_BITS, Q_BLOCK), I32),
            pltpu.VMEM((IDX_DIM, cols), BF16),
            pltpu.VMEM((ATT_KV_WIDTH, cols), BF16),
            pltpu.VMEM((tk, cols), BF16),
            pltpu.VMEM((ATT_KV_WIDTH, cols), F32),
        ],
        compiler_params=pltpu.CompilerParams(
            dimension_semantics=("arbitrary", "arbitrary"), vmem_limit_bytes=VMEM_LIMIT_BYTES),
        name="dsa_attention",
    )(aq_t, iq_t, small, ak, av_t, ik)


def _mlstm_body(mqk_ref, mv_ref, og_ref, small_ref, cw_ref, cb_ref, gb_ref, gn_ref, y_ref,
                ubuf, c_scr, n_scr, m_scr, *, chunk):
    ln = chunk
    d = MLSTM_HEAD_DIM
    halo = SUBLANES

    @pl.when(pl.program_id(1) == 0)
    def _():
        ubuf[0:halo, :] = jnp.zeros((halo, 2 * MLSTM_WIDTH), F32)
        c_scr[...] = jnp.zeros_like(c_scr)
        n_scr[...] = jnp.zeros_like(n_scr)
        m_scr[...] = jnp.zeros_like(m_scr)

    ubuf[halo:halo + ln, :] = mqk_ref[...]
    conv = cb_ref[...]
    for j in range(CONV_WIDTH):
        off = halo - (CONV_WIDTH - 1) + j
        conv = conv + cw_ref[j:j + 1, :] * ubuf[off:off + ln, :]
    ubuf[0:halo, :] = ubuf[ln:ln + halo, :]
    qk = conv * _sigmoid(conv)
    q_all = qk[:, :MLSTM_WIDTH]
    k_all = qk[:, MLSTM_WIDTH:] * (d ** -0.5)

    sm = small_ref[...] + gb_ref[...]
    lf = jnp.minimum(sm, 0.0) - jnp.log(1.0 + jnp.exp(-jnp.abs(sm)))
    sm_t = sm.T
    lf_t = lf.T
    rr = lax.broadcasted_iota(I32, (ln, ln), 0)
    cc = lax.broadcasted_iota(I32, (ln, ln), 1)
    causal = cc <= rr
    tri = jnp.where(causal, 1.0, 0.0).astype(BF16)
    tri_t = jnp.where(rr <= cc, 1.0, 0.0).astype(BF16)
    b_cols = _dot_exact_lhs(tri, lf)
    b_rows = _dot_exact_rhs(lf_t, tri_t)

    for h in range(MLSTM_HEADS):
        hs = slice(h * d, (h + 1) * d)
        ig_col = sm[:, SMALL_MI + h:SMALL_MI + h + 1]
        ig_row = sm_t[SMALL_MI + h:SMALL_MI + h + 1, :]
        b_col = b_cols[:, SMALL_MF + h:SMALL_MF + h + 1]
        b_row = b_rows[SMALL_MF + h:SMALL_MF + h + 1, :]
        m_st = m_scr[h:h + 1, 0:1]
        c_st = c_scr[h]
        n_st = n_scr[h:h + 1, :]

        dmat = jnp.where(causal, b_col - b_row + ig_row, NEG_INF)
        inter = b_col + m_st
        m_t = jnp.maximum(inter, jnp.max(dmat, axis=-1, keepdims=True))
        w_intra = jnp.exp(dmat - m_t)
        w_inter = jnp.exp(inter - m_t)
        qf = q_all[:, hs]
        kf = k_all[:, hs]
        qh = qf.astype(BF16)
        vh = mv_ref[:, hs]
        s = _dot_nt(qh, kf.astype(BF16)) * w_intra
        num = (w_inter * jnp.dot(qh, c_st.astype(BF16), preferred_element_type=F32)
               + jnp.dot(s.astype(BF16), vh, preferred_element_type=F32))
        den = (w_inter * jnp.sum(qf * n_st, axis=-1, keepdims=True)
               + jnp.sum(s, axis=-1, keepdims=True))
        hh = num / jnp.maximum(jnp.abs(den), jnp.exp(-m_t))

        b_last = b_col[ln - 1:ln, :]
        gdec = b_last - b_col + ig_col
        m_new = jnp.maximum(b_last + m_st, jnp.max(gdec, axis=0, keepdims=True))
        decay = jnp.exp(b_last + m_st - m_new)
        kw = kf * jnp.exp(gdec - m_new)
        c_scr[h] = decay * c_st + jnp.dot(kw.T.astype(BF16), vh, preferred_element_type=F32)
        n_scr[h:h + 1, :] = decay * n_st + jnp.sum(kw, axis=0, keepdims=True)
        m_scr[h:h + 1, :] = jnp.broadcast_to(m_new, (1, LANES))

        ms = jnp.mean(hh * hh, axis=-1, keepdims=True)
        yh = hh * lax.rsqrt(ms + EPS) * gn_ref[:, hs]
        y_ref[:, hs] = (og_ref[:, hs].astype(F32) * yh).astype(BF16)


def _mlstm(mqk, mv, og, small, w_conv, b_conv, b_igate, b_fgate, g_norm_out, bsz, seq):
    ln = min(MLSTM_CHUNK, seq)
    nc = seq // ln
    gbias = jnp.zeros((1, LANES), F32)
    gbias = gbias.at[0, SMALL_MI:SMALL_MI + MLSTM_HEADS].set(b_igate.astype(F32))
    gbias = gbias.at[0, SMALL_MF:SMALL_MF + MLSTM_HEADS].set(b_fgate.astype(F32))

    def row(width):
        return pl.BlockSpec((ln, width), lambda b, c: (b * nc + c, 0))

    def const(shape):
        return pl.BlockSpec(shape, lambda b, c: (0, 0))

    return pl.pallas_call(
        functools.partial(_mlstm_body, chunk=ln),
        grid=(bsz, nc),
        in_specs=[row(2 * MLSTM_WIDTH), row(MLSTM_WIDTH), row(MLSTM_WIDTH), row(LANES),
                  const((CONV_WIDTH, 2 * MLSTM_WIDTH)), const((1, 2 * MLSTM_WIDTH)),
                  const((1, LANES)), const((1, MLSTM_WIDTH))],
        out_specs=row(MLSTM_WIDTH),
        out_shape=jax.ShapeDtypeStruct((bsz * seq, MLSTM_WIDTH), BF16),
        scratch_shapes=[
            pltpu.VMEM((ln + SUBLANES, 2 * MLSTM_WIDTH), F32),
            pltpu.VMEM((MLSTM_HEADS, MLSTM_HEAD_DIM, MLSTM_HEAD_DIM), F32),
            pltpu.VMEM((SUBLANES, MLSTM_HEAD_DIM), F32),
            pltpu.VMEM((SUBLANES, LANES), F32),
        ],
        compiler_params=pltpu.CompilerParams(
            dimension_semantics=("arbitrary", "arbitrary"), vmem_limit_bytes=VMEM_LIMIT_BYTES),
        name="mlstm",
    )(mqk, mv, og, small, w_conv.astype(F32), b_conv.astype(F32)[None, :], gbias,
      g_norm_out.astype(F32)[None, :])


def _merge_body(x_ref, ya_ref, ym_ref, ga_ref, gm_ref, wpa_ref, wpm_ref, wo_ref, gmoe_ref,
                wrh_ref, wrl_ref, br_ref, x1_ref, xn_ref, ri_ref, rf_ref):
    pa = jnp.dot(ya_ref[...], wpa_ref[...], preferred_element_type=F32)
    pm = jnp.dot(ym_ref[...], wpm_ref[...], preferred_element_type=F32)
    merged = ga_ref[...].astype(F32) * pa + gm_ref[...].astype(F32) * pm
    x1 = x_ref[...] + jnp.dot(merged.astype(BF16), wo_ref[...], preferred_element_type=F32)
    x1_ref[...] = x1
    ms = jnp.mean(x1 * x1, axis=-1, keepdims=True)
    xn = x1 * lax.rsqrt(ms + EPS) * gmoe_ref[...]
    for s in range(ROW_TILE_SUBLANES):
        xn_ref[:, s, :] = xn[:, s * LANES:(s + 1) * LANES]

    xh = xn.astype(BF16)
    xl = (xn - xh.astype(F32)).astype(BF16)
    logits = (jnp.dot(xh, wrh_ref[...], preferred_element_type=F32)
              + jnp.dot(xh, wrl_ref[...], preferred_element_type=F32)
              + jnp.dot(xl, wrh_ref[...], preferred_element_type=F32)) + br_ref[...]
    lane = lax.broadcasted_iota(I32, logits.shape, 1).astype(F32)

    def first_argmax(vals, mask):
        v = jnp.where(mask, vals, -jnp.inf)
        top = jnp.max(v, axis=-1, keepdims=True)
        idx = jnp.min(jnp.where(mask & (v == top), lane, float(LANES)), axis=-1, keepdims=True)
        return top, idx

    is_group = lane < N_GROUPS
    g_max, g_sel = first_argmax(logits, is_group)
    g_den = jnp.sum(jnp.where(is_group, jnp.exp(logits - g_max), 0.0), axis=-1, keepdims=True)
    g_top = 1.0 / g_den
    lo = N_GROUPS + g_sel * EXPERTS_PER_GROUP
    in_group = (lane >= lo) & (lane < lo + EXPERTS_PER_GROUP)
    v1, i1 = first_argmax(logits, in_group)
    v2, i2 = first_argmax(logits, in_group & (lane != i1))
    e2 = jnp.exp(v2 - v1)
    p1 = 1.0 / (1.0 + e2)
    gate1 = p1 * g_top
    gate2 = (e2 * p1) * g_top
    ri_ref[...] = jnp.where(lane == 0, i1 - N_GROUPS,
                            jnp.where(lane == 1, i2 - N_GROUPS, 0.0)).astype(I32)
    rf_ref[...] = jnp.where(lane == 0, gate1, jnp.where(lane == 1, gate2, 0.0))


def _merge(x2, y_att, y_ml, ga, gm, w_proj_att, w_proj_mlstm, w_out, g_norm_moe,
           w_router_group, b_router_group, w_router_expert, b_router_expert):
    t_rows = x2.shape[0]
    tm = min(MERGE_ROWS, t_rows)
    n_r = N_GROUPS + N_EXPERTS
    wr = jnp.concatenate([w_router_group, w_router_expert,
                          jnp.zeros((D_MODEL, LANES - n_r), F32)], axis=-1).astype(F32)
    wr_hi = wr.astype(BF16)
    wr_lo = (wr - wr_hi.astype(F32)).astype(BF16)
    br =jnp.concatenate([b_router_group, b_router_expert,
                          jnp.zeros((LANES - n_r,), F32)]).astype(F32)[None, :]

    def row(width):
        return pl.BlockSpec((tm, width), lambda i: (i, 0))

    def const(shape):
        return pl.BlockSpec(shape, lambda i: (0, 0))

    out_shapes = (
        jax.ShapeDtypeStruct((t_rows, D_MODEL), F32),
        jax.ShapeDtypeStruct((t_rows, ROW_TILE_SUBLANES, LANES), F32),
        jax.ShapeDtypeStruct((t_rows, LANES), I32),
        jax.ShapeDtypeStruct((t_rows, LANES), F32),
    )
    return pl.pallas_call(
        _merge_body,
        grid=(t_rows // tm,),
        in_specs=[row(D_MODEL), row(ATT_Q_WIDTH), row(MLSTM_WIDTH), row(D_MODEL), row(D_MODEL),
                  const((ATT_Q_WIDTH, D_MODEL)), const((MLSTM_WIDTH, D_MODEL)),
                  const((D_MODEL, D_MODEL)), const((1, D_MODEL)), const((D_MODEL, LANES)),
                  const((D_MODEL, LANES)), const((1, LANES))],
        out_specs=[row(D_MODEL),
                   pl.BlockSpec((tm, ROW_TILE_SUBLANES, LANES), lambda i: (i, 0, 0)),
                   row(LANES), row(LANES)],
        out_shape=out_shapes,
        compiler_params=pltpu.CompilerParams(
            dimension_semantics=("arbitrary",), vmem_limit_bytes=VMEM_LIMIT_BYTES),
        name="merge_route",
    )(x2, y_att, y_ml, ga, gm, w_proj_att.astype(BF16), w_proj_mlstm.astype(BF16),
      w_out.astype(BF16), g_norm_moe.astype(F32)[None, :], wr_hi, wr_lo, br)


def _expert_body(be_ref, tok_cur_ref, tok_nxt_ref, xn_hbm, wg_ref, wu_ref, wd_ref,
                 ys_ref, xbuf, sem, *, rows):
    i = pl.program_id(0)
    nblk = pl.num_programs(0)
    slot = lax.rem(i, 2)
    nslot = 1 - slot

    def row_copy(tok, dst_slot, r):
        return pltpu.make_async_copy(
            xn_hbm.at[pl.ds(tok, 1)], xbuf.at[dst_slot, pl.ds(r, 1)], sem.at[dst_slot])

    def all_rows(dst_slot):
        return pltpu.make_async_copy(
            xn_hbm.at[pl.ds(0, rows)], xbuf.at[dst_slot], sem.at[dst_slot])

    @pl.when(i == 0)
    def _():
        for r in range(rows):
            row_copy(tok_cur_ref[0, 0, r], 0, r).start(priority=r % DMA_PRIORITIES)

    all_rows(slot).wait()

    def prefetch(part):
        for r in range(part * rows // 4, (part + 1) * rows // 4):
            row_copy(tok_nxt_ref[0, 0, r], nslot, r).start(priority=r % DMA_PRIORITIES)

    prefetch(0)
    xb = jnp.concatenate([xbuf[slot, :, s, :] for s in range(ROW_TILE_SUBLANES)],
                         axis=1).astype(BF16)
    a = jnp.dot(xb, wg_ref[0], preferred_element_type=F32)
    prefetch(1)
    u = jnp.dot(xb, wu_ref[0], preferred_element_type=F32)
    prefetch(2)
    hid = (a * _sigmoid(a) * u).astype(BF16)
    prefetch(3)
    y = jnp.dot(hid, wd_ref[0], preferred_element_type=F32)
    for s in range(ROW_TILE_SUBLANES):
        ys_ref[:, s, :] = y[:, s * LANES:(s + 1) * LANES]

    @pl.when(i == nblk - 1)
    def _():
        all_rows(nslot).wait()


def _experts(xn, buf_tok, block_e, w_gate, w_up, w_down):
    rows = EXPERT_ROWS
    cap = buf_tok.shape[0]
    nblk = cap // rows
    tok3 = buf_tok.reshape(nblk, 1, rows)
    grid_spec = pltpu.PrefetchScalarGridSpec(
        num_scalar_prefetch=1,
        grid=(nblk,),
        in_specs=[
            pl.BlockSpec((1, 1, rows), lambda i, be: (i, 0, 0), memory_space=pltpu.SMEM),
            pl.BlockSpec((1, 1, rows), lambda i, be: (jnp.minimum(i + 1, nblk - 1), 0, 0),
                         memory_space=pltpu.SMEM),
            pl.BlockSpec(memory_space=pl.ANY),
            pl.BlockSpec((1, D_MODEL, D_EXPERT), lambda i, be: (be[i], 0, 0)),
            pl.BlockSpec((1, D_MODEL, D_EXPERT), lambda i, be: (be[i], 0, 0)),
            pl.BlockSpec((1, D_EXPERT, D_MODEL), lambda i, be: (be[i], 0, 0)),
        ],
        out_specs=pl.BlockSpec((rows, ROW_TILE_SUBLANES, LANES), lambda i, be: (i, 0, 0)),
        scratch_shapes=[pltpu.VMEM((2, rows, ROW_TILE_SUBLANES, LANES), F32),
                        pltpu.SemaphoreType.DMA((2,))],
    )
    return pl.pallas_call(
        functools.partial(_expert_body, rows=rows),
        grid_spec=grid_spec,
        out_shape=jax.ShapeDtypeStruct((cap, ROW_TILE_SUBLANES, LANES), F32),
        compiler_params=pltpu.CompilerParams(
            dimension_semantics=("arbitrary",), vmem_limit_bytes=VMEM_LIMIT_BYTES),
        name="moe_experts",
    )(block_e, tok3, tok3, xn, w_gate.astype(BF16), w_up.astype(BF16), w_down.astype(BF16))


def _combine_body(dcur_ref, dnxt_ref, x1_ref, rf_ref, ys_hbm, out_ref, ybuf, sem, *, rows):
    i = pl.program_id(0)
    nt = pl.num_programs(0)
    slot = lax.rem(i, 2)
    nslot = 1 - slot

    def row_copy(src, dst_slot, k, r):
        return pltpu.make_async_copy(
            ys_hbm.at[pl.ds(src, 1)], ybuf.at[dst_slot, k, pl.ds(r, 1)], sem.at[dst_slot])

    def wait_all(dst_slot):
        for k in range(TOP_K_EXPERTS):
            pltpu.make_async_copy(
                ys_hbm.at[pl.ds(0, rows)], ybuf.at[dst_slot, k], sem.at[dst_slot]).wait()

    def gather(dest_ref, dst_slot):
        for r in range(rows):
            for k in range(TOP_K_EXPERTS):
                row_copy(dest_ref[0, 0, TOP_K_EXPERTS * r + k], dst_slot, k, r).start(
                    priority=k % DMA_PRIORITIES)

    @pl.when(i == 0)
    def _():
        gather(dcur_ref, 0)

    wait_all(slot)
    gather(dnxt_ref, nslot)
    gates = rf_ref[...]
    for s in range(ROW_TILE_SUBLANES):
        cols = slice(s * LANES, (s + 1) * LANES)
        out = x1_ref[:, cols]
        for k in range(TOP_K_EXPERTS):
            out = out + gates[:, k:k + 1] * ybuf[slot, k, :, s, :]
        out_ref[:, cols] = out

    @pl.when(i == nt - 1)
    def _():
        wait_all(nslot)


def _combine(x1, rf, ys, dest):
    t_rows = x1.shape[0]
    rows = min(COMBINE_ROWS, t_rows)
    nt = t_rows // rows
    dest3 = dest.reshape(nt, 1, rows * TOP_K_EXPERTS)
    dspec = functools.partial(pl.BlockSpec, (1, 1, rows * TOP_K_EXPERTS), memory_space=pltpu.SMEM)
    return pl.pallas_call(
        functools.partial(_combine_body, rows=rows),
        grid=(nt,),
        in_specs=[
            dspec(lambda i: (i, 0, 0)),
            dspec(lambda i: (jnp.minimum(i + 1, nt - 1), 0, 0)),
            pl.BlockSpec((rows, D_MODEL), lambda i: (i, 0)),
            pl.BlockSpec((rows, LANES), lambda i: (i, 0)),
            pl.BlockSpec(memory_space=pl.ANY),
        ],
        out_specs=pl.BlockSpec((rows, D_MODEL), lambda i: (i, 0)),
        out_shape=jax.ShapeDtypeStruct((t_rows, D_MODEL), F32),
        scratch_shapes=[pltpu.VMEM((2, TOP_K_EXPERTS, rows, ROW_TILE_SUBLANES, LANES), F32),
                        pltpu.SemaphoreType.DMA((2,))],
        compiler_params=pltpu.CompilerParams(
            dimension_semantics=("arbitrary",), vmem_limit_bytes=VMEM_LIMIT_BYTES),
        name="moe_combine",
    )(dest3, dest3, x1, rf, ys)


def _route_plan(expert_id):
    t_rows = expert_id.shape[0]
    m = t_rows * TOP_K_EXPERTS
    rows = EXPERT_ROWS
    e_flat = expert_id.reshape(m)
    onehot = (e_flat[:, None] == jnp.arange(N_EXPERTS, dtype=I32)[None, :]).astype(I32)
    running = jnp.cumsum(onehot, axis=0)
    counts = running[-1]
    rank = jnp.sum(onehot * running, axis=1) - 1
    padded = ((counts + rows - 1) // rows) * rows
    pends = jnp.cumsum(padded)
    pstarts = pends - padded
    cap = ((m + N_EXPERTS * rows + rows - 1) // rows) * rows
    nblk = cap // rows
    blk_start = jnp.arange(nblk, dtype=I32) * rows
    block_e = jnp.minimum(jnp.sum((pends[None, :] <= blk_start[:, None]).astype(I32), axis=1),
                          N_EXPERTS - 1)
    dest = (jnp.sum(onehot * pstarts[None, :], axis=1) + rank).astype(I32)
    order = jnp.argsort(e_flat, stable=True).astype(I32)
    starts = jnp.cumsum(counts) - counts
    slot = jnp.arange(cap, dtype=I32)
    slot_e = jnp.repeat(block_e, rows)
    off = slot - pstarts[slot_e]
    live = (off < counts[slot_e]) & (slot < pends[-1])
    src = jnp.clip(starts[slot_e] + off, 0, m - 1)
    buf_tok = jnp.where(live, order[src] // TOP_K_EXPERTS, 0).astype(I32)
    return buf_tok, block_e, dest


def kernel(x, positions, g_norm_mix, w_in, g_q_att, g_k_att, w_conv_mlstm, b_conv_mlstm, b_igate,
           b_fgate, g_norm_mlstm_out, w_proj_att, w_proj_mlstm, w_out, g_norm_moe, w_router_group,
           b_router_group, w_router_expert, b_router_expert, w_exp_gate, w_exp_up, w_exp_down):
    bsz, seq, dm = x.shape
    depth = g_norm_mix.shape[0]
    x2 = x.reshape(bsz * seq, dm)
    pos2 = positions.reshape(bsz * seq, 1).astype(I32)
    for l in range(depth):
        aq, ak, av, iq, ik, small, mqk, mv, og, ga, gm = _in_proj(
            x2, pos2, g_norm_mix[l], w_in[l], g_q_att[l], g_k_att[l])
        y_att = _dsa(aq, iq, small, ak, av, ik, bsz, seq)
        y_ml = _mlstm(mqk, mv, og, small, w_conv_mlstm[l], b_conv_mlstm[l], b_igate[l], b_fgate[l],
                      g_norm_mlstm_out[l], bsz, seq)
        x1, xn, ri, rf = _merge(x2, y_att, y_ml, ga, gm, w_proj_att[l], w_proj_mlstm[l], w_out[l],
                                g_norm_moe[l], w_router_group[l], b_router_group[l],
                                w_router_expert[l], b_router_expert[l])
        buf_tok, block_e, dest = _route_plan(ri[:, :TOP_K_EXPERTS])
        ys = _experts(xn, buf_tok, block_e, w_exp_gate[l], w_exp_up[l], w_exp_down[l])
        x2 = _combine(x1, rf, ys, dest)
    return x2.reshape(bsz, seq, dm)
```

```python
import functools

import numpy as np
import jax
import jax.numpy as jnp
from jax import lax
from jax.experimental import pallas as pl
from jax.experimental.pallas import tpu as pltpu

F32 = jnp.float32
BF16 = jnp.bfloat16
I32 = jnp.int32

D_MODEL = 1024
CHUNK = 64
Q_BLOCK = 128
EPS = 1e-6
NEG_INF = -1e30

ATT_HEADS = 8
ATT_KV_HEADS = 2
ATT_HEAD_DIM = 64
ATT_Q_WIDTH = ATT_HEADS * ATT_HEAD_DIM
ATT_KV_WIDTH = ATT_KV_HEADS * ATT_HEAD_DIM
IDX_HEADS = 8
IDX_DIM = 32
TOPK_MAX = 256
ROPE_THETA = 500000.0
ROPE_FRACTION = 4

MLSTM_HEADS = 4
MLSTM_HEAD_DIM = 128
MLSTM_WIDTH = MLSTM_HEADS * MLSTM_HEAD_DIM
CONV_WIDTH = 4

N_GROUPS = 4
EXPERTS_PER_GROUP = 8
N_EXPERTS = N_GROUPS * EXPERTS_PER_GROUP
TOP_K_EXPERTS = 2
D_EXPERT = 512

LANES = 128
SUBLANES = 8
VMEM_LIMIT_BYTES = 56 * 1024 * 1024

SMALL_IK = 0
SMALL_IW = IDX_DIM
SMALL_MI = SMALL_IW + IDX_HEADS
SMALL_MF = SMALL_MI + MLSTM_HEADS

COL_AQ = 0
COL_AK = COL_AQ + ATT_Q_WIDTH
COL_AV = COL_AK + ATT_KV_WIDTH
COL_IQ = COL_AV + ATT_KV_WIDTH
COL_SMALL = COL_IQ + IDX_HEADS * IDX_DIM
COL_MQK = COL_SMALL + LANES
COL_MV = COL_MQK + 2 * MLSTM_WIDTH
COL_MO = COL_MV + MLSTM_WIDTH
COL_GA = COL_MO + MLSTM_WIDTH
COL_GM = COL_GA + D_MODEL
COL_END = COL_GM + D_MODEL

KEY_TILE = 512
MLSTM_CHUNK = 256
MERGE_ROWS = 512
EXPERT_ROWS = 512
COMBINE_ROWS = 256
WORD_BITS = 32
PLANE_GROUP = WORD_BITS * SUBLANES

DMA_PRIORITIES = 2

LOG2_E = 1.4426950408889634
_INT_MIN = -(2 ** 31)


def _sortable_key_const(v):
    b = int(np.array(v, np.float32).view(np.int32))
    return b if b >= 0 else b ^ 0x7FFFFFFF


_KEY_NEG_INF = _sortable_key_const(NEG_INF)
_KEY_VALID = _sortable_key_const(NEG_INF * 0.5)


def _split3(a):
    hi = a.astype(BF16)
    r1 = a - hi.astype(F32)
    mid = r1.astype(BF16)
    lo = (r1 - mid.astype(F32)).astype(BF16)
    return hi, mid, lo


def _dot_exact_rhs(a, b_bf16):
    hi, mid, lo = _split3(a)
    d = functools.partial(jnp.dot, preferred_element_type=F32)
    return d(hi, b_bf16) + d(mid, b_bf16) + d(lo, b_bf16)


def _dot_exact_lhs(a_bf16, b):
    hi, mid, lo = _split3(b)
    d = functools.partial(jnp.dot, preferred_element_type=F32)
    return d(a_bf16, hi) + d(a_bf16, mid) + d(a_bf16, lo)


def _dot_nt(a, b):
    return lax.dot_general(a, b, (((1,), (1,)), ((), ())), preferred_element_type=F32)


def _sigmoid(x):
    return 1.0 / (1.0 + jnp.exp(-x))


def _in_proj_body(x_ref, pos_ref, g_ref, w_ref, rc_ref, bd_ref, gq_ref, gk_ref,
                  aq_ref, ak_ref, av_ref, iq_ref, ik_ref, small_ref, mqk_ref, mv_ref,
                  og_ref, ga_ref, gm_ref):
    x = x_ref[...]
    ms = jnp.mean(x * x, axis=-1, keepdims=True)
    h = (x * lax.rsqrt(ms + EPS) * g_ref[...]).astype(BF16)
    posf = pos_ref[...].astype(F32)

    def proj(lo, hi):
        return jnp.dot(h, w_ref[:, lo:hi], preferred_element_type=F32)

    def rope(t, cos, s_up, s_dn, half):
        n = t.shape[-1]
        return t * cos + pltpu.roll(t, half, 1) * s_up + pltpu.roll(t, n - half, 1) * s_dn

    def head_norm(t, gain):
        ssum = _dot_exact_rhs(t * t, bd_ref[...])
        return t * lax.rsqrt(ssum * (1.0 / ATT_HEAD_DIM) + EPS) * gain

    ang_a = posf * rc_ref[0:1, :]
    cos_a, sin_a = jnp.cos(ang_a), jnp.sin(ang_a)
    up_a, dn_a = sin_a * rc_ref[1:2, :], sin_a * rc_ref[2:3, :]
    ang_i = posf * rc_ref[3:4, :]
    cos_i, sin_i = jnp.cos(ang_i), jnp.sin(ang_i)
    up_i, dn_i = sin_i * rc_ref[4:5, :], sin_i * rc_ref[5:6, :]
    small_mask = rc_ref[6:7, :]
    cos_s = 1.0 + (cos_i - 1.0) * small_mask
    up_s, dn_s = up_i * small_mask, dn_i * small_mask

    att_half = ATT_HEAD_DIM // ROPE_FRACTION // 2
    idx_half = IDX_DIM // ROPE_FRACTION // 2
    q_scale = ATT_HEAD_DIM ** -0.5 * LOG2_E
    tm = x.shape[0]

    def store_qblock_t(ref, c, t):
        for r in range(tm // Q_BLOCK):
            ref[r, c * LANES:(c + 1) * LANES, :] = t[r * Q_BLOCK:(r + 1) * Q_BLOCK, :].T.astype(BF16)

    for c in range(ATT_Q_WIDTH // LANES):
        t = proj(COL_AQ + c * LANES, COL_AQ + (c + 1) * LANES)
        t = rope(head_norm(t, gq_ref[...]), cos_a, up_a, dn_a, att_half)
        store_qblock_t(aq_ref, c, t * q_scale)
    t = proj(COL_AK, COL_AK + LANES)
    ak_ref[...] = rope(head_norm(t, gk_ref[...]), cos_a, up_a, dn_a, att_half).astype(BF16)
    av_ref[0] = proj(COL_AV, COL_AV + LANES).T.astype(BF16)
    for c in range(IDX_HEADS * IDX_DIM // LANES):
        t = proj(COL_IQ + c * LANES, COL_IQ + (c + 1) * LANES)
        store_qblock_t(iq_ref, c, rope(t, cos_i, up_i, dn_i, idx_half))
    t = rope(proj(COL_SMALL, COL_SMALL + LANES), cos_s, up_s, dn_s, idx_half)
    small_ref[...] = t
    ik_ref[...] = t[:, SMALL_IK:SMALL_IK + IDX_DIM].astype(BF16)
    for c in range(2 * MLSTM_WIDTH // 512):
        mqk_ref[:, c * 512:(c + 1) * 512] = proj(COL_MQK + c * 512, COL_MQK + (c + 1) * 512)
    mv_ref[...] = proj(COL_MV, COL_MV + MLSTM_WIDTH).astype(BF16)
    og_ref[...] = _sigmoid(proj(COL_MO, COL_MO + MLSTM_WIDTH)).astype(BF16)
    for c in range(D_MODEL // 512):
        ga_ref[:, c * 512:(c + 1) * 512] = _sigmoid(
            proj(COL_GA + c * 512, COL_GA + (c + 1) * 512)).astype(BF16)
        gm_ref[:, c * 512:(c + 1) * 512] = _sigmoid(
            proj(COL_GM + c * 512, COL_GM + (c + 1) * 512)).astype(BF16)


def _rope_consts():
    lane = np.arange(LANES)

    def inv_freq(rot):
        half = rot // 2
        return jnp.float32(ROPE_THETA) ** (-jnp.arange(half, dtype=F32) * 2.0 / rot)

    rot_a = ATT_HEAD_DIM // ROPE_FRACTION
    rot_i = IDX_DIM // ROPE_FRACTION
    fa, fi = inv_freq(rot_a), inv_freq(rot_i)
    ja, ji = lane % ATT_HEAD_DIM, lane % IDX_DIM
    rc = jnp.zeros((16, LANES), F32)
    rc = rc.at[0].set(jnp.where(ja < rot_a, fa[ja % (rot_a // 2)], 0.0))
    rc = rc.at[1].set(jnp.asarray(((ja >= rot_a // 2) & (ja < rot_a)).astype(np.float32)))
    rc = rc.at[2].set(jnp.asarray(-(ja < rot_a // 2).astype(np.float32)))
    rc = rc.at[3].set(jnp.where(ji < rot_i, fi[ji % (rot_i // 2)], 0.0))
    rc = rc.at[4].set(jnp.asarray(((ji >= rot_i // 2) & (ji < rot_i)).astype(np.float32)))
    rc = rc.at[5].set(jnp.asarray(-(ji < rot_i // 2).astype(np.float32)))
    rc = rc.at[6].set(jnp.asarray((lane < IDX_DIM).astype(np.float32)))
    return rc


def _pack_w_in(w_in):
    sizes = (ATT_Q_WIDTH, ATT_KV_WIDTH, ATT_KV_WIDTH, IDX_HEADS * IDX_DIM, IDX_DIM, IDX_HEADS,
             MLSTM_WIDTH, MLSTM_WIDTH, MLSTM_WIDTH, MLSTM_WIDTH, MLSTM_HEADS, MLSTM_HEADS,
             D_MODEL, D_MODEL)
    pts = np.cumsum(sizes)[:-1].tolist()
    (aq, ak, av, iq, ik, iw, mq, mk, mv, mo, mi, mf, ga, gm) = jnp.split(w_in, pts, axis=-1)
    pad = jnp.zeros((D_MODEL, LANES - (SMALL_MF + MLSTM_HEADS)), w_in.dtype)
    small = jnp.concatenate([ik, iw, mi, mf, pad], axis=-1)
    return jnp.concatenate([aq, ak, av, iq, small, mq, mk, mv, mo, ga, gm], axis=-1).astype(BF16)


def _in_proj(x2, pos2, g_norm_mix, w_in, g_q_att, g_k_att):
    t_rows = x2.shape[0]
    tm = KEY_TILE
    w = _pack_w_in(w_in)
    lane = np.arange(LANES)
    bd = jnp.asarray((lane[:, None] // ATT_HEAD_DIM == lane[None, :] // ATT_HEAD_DIM)
                     .astype(np.float32)).astype(BF16)
    reps = LANES // ATT_HEAD_DIM
    gq = jnp.tile(g_q_att.astype(F32), reps)[None, :]
    gk = jnp.tile(g_k_att.astype(F32), reps)[None, :]

    def row(width):
        return pl.BlockSpec((tm, width), lambda i: (i, 0))

    def const(shape):
        return pl.BlockSpec(shape, lambda i: (0, 0))

    qb = tm // Q_BLOCK

    def qblock_t(width):
        return pl.BlockSpec((qb, width, Q_BLOCK), lambda i: (i, 0, 0))

    out_shapes = (
        jax.ShapeDtypeStruct((t_rows // Q_BLOCK, ATT_Q_WIDTH, Q_BLOCK), BF16),
        jax.ShapeDtypeStruct((t_rows, ATT_KV_WIDTH), BF16),
        jax.ShapeDtypeStruct((t_rows // tm, ATT_KV_WIDTH, tm), BF16),
        jax.ShapeDtypeStruct((t_rows // Q_BLOCK, IDX_HEADS * IDX_DIM, Q_BLOCK), BF16),
        jax.ShapeDtypeStruct((t_rows, IDX_DIM), BF16),
        jax.ShapeDtypeStruct((t_rows, LANES), F32),
        jax.ShapeDtypeStruct((t_rows, 2 * MLSTM_WIDTH), F32),
        jax.ShapeDtypeStruct((t_rows, MLSTM_WIDTH), BF16),
        jax.ShapeDtypeStruct((t_rows, MLSTM_WIDTH), BF16),
        jax.ShapeDtypeStruct((t_rows, D_MODEL), BF16),
        jax.ShapeDtypeStruct((t_rows, D_MODEL), BF16),
    )
    return pl.pallas_call(
        _in_proj_body,
        grid=(t_rows // tm,),
        in_specs=[row(D_MODEL), row(1), const((1, D_MODEL)), const((D_MODEL, COL_END)),
                  const((16, LANES)), const((LANES, LANES)), const((1, LANES)), const((1, LANES))],
        out_specs=[qblock_t(ATT_Q_WIDTH), row(ATT_KV_WIDTH),
                   pl.BlockSpec((1, ATT_KV_WIDTH, tm), lambda i: (i, 0, 0)),
                   qblock_t(IDX_HEADS * IDX_DIM)] + [row(s.shape[1]) for s in out_shapes[4:]],
        out_shape=out_shapes,
        compiler_params=pltpu.CompilerParams(
            dimension_semantics=("arbitrary",), vmem_limit_bytes=VMEM_LIMIT_BYTES),
        name="in_proj",
    )(x2, pos2, g_norm_mix.astype(F32)[None, :], w, _rope_consts(), bd, gq, gk)


def _bit_transpose32(words):
    a = list(words)
    j, m = 16, 0x0000FFFF
    while j:
        k = 0
        while k < WORD_BITS:
            t = (a[k] ^ lax.shift_right_logical(a[k + j], jnp.int32(j))) & m
            a[k] = a[k] ^ t
            a[k + j] = a[k + j] ^ lax.shift_left(t, jnp.int32(j))
            k = (k + j + 1) & ~j
        j >>= 1
        m = m ^ (m << j)
    return a


def _dsa_body(aq_ref, iq_ref, small_ref, k_ref, vt_ref, ik_ref, out_ref,
              key_scr, plane_scr, qi_scr, q2_scr, p_scr, acc_scr, *, topk, tk):
    blk = pl.program_id(1)
    nq = Q_BLOCK
    rep = ATT_HEADS // ATT_KV_HEADS
    groups_per_tile = tk // PLANE_GROUP
    n_tiles = lax.shift_right_logical(blk * nq + nq + tk - 1, int(np.log2(tk)))
    idx_scale = (IDX_DIM * IDX_HEADS) ** -0.5
    plane_rows = plane_scr.shape[1]

    @pl.when((pl.program_id(0) == 0) & (blk == 0))
    def _():
        plane_scr[...] = jnp.zeros_like(plane_scr)

    for h in range(IDX_HEADS):
        qi_scr[:, h * nq:(h + 1) * nq] = iq_ref[0, h * IDX_DIM:(h + 1) * IDX_DIM, :]
    w_rows = small_ref[...].T[SMALL_IW:SMALL_IW + IDX_HEADS, :] * idx_scale

    q_chunk = lax.shift_right_logical(
        blk * nq + lax.broadcasted_iota(I32, (1, nq), 1), int(np.log2(CHUNK)))
    key_row = lax.broadcasted_iota(I32, (tk, 1), 0)

    def score_tile(t, carry):
        start = pl.multiple_of(t * tk, tk)
        rel = jnp.dot(ik_ref[pl.ds(start, tk), :], qi_scr[...], preferred_element_type=F32)
        sc = jnp.maximum(rel[:, 0:nq], 0.0) * w_rows[0:1]
        for h in range(1, IDX_HEADS):
            sc = sc + jnp.maximum(rel[:, h * nq:(h + 1) * nq], 0.0) * w_rows[h:h + 1]
        visible = lax.shift_right_logical(start + key_row, int(np.log2(CHUNK))) <= q_chunk
        sc = jnp.where(visible, sc, NEG_INF)
        bits = lax.bitcast_convert_type(sc, I32)
        key = jnp.where(bits < 0, bits ^ 0x7FFFFFFF, bits)
        key_scr[pl.ds(start, tk), :] = key
        biased = key ^ _INT_MIN
        for g in range(groups_per_tile):
            words = [biased[g * PLANE_GROUP + i * SUBLANES:g * PLANE_GROUP + (i + 1) * SUBLANES, :]
                     for i in range(WORD_BITS)]
            planes = _bit_transpose32(words)
            row0 = pl.multiple_of((t * groups_per_tile + g) * SUBLANES, SUBLANES)
            for p in range(WORD_BITS):
                plane_scr[p, pl.ds(row0, SUBLANES), :] = planes[p]
        return carry

    lax.fori_loop(0, n_tiles, score_tile, 0)

    group_of_row = lax.shift_right_logical(
        lax.broadcasted_iota(I32, (plane_rows, nq), 0), int(np.log2(SUBLANES)))
    alive = jnp.where(group_of_row < n_tiles * groups_per_tile, -1, 0)
    k_rem = jnp.full((1, nq), float(topk), F32)
    thr_u = jnp.zeros((1, nq), I32)

    def lane_count(mask_words):
        pc = lax.population_count(mask_words)
        part = jnp.sum(pc.reshape(plane_rows // SUBLANES, SUBLANES, nq), axis=0)
        return jnp.sum(part.astype(F32), axis=0, keepdims=True)

    for p in range(WORD_BITS):
        bit_value = _INT_MIN if p == 0 else 1 << (WORD_BITS - 1 - p)
        ones = alive & plane_scr[p]
        c_ones = lane_count(ones)
        take = c_ones >= k_rem
        alive = jnp.where(take, ones, alive ^ ones)
        k_rem = jnp.where(take, k_rem, k_rem - c_ones)
        thr_u = jnp.where(take, thr_u | bit_value, thr_u)
    thr = thr_u ^ _INT_MIN
    quota = k_rem

    q2_scr[...] = jnp.zeros_like(q2_scr)
    for h in range(ATT_HEADS):
        g = h // rep
        q2_scr[g * ATT_HEAD_DIM:(g + 1) * ATT_HEAD_DIM, h * nq:(h + 1) * nq] = (
            aq_ref[0, h * ATT_HEAD_DIM:(h + 1) * ATT_HEAD_DIM, :])

    def attend(with_ties):
        acc_scr[...] = jnp.zeros_like(acc_scr)
        if with_ties:
            rr = lax.broadcasted_iota(I32, (tk, tk), 0)
            cc = lax.broadcasted_iota(I32, (tk, tk), 1)
            earlier = jnp.where(cc < rr, 1.0, 0.0).astype(BF16)

        def tile(t, carry):
            m_all, l_all, eq_seen = carry
            start = pl.multiple_of(t * tk, tk)
            key = key_scr[pl.ds(start, tk), :]
            valid = key > _KEY_VALID
            if with_ties:
                eq = key == thr
                eqf = jnp.where(eq, 1.0, 0.0)
                rank = jnp.dot(earlier, eqf.astype(BF16), preferred_element_type=F32) + eq_seen
                sel = valid & ((key > thr) | (eq & (rank < quota)))
                eq_seen = eq_seen + jnp.sum(eqf, axis=0, keepdims=True)
            else:
                sel = valid & (key >= thr)
            bias = jnp.where(sel, 0.0, NEG_INF)
            s = jnp.dot(k_ref[pl.ds(start, tk), :], q2_scr[...], preferred_element_type=F32)
            m_rows, l_rows, a_rows = [], [], []
            for h in range(ATT_HEADS):
                cols = slice(h * nq, (h + 1) * nq)
                sh = s[:, cols] + bias
                m_old = m_all[h:h + 1]
                m_new = jnp.maximum(m_old, jnp.max(sh, axis=0, keepdims=True))
                p = jnp.exp2(sh - m_new)
                alpha = jnp.exp2(m_old - m_new)
                l_rows.append(alpha * l_all[h:h + 1] + jnp.sum(p, axis=0, keepdims=True))
                p_scr[:, cols] = p.astype(BF16)
                m_rows.append(m_new)
                a_rows.append(alpha)
            acc_scr[...] = (acc_scr[...] * jnp.concatenate(a_rows, axis=1)
                            + jnp.dot(vt_ref[t], p_scr[...], preferred_element_type=F32))
            return jnp.concatenate(m_rows, axis=0), jnp.concatenate(l_rows, axis=0), eq_seen

        m_all, l_all, _ = lax.fori_loop(
            0, n_tiles, tile,
            (jnp.full((ATT_HEADS, nq), NEG_INF, F32), jnp.zeros((ATT_HEADS, nq), F32),
             jnp.zeros((1, nq), F32)))
        inv_l = 1.0 / l_all
        for j in range(ATT_HEADS // 2):
            g = (2 * j) // rep
            halves = [acc_scr[g * ATT_HEAD_DIM:(g + 1) * ATT_HEAD_DIM, h * nq:(h + 1) * nq]
                      * inv_l[h:h + 1] for h in (2 * j, 2 * j + 1)]
            out_ref[:, j * LANES:(j + 1) * LANES] = jnp.concatenate(halves, axis=0).T.astype(BF16)

    tie_lanes = jnp.where((lane_count(alive) > quota) & (thr > _KEY_NEG_INF), 1.0, 0.0)
    need_ties = jnp.max(tie_lanes) > 0.5

    @pl.when(need_ties)
    def _():
        attend(True)

    @pl.when(jnp.logical_not(need_ties))
    def _():
        attend(False)


def _dsa(aq_t, iq_t, small, ak, av_t, ik, bsz, seq):
    nb = seq // Q_BLOCK
    tk = KEY_TILE
    topk = min(TOPK_MAX, seq // 4)
    assert seq % tk == 0 and tk >= topk and tk & (tk - 1) == 0 and tk % PLANE_GROUP == 0
    nkt = seq // tk

    def qspec_t(width):
        return pl.BlockSpec((1, width, Q_BLOCK), lambda b, j: (b * nb + j, 0, 0))

    def qspec(width):
        return pl.BlockSpec((Q_BLOCK, width), lambda b, j: (b * nb + j, 0))

    def kspec(width):
        return pl.BlockSpec((seq, width), lambda b, j: (b, 0))

    cols = ATT_HEADS * Q_BLOCK
    return pl.pallas_call(
        functools.partial(_dsa_body, topk=topk, tk=tk),
        grid=(bsz, nb),
        in_specs=[qspec_t(ATT_Q_WIDTH), qspec_t(IDX_HEADS * IDX_DIM), qspec(LANES),
                  kspec(ATT_KV_WIDTH),
                  pl.BlockSpec((nkt, ATT_KV_WIDTH, tk), lambda b, j: (b, 0, 0)),
                  kspec(IDX_DIM)],
        out_specs=qspec(ATT_Q_WIDTH),
        out_shape=jax.ShapeDtypeStruct((bsz * seq, ATT_Q_WIDTH), BF16),
        scratch_shapes=[
            pltpu.VMEM((seq, Q_BLOCK), I32),
            pltpu.VMEM((WORD_BITS, seq // WORD_BITS, Q_BLOCK), I32),
            pltpu.VMEM((IDX_DIM, cols), BF16),
            pltpu.VMEM((ATT_KV_WIDTH, cols), BF16),
            pltpu.VMEM((tk, cols), BF16),
            pltpu.VMEM((ATT_KV_WIDTH, cols), F32),
        ],
        compiler_params=pltpu.CompilerParams(
            dimension_semantics=("arbitrary", "arbitrary"), vmem_limit_bytes=VMEM_LIMIT_BYTES),
        name="dsa_attention",
    )(aq_t, iq_t, small, ak, av_t, ik)


def _mlstm_body(mqk_ref, mv_ref, og_ref, small_ref, cw_ref, cb_ref, gb_ref, gn_ref, y_ref,
                ubuf, c_scr, n_scr, m_scr, *, chunk):
    ln = chunk
    d = MLSTM_HEAD_DIM
    halo = SUBLANES

    @pl.when(pl.program_id(1) == 0)
    def _():
        ubuf[0:halo, :] = jnp.zeros((halo, 2 * MLSTM_WIDTH), F32)
        c_scr[...] = jnp.zeros_like(c_scr)
        n_scr[...] = jnp.zeros_like(n_scr)
        m_scr[...] = jnp.zeros_like(m_scr)

    ubuf[halo:halo + ln, :] = mqk_ref[...]
    conv = cb_ref[...]
    for j in range(CONV_WIDTH):
        off = halo - (CONV_WIDTH - 1) + j
        conv = conv + cw_ref[j:j + 1, :] * ubuf[off:off + ln, :]
    ubuf[0:halo, :] = ubuf[ln:ln + halo, :]
    qk = conv * _sigmoid(conv)
    q_all = qk[:, :MLSTM_WIDTH]
    k_all = qk[:, MLSTM_WIDTH:] * (d ** -0.5)

    sm = small_ref[...] + gb_ref[...]
    lf = jnp.minimum(sm, 0.0) - jnp.log(1.0 + jnp.exp(-jnp.abs(sm)))
    sm_t = sm.T
    lf_t = lf.T
    rr = lax.broadcasted_iota(I32, (ln, ln), 0)
    cc = lax.broadcasted_iota(I32, (ln, ln), 1)
    causal = cc <= rr
    tri = jnp.where(causal, 1.0, 0.0).astype(BF16)
    tri_t = jnp.where(rr <= cc, 1.0, 0.0).astype(BF16)
    b_cols = _dot_exact_lhs(tri, lf)
    b_rows = _dot_exact_rhs(lf_t, tri_t)

    for h in range(MLSTM_HEADS):
        hs = slice(h * d, (h + 1) * d)
        ig_col = sm[:, SMALL_MI + h:SMALL_MI + h + 1]
        ig_row = sm_t[SMALL_MI + h:SMALL_MI + h + 1, :]
        b_col = b_cols[:, SMALL_MF + h:SMALL_MF + h + 1]
        b_row = b_rows[SMALL_MF + h:SMALL_MF + h + 1, :]
        m_st = m_scr[h:h + 1, 0:1]
        c_st = c_scr[h]
        n_st = n_scr[h:h + 1, :]

        dmat = jnp.where(causal, b_col - b_row + ig_row, NEG_INF)
        inter = b_col + m_st
        m_t = jnp.maximum(inter, jnp.max(dmat, axis=-1, keepdims=True))
        w_intra = jnp.exp(dmat - m_t)
        w_inter = jnp.exp(inter - m_t)
        qf = q_all[:, hs]
        kf = k_all[:, hs]
        qh = qf.astype(BF16)
        vh = mv_ref[:, hs]
        s = _dot_nt(qh, kf.astype(BF16)) * w_intra
        num = (w_inter * jnp.dot(qh, c_st.astype(BF16), preferred_element_type=F32)
               + jnp.dot(s.astype(BF16), vh, preferred_element_type=F32))
        den = (w_inter * jnp.sum(qf * n_st, axis=-1, keepdims=True)
               + jnp.sum(s, axis=-1, keepdims=True))
        hh = num / jnp.maximum(jnp.abs(den), jnp.exp(-m_t))

        b_last = b_col[ln - 1:ln, :]
        gdec = b_last - b_col + ig_col
        m_new = jnp.maximum(b_last + m_st, jnp.max(gdec, axis=0, keepdims=True))
        decay = jnp.exp(b_last + m_st - m_new)
        kw = kf * jnp.exp(gdec - m_new)
        c_scr[h] = decay * c_st + jnp.dot(kw.T.astype(BF16), vh, preferred_element_type=F32)
        n_scr[h:h + 1, :] = decay * n_st + jnp.sum(kw, axis=0, keepdims=True)
        m_scr[h:h + 1, :] = jnp.broadcast_to(m_new, (1, LANES))

        ms = jnp.mean(hh * hh, axis=-1, keepdims=True)
        yh = hh * lax.rsqrt(ms + EPS) * gn_ref[:, hs]
        y_ref[:, hs] = (og_ref[:, hs].astype(F32) * yh).astype(BF16)


def _mlstm(mqk, mv, og, small, w_conv, b_conv, b_igate, b_fgate, g_norm_out, bsz, seq):
    ln = min(MLSTM_CHUNK, seq)
    nc = seq // ln
    gbias = jnp.zeros((1, LANES), F32)
    gbias = gbias.at[0, SMALL_MI:SMALL_MI + MLSTM_HEADS].set(b_igate.astype(F32))
    gbias = gbias.at[0, SMALL_MF:SMALL_MF + MLSTM_HEADS].set(b_fgate.astype(F32))

    def row(width):
        return pl.BlockSpec((ln, width), lambda b, c: (b * nc + c, 0))

    def const(shape):
        return pl.BlockSpec(shape, lambda b, c: (0, 0))

    return pl.pallas_call(
        functools.partial(_mlstm_body, chunk=ln),
        grid=(bsz, nc),
        in_specs=[row(2 * MLSTM_WIDTH), row(MLSTM_WIDTH), row(MLSTM_WIDTH), row(LANES),
                  const((CONV_WIDTH, 2 * MLSTM_WIDTH)), const((1, 2 * MLSTM_WIDTH)),
                  const((1, LANES)), const((1, MLSTM_WIDTH))],
        out_specs=row(MLSTM_WIDTH),
        out_shape=jax.ShapeDtypeStruct((bsz * seq, MLSTM_WIDTH), BF16),
        scratch_shapes=[
            pltpu.VMEM((ln + SUBLANES, 2 * MLSTM_WIDTH), F32),
            pltpu.VMEM((MLSTM_HEADS, MLSTM_HEAD_DIM, MLSTM_HEAD_DIM), F32),
            pltpu.VMEM((SUBLANES, MLSTM_HEAD_DIM), F32),
            pltpu.VMEM((SUBLANES, LANES), F32),
        ],
        compiler_params=pltpu.CompilerParams(
            dimension_semantics=("arbitrary", "arbitrary"), vmem_limit_bytes=VMEM_LIMIT_BYTES),
        name="mlstm",
    )(mqk, mv, og, small, w_conv.astype(F32), b_conv.astype(F32)[None, :], gbias,
      g_norm_out.astype(F32)[None, :])


def _merge_body(x_ref, ya_ref, ym_ref, ga_ref, gm_ref, wpa_ref, wpm_ref, wo_ref, gmoe_ref,
                wrh_ref, wrl_ref, br_ref, x1_ref, xn_ref, ri_ref, rf_ref):
    pa = jnp.dot(ya_ref[...], wpa_ref[...], preferred_element_type=F32)
    pm = jnp.dot(ym_ref[...], wpm_ref[...], preferred_element_type=F32)
    merged = ga_ref[...].astype(F32) * pa + gm_ref[...].astype(F32) * pm
    x1 = x_ref[...] + jnp.dot(merged.astype(BF16), wo_ref[...], preferred_element_type=F32)
    x1_ref[...] = x1
    ms = jnp.mean(x1 * x1, axis=-1, keepdims=True)
    xn = x1 * lax.rsqrt(ms + EPS) * gmoe_ref[...]
    xn_ref[...] = xn

    xh = xn.astype(BF16)
    xl = (xn - xh.astype(F32)).astype(BF16)
    logits = (jnp.dot(xh, wrh_ref[...], preferred_element_type=F32)
              + jnp.dot(xh, wrl_ref[...], preferred_element_type=F32)
              + jnp.dot(xl, wrh_ref[...], preferred_element_type=F32)) + br_ref[...]
    lane = lax.broadcasted_iota(I32, logits.shape, 1).astype(F32)

    def first_argmax(vals, mask):
        v = jnp.where(mask, vals, -jnp.inf)
        top = jnp.max(v, axis=-1, keepdims=True)
        idx = jnp.min(jnp.where(mask & (v == top), lane, float(LANES)), axis=-1, keepdims=True)
        return top, idx

    is_group = lane < N_GROUPS
    g_max, g_sel = first_argmax(logits, is_group)
    g_den = jnp.sum(jnp.where(is_group, jnp.exp(logits - g_max), 0.0), axis=-1, keepdims=True)
    g_top = 1.0 / g_den
    lo = N_GROUPS + g_sel * EXPERTS_PER_GROUP
    in_group = (lane >= lo) & (lane < lo + EXPERTS_PER_GROUP)
    v1, i1 = first_argmax(logits, in_group)
    v2, i2 = first_argmax(logits, in_group & (lane != i1))
    e2 = jnp.exp(v2 - v1)
    p1 = 1.0 / (1.0 + e2)
    gate1 = p1 * g_top
    gate2 = (e2 * p1) * g_top
    ri_ref[...] = jnp.where(lane == 0, i1 - N_GROUPS,
                            jnp.where(lane == 1, i2 - N_GROUPS, 0.0)).astype(I32)
    rf_ref[...] = jnp.where(lane == 0, gate1, jnp.where(lane == 1, gate2, 0.0))


def _merge(x2, y_att, y_ml, ga, gm, w_proj_att, w_proj_mlstm, w_out, g_norm_moe,
           w_router_group, b_router_group, w_router_expert, b_router_expert):
    t_rows = x2.shape[0]
    tm = min(MERGE_ROWS, t_rows)
    n_r = N_GROUPS + N_EXPERTS
    wr = jnp.concatenate([w_router_group, w_router_expert,
                          jnp.zeros((D_MODEL, LANES - n_r), F32)], axis=-1).astype(F32)
    wr_hi = wr.astype(BF16)
    wr_lo = (wr - wr_hi.astype(F32)).astype(BF16)
    br =jnp.concatenate([b_router_group, b_router_expert,
                          jnp.zeros((LANES - n_r,), F32)]).astype(F32)[None, :]

    def row(width):
        return pl.BlockSpec((tm, width), lambda i: (i, 0))

    def const(shape):
        return pl.BlockSpec(shape, lambda i: (0, 0))

    out_shapes = (
        jax.ShapeDtypeStruct((t_rows, D_MODEL), F32),
        jax.ShapeDtypeStruct((t_rows, D_MODEL), F32),
        jax.ShapeDtypeStruct((t_rows, LANES), I32),
        jax.ShapeDtypeStruct((t_rows, LANES), F32),
    )
    return pl.pallas_call(
        _merge_body,
        grid=(t_rows // tm,),
        in_specs=[row(D_MODEL), row(ATT_Q_WIDTH), row(MLSTM_WIDTH), row(D_MODEL), row(D_MODEL),
                  const((ATT_Q_WIDTH, D_MODEL)), const((MLSTM_WIDTH, D_MODEL)),
                  const((D_MODEL, D_MODEL)), const((1, D_MODEL)), const((D_MODEL, LANES)),
                  const((D_MODEL, LANES)), const((1, LANES))],
        out_specs=[row(s.shape[1]) for s in out_shapes],
        out_shape=out_shapes,
        compiler_params=pltpu.CompilerParams(
            dimension_semantics=("arbitrary",), vmem_limit_bytes=VMEM_LIMIT_BYTES),
        name="merge_route",
    )(x2, y_att, y_ml, ga, gm, w_proj_att.astype(BF16), w_proj_mlstm.astype(BF16),
      w_out.astype(BF16), g_norm_moe.astype(F32)[None, :], wr_hi, wr_lo, br)


def _expert_body(be_ref, tok_cur_ref, tok_nxt_ref, xn_hbm, wg_ref, wu_ref, wd_ref,
                 ys_ref, xbuf, sem, *, rows):
    i = pl.program_id(0)
    nblk = pl.num_programs(0)
    slot = lax.rem(i, 2)
    nslot = 1 - slot

    def row_copy(tok, dst_slot, r):
        return pltpu.make_async_copy(
            xn_hbm.at[pl.ds(tok, 1), :], xbuf.at[dst_slot, pl.ds(r, 1), :], sem.at[dst_slot])

    def all_rows(dst_slot):
        return pltpu.make_async_copy(
            xn_hbm.at[pl.ds(0, rows), :], xbuf.at[dst_slot], sem.at[dst_slot])

    @pl.when(i == 0)
    def _():
        for r in range(rows):
            row_copy(tok_cur_ref[0, 0, r], 0, r).start(priority=r % DMA_PRIORITIES)

    all_rows(slot).wait()

    def prefetch(part):
        for r in range(part * rows // 4, (part + 1) * rows // 4):
            row_copy(tok_nxt_ref[0, 0, r], nslot, r).start(priority=r % DMA_PRIORITIES)

    prefetch(0)
    xb = xbuf[slot].astype(BF16)
    a = jnp.dot(xb, wg_ref[0], preferred_element_type=F32)
    prefetch(1)
    u = jnp.dot(xb, wu_ref[0], preferred_element_type=F32)
    prefetch(2)
    hid = (a * _sigmoid(a) * u).astype(BF16)
    prefetch(3)
    ys_ref[...] = jnp.dot(hid, wd_ref[0], preferred_element_type=F32)

    @pl.when(i == nblk - 1)
    def _():
        all_rows(nslot).wait()


def _experts(xn, buf_tok, block_e, w_gate, w_up, w_down):
    rows = EXPERT_ROWS
    cap = buf_tok.shape[0]
    nblk = cap // rows
    tok3 = buf_tok.reshape(nblk, 1, rows)
    grid_spec = pltpu.PrefetchScalarGridSpec(
        num_scalar_prefetch=1,
        grid=(nblk,),
        in_specs=[
            pl.BlockSpec((1, 1, rows), lambda i, be: (i, 0, 0), memory_space=pltpu.SMEM),
            pl.BlockSpec((1, 1, rows), lambda i, be: (jnp.minimum(i + 1, nblk - 1), 0, 0),
                         memory_space=pltpu.SMEM),
            pl.BlockSpec(memory_space=pl.ANY),
            pl.BlockSpec((1, D_MODEL, D_EXPERT), lambda i, be: (be[i], 0, 0)),
            pl.BlockSpec((1, D_MODEL, D_EXPERT), lambda i, be: (be[i], 0, 0)),
            pl.BlockSpec((1, D_EXPERT, D_MODEL), lambda i, be: (be[i], 0, 0)),
        ],
        out_specs=pl.BlockSpec((rows, D_MODEL), lambda i, be: (i, 0)),
        scratch_shapes=[pltpu.VMEM((2, rows, D_MODEL), F32), pltpu.SemaphoreType.DMA((2,))],
    )
    return pl.pallas_call(
        functools.partial(_expert_body, rows=rows),
        grid_spec=grid_spec,
        out_shape=jax.ShapeDtypeStruct((cap, D_MODEL), F32),
        compiler_params=pltpu.CompilerParams(
            dimension_semantics=("arbitrary",), vmem_limit_bytes=VMEM_LIMIT_BYTES),
        name="moe_experts",
    )(block_e, tok3, tok3, xn, w_gate.astype(BF16), w_up.astype(BF16), w_down.astype(BF16))


def _combine_body(dcur_ref, dnxt_ref, x1_ref, rf_ref, ys_hbm, out_ref, ybuf, sem, *, rows):
    i = pl.program_id(0)
    nt = pl.num_programs(0)
    slot = lax.rem(i, 2)
    nslot = 1 - slot

    def row_copy(src, dst_slot, k, r):
        return pltpu.make_async_copy(
            ys_hbm.at[pl.ds(src, 1), :], ybuf.at[dst_slot, k, pl.ds(r, 1), :], sem.at[dst_slot])

    def wait_all(dst_slot):
        for k in range(TOP_K_EXPERTS):
            pltpu.make_async_copy(
                ys_hbm.at[pl.ds(0, rows), :], ybuf.at[dst_slot, k], sem.at[dst_slot]).wait()

    def gather(dest_ref, dst_slot):
        for r in range(rows):
            for k in range(TOP_K_EXPERTS):
                row_copy(dest_ref[0, 0, TOP_K_EXPERTS * r + k], dst_slot, k, r).start(
                    priority=k % DMA_PRIORITIES)

    @pl.when(i == 0)
    def _():
        gather(dcur_ref, 0)

    wait_all(slot)
    gather(dnxt_ref, nslot)
    gates = rf_ref[...]
    out = x1_ref[...]
    for k in range(TOP_K_EXPERTS):
        out = out + gates[:, k:k + 1] * ybuf[slot, k]
    out_ref[...] = out

    @pl.when(i == nt - 1)
    def _():
        wait_all(nslot)


def _combine(x1, rf, ys, dest):
    t_rows = x1.shape[0]
    rows = min(COMBINE_ROWS, t_rows)
    nt = t_rows // rows
    dest3 = dest.reshape(nt, 1, rows * TOP_K_EXPERTS)
    dspec = functools.partial(pl.BlockSpec, (1, 1, rows * TOP_K_EXPERTS), memory_space=pltpu.SMEM)
    return pl.pallas_call(
        functools.partial(_combine_body, rows=rows),
        grid=(nt,),
        in_specs=[
            dspec(lambda i: (i, 0, 0)),
            dspec(lambda i: (jnp.minimum(i + 1, nt - 1), 0, 0)),
            pl.BlockSpec((rows, D_MODEL), lambda i: (i, 0)),
            pl.BlockSpec((rows, LANES), lambda i: (i, 0)),
            pl.BlockSpec(memory_space=pl.ANY),
        ],
        out_specs=pl.BlockSpec((rows, D_MODEL), lambda i: (i, 0)),
        out_shape=jax.ShapeDtypeStruct((t_rows, D_MODEL), F32),
        scratch_shapes=[pltpu.VMEM((2, TOP_K_EXPERTS, rows, D_MODEL), F32),
                        pltpu.SemaphoreType.DMA((2,))],
        compiler_params=pltpu.CompilerParams(
            dimension_semantics=("arbitrary",), vmem_limit_bytes=VMEM_LIMIT_BYTES),
        name="moe_combine",
    )(dest3, dest3, x1, rf, ys)


def _route_plan(expert_id):
    t_rows = expert_id.shape[0]
    m = t_rows * TOP_K_EXPERTS
    rows = EXPERT_ROWS
    e_flat = expert_id.reshape(m)
    onehot = (e_flat[:, None] == jnp.arange(N_EXPERTS, dtype=I32)[None, :]).astype(I32)
    running = jnp.cumsum(onehot, axis=0)
    counts = running[-1]
    rank = jnp.sum(onehot * running, axis=1) - 1
    padded = ((counts + rows - 1) // rows) * rows
    pends = jnp.cumsum(padded)
    pstarts = pends - padded
    cap = ((m + N_EXPERTS * rows + rows - 1) // rows) * rows
    nblk = cap // rows
    blk_start = jnp.arange(nblk, dtype=I32) * rows
    block_e = jnp.minimum(jnp.sum((pends[None, :] <= blk_start[:, None]).astype(I32), axis=1),
                          N_EXPERTS - 1)
    dest = (jnp.sum(onehot * pstarts[None, :], axis=1) + rank).astype(I32)
    order = jnp.argsort(e_flat, stable=True).astype(I32)
    starts = jnp.cumsum(counts) - counts
    slot = jnp.arange(cap, dtype=I32)
    slot_e = jnp.repeat(block_e, rows)
    off = slot - pstarts[slot_e]
    live = (off < counts[slot_e]) & (slot < pends[-1])
    src = jnp.clip(starts[slot_e] + off, 0, m - 1)
    buf_tok = jnp.where(live, order[src] // TOP_K_EXPERTS, 0).astype(I32)
    return buf_tok, block_e, dest


def kernel(x, positions, g_norm_mix, w_in, g_q_att, g_k_att, w_conv_mlstm, b_conv_mlstm, b_igate,
           b_fgate, g_norm_mlstm_out, w_proj_att, w_proj_mlstm, w_out, g_norm_moe, w_router_group,
           b_router_group, w_router_expert, b_router_expert, w_exp_gate, w_exp_up, w_exp_down):
    bsz, seq, dm = x.shape
    depth = g_norm_mix.shape[0]
    x2 = x.reshape(bsz * seq, dm)
    pos2 = positions.reshape(bsz * seq, 1).astype(I32)
    for l in range(depth):
        aq, ak, av, iq, ik, small, mqk, mv, og, ga, gm = _in_proj(
            x2, pos2, g_norm_mix[l], w_in[l], g_q_att[l], g_k_att[l])
        y_att = _dsa(aq, iq, small, ak, av, ik, bsz, seq)
        y_ml = _mlstm(mqk, mv, og, small, w_conv_mlstm[l], b_conv_mlstm[l], b_igate[l], b_fgate[l],
                      g_norm_mlstm_out[l], bsz, seq)
        x1, xn, ri, rf = _merge(x2, y_att, y_ml, ga, gm, w_proj_att[l], w_proj_mlstm[l], w_out[l],
                                g_norm_moe[l], w_router_group[l], b_router_group[l],
                                w_router_expert[l], b_router_expert[l])
        buf_tok, block_e, dest = _route_plan(ri[:, :TOP_K_EXPERTS])
        ys = _experts(xn, buf_tok, block_e, w_exp_gate[l], w_exp_up[l], w_exp_down[l])
        x2 = _combine(x1, rf, ys, dest)
    return x2.reshape(bsz, seq, dm)
```

```python
import functools

import numpy as np
import jax
import jax.numpy as jnp
from jax import lax
from jax.experimental import pallas as pl
from jax.experimental.pallas import tpu as pltpu

F32 = jnp.float32
BF16 = jnp.bfloat16
I32 = jnp.int32

D_MODEL = 1024
CHUNK = 64
Q_BLOCK = 128
EPS = 1e-6
NEG_INF = -1e30

ATT_HEADS = 8
ATT_KV_HEADS = 2
ATT_HEAD_DIM = 64
ATT_Q_WIDTH = ATT_HEADS * ATT_HEAD_DIM
ATT_KV_WIDTH = ATT_KV_HEADS * ATT_HEAD_DIM
IDX_HEADS = 8
IDX_DIM = 32
TOPK_MAX = 256
ROPE_THETA = 500000.0
ROPE_FRACTION = 4

MLSTM_HEADS = 4
MLSTM_HEAD_DIM = 128
MLSTM_WIDTH = MLSTM_HEADS * MLSTM_HEAD_DIM
CONV_WIDTH = 4

N_GROUPS = 4
EXPERTS_PER_GROUP = 8
N_EXPERTS = N_GROUPS * EXPERTS_PER_GROUP
TOP_K_EXPERTS = 2
D_EXPERT = 512

LANES = 128
SUBLANES = 8
VMEM_LIMIT_BYTES = 56 * 1024 * 1024

SMALL_IK = 0
SMALL_IW = IDX_DIM
SMALL_MI = SMALL_IW + IDX_HEADS
SMALL_MF = SMALL_MI + MLSTM_HEADS

COL_AQ = 0
COL_AK = COL_AQ + ATT_Q_WIDTH
COL_AV = COL_AK + ATT_KV_WIDTH
COL_IQ = COL_AV + ATT_KV_WIDTH
COL_SMALL = COL_IQ + IDX_HEADS * IDX_DIM
COL_MQK = COL_SMALL + LANES
COL_MV = COL_MQK + 2 * MLSTM_WIDTH
COL_MO = COL_MV + MLSTM_WIDTH
COL_GA = COL_MO + MLSTM_WIDTH
COL_GM = COL_GA + D_MODEL
COL_END = COL_GM + D_MODEL

KEY_TILE = 512
MLSTM_CHUNK = 256
MERGE_ROWS = 512
EXPERT_ROWS = 256
DISPATCH_ROWS = 512
COMBINE_ROWS = 256
WORD_BITS = 32
PLANE_GROUP = WORD_BITS * SUBLANES

DMA_PRIORITIES = 2

LOG2_E = 1.4426950408889634
_INT_MIN = -(2 ** 31)


def _sortable_key_const(v):
    b = int(np.array(v, np.float32).view(np.int32))
    return b if b >= 0 else b ^ 0x7FFFFFFF


_KEY_NEG_INF = _sortable_key_const(NEG_INF)
_KEY_VALID = _sortable_key_const(NEG_INF * 0.5)


def _split3(a):
    hi = a.astype(BF16)
    r1 = a - hi.astype(F32)
    mid = r1.astype(BF16)
    lo = (r1 - mid.astype(F32)).astype(BF16)
    return hi, mid, lo


def _dot_exact_rhs(a, b_bf16):
    hi, mid, lo = _split3(a)
    d = functools.partial(jnp.dot, preferred_element_type=F32)
    return d(hi, b_bf16) + d(mid, b_bf16) + d(lo, b_bf16)


def _dot_exact_lhs(a_bf16, b):
    hi, mid, lo = _split3(b)
    d = functools.partial(jnp.dot, preferred_element_type=F32)
    return d(a_bf16, hi) + d(a_bf16, mid) + d(a_bf16, lo)


def _dot_nt(a, b):
    return lax.dot_general(a, b, (((1,), (1,)), ((), ())), preferred_element_type=F32)


def _sigmoid(x):
    return 1.0 / (1.0 + jnp.exp(-x))


def _in_proj_body(x_ref, pos_ref, g_ref, w_ref, rc_ref, bd_ref, gq_ref, gk_ref,
                  aq_ref, ak_ref, av_ref, iq_ref, ik_ref, small_ref, mqk_ref, mv_ref,
                  og_ref, ga_ref, gm_ref):
    x = x_ref[...]
    ms = jnp.mean(x * x, axis=-1, keepdims=True)
    h = (x * lax.rsqrt(ms + EPS) * g_ref[...]).astype(BF16)
    posf = pos_ref[...].astype(F32)

    def proj(lo, hi):
        return jnp.dot(h, w_ref[:, lo:hi], preferred_element_type=F32)

    def rope(t, cos, s_up, s_dn, half):
        n = t.shape[-1]
        return t * cos + pltpu.roll(t, half, 1) * s_up + pltpu.roll(t, n - half, 1) * s_dn

    def head_norm(t, gain):
        ssum = _dot_exact_rhs(t * t, bd_ref[...])
        return t * lax.rsqrt(ssum * (1.0 / ATT_HEAD_DIM) + EPS) * gain

    ang_a = posf * rc_ref[0:1, :]
    cos_a, sin_a = jnp.cos(ang_a), jnp.sin(ang_a)
    up_a, dn_a = sin_a * rc_ref[1:2, :], sin_a * rc_ref[2:3, :]
    ang_i = posf * rc_ref[3:4, :]
    cos_i, sin_i = jnp.cos(ang_i), jnp.sin(ang_i)
    up_i, dn_i = sin_i * rc_ref[4:5, :], sin_i * rc_ref[5:6, :]
    small_mask = rc_ref[6:7, :]
    cos_s = 1.0 + (cos_i - 1.0) * small_mask
    up_s, dn_s = up_i * small_mask, dn_i * small_mask

    att_half = ATT_HEAD_DIM // ROPE_FRACTION // 2
    idx_half = IDX_DIM // ROPE_FRACTION // 2
    q_scale = ATT_HEAD_DIM ** -0.5 * LOG2_E
    tm = x.shape[0]

    def store_qblock_t(ref, c, t):
        for r in range(tm // Q_BLOCK):
            ref[r, c * LANES:(c + 1) * LANES, :] = t[r * Q_BLOCK:(r + 1) * Q_BLOCK, :].T.astype(BF16)

    for c in range(ATT_Q_WIDTH // LANES):
        t = proj(COL_AQ + c * LANES, COL_AQ + (c + 1) * LANES)
        t = rope(head_norm(t, gq_ref[...]), cos_a, up_a, dn_a, att_half)
        store_qblock_t(aq_ref, c, t * q_scale)
    t = proj(COL_AK, COL_AK + LANES)
    ak_ref[...] = rope(head_norm(t, gk_ref[...]), cos_a, up_a, dn_a, att_half).astype(BF16)
    av_ref[0] = proj(COL_AV, COL_AV + LANES).T.astype(BF16)
    for c in range(IDX_HEADS * IDX_DIM // LANES):
        t = proj(COL_IQ + c * LANES, COL_IQ + (c + 1) * LANES)
        store_qblock_t(iq_ref, c, rope(t, cos_i, up_i, dn_i, idx_half))
    t = rope(proj(COL_SMALL, COL_SMALL + LANES), cos_s, up_s, dn_s, idx_half)
    small_ref[...] = t
    ik_ref[...] = t[:, SMALL_IK:SMALL_IK + IDX_DIM].astype(BF16)
    for c in range(2 * MLSTM_WIDTH // 512):
        mqk_ref[:, c * 512:(c + 1) * 512] = proj(COL_MQK + c * 512, COL_MQK + (c + 1) * 512)
    mv_ref[...] = proj(COL_MV, COL_MV + MLSTM_WIDTH).astype(BF16)
    og_ref[...] = _sigmoid(proj(COL_MO, COL_MO + MLSTM_WIDTH)).astype(BF16)
    for c in range(D_MODEL // 512):
        ga_ref[:, c * 512:(c + 1) * 512] = _sigmoid(
            proj(COL_GA + c * 512, COL_GA + (c + 1) * 512)).astype(BF16)
        gm_ref[:, c * 512:(c + 1) * 512] = _sigmoid(
            proj(COL_GM + c * 512, COL_GM + (c + 1) * 512)).astype(BF16)


def _rope_consts():
    lane = np.arange(LANES)

    def inv_freq(rot):
        half = rot // 2
        return jnp.float32(ROPE_THETA) ** (-jnp.arange(half, dtype=F32) * 2.0 / rot)

    rot_a = ATT_HEAD_DIM // ROPE_FRACTION
    rot_i = IDX_DIM // ROPE_FRACTION
    fa, fi = inv_freq(rot_a), inv_freq(rot_i)
    ja, ji = lane % ATT_HEAD_DIM, lane % IDX_DIM
    rc = jnp.zeros((16, LANES), F32)
    rc = rc.at[0].set(jnp.where(ja < rot_a, fa[ja % (rot_a // 2)], 0.0))
    rc = rc.at[1].set(jnp.asarray(((ja >= rot_a // 2) & (ja < rot_a)).astype(np.float32)))
    rc = rc.at[2].set(jnp.asarray(-(ja < rot_a // 2).astype(np.float32)))
    rc = rc.at[3].set(jnp.where(ji < rot_i, fi[ji % (rot_i // 2)], 0.0))
    rc = rc.at[4].set(jnp.asarray(((ji >= rot_i // 2) & (ji < rot_i)).astype(np.float32)))
    rc = rc.at[5].set(jnp.asarray(-(ji < rot_i // 2).astype(np.float32)))
    rc = rc.at[6].set(jnp.asarray((lane < IDX_DIM).astype(np.float32)))
    return rc


def _pack_w_in(w_in):
    sizes = (ATT_Q_WIDTH, ATT_KV_WIDTH, ATT_KV_WIDTH, IDX_HEADS * IDX_DIM, IDX_DIM, IDX_HEADS,
             MLSTM_WIDTH, MLSTM_WIDTH, MLSTM_WIDTH, MLSTM_WIDTH, MLSTM_HEADS, MLSTM_HEADS,
             D_MODEL, D_MODEL)
    pts = np.cumsum(sizes)[:-1].tolist()
    (aq, ak, av, iq, ik, iw, mq, mk, mv, mo, mi, mf, ga, gm) = jnp.split(w_in, pts, axis=-1)
    pad = jnp.zeros((D_MODEL, LANES - (SMALL_MF + MLSTM_HEADS)), w_in.dtype)
    small = jnp.concatenate([ik, iw, mi, mf, pad], axis=-1)
    return jnp.concatenate([aq, ak, av, iq, small, mq, mk, mv, mo, ga, gm], axis=-1).astype(BF16)


def _in_proj(x2, pos2, g_norm_mix, w_in, g_q_att, g_k_att):
    t_rows = x2.shape[0]
    tm = KEY_TILE
    w = _pack_w_in(w_in)
    lane = np.arange(LANES)
    bd = jnp.asarray((lane[:, None] // ATT_HEAD_DIM == lane[None, :] // ATT_HEAD_DIM)
                     .astype(np.float32)).astype(BF16)
    reps = LANES // ATT_HEAD_DIM
    gq = jnp.tile(g_q_att.astype(F32), reps)[None, :]
    gk = jnp.tile(g_k_att.astype(F32), reps)[None, :]

    def row(width):
        return pl.BlockSpec((tm, width), lambda i: (i, 0))

    def const(shape):
        return pl.BlockSpec(shape, lambda i: (0, 0))

    qb = tm // Q_BLOCK

    def qblock_t(width):
        return pl.BlockSpec((qb, width, Q_BLOCK), lambda i: (i, 0, 0))

    out_shapes = (
        jax.ShapeDtypeStruct((t_rows // Q_BLOCK, ATT_Q_WIDTH, Q_BLOCK), BF16),
        jax.ShapeDtypeStruct((t_rows, ATT_KV_WIDTH), BF16),
        jax.ShapeDtypeStruct((t_rows // tm, ATT_KV_WIDTH, tm), BF16),
        jax.ShapeDtypeStruct((t_rows // Q_BLOCK, IDX_HEADS * IDX_DIM, Q_BLOCK), BF16),
        jax.ShapeDtypeStruct((t_rows, IDX_DIM), BF16),
        jax.ShapeDtypeStruct((t_rows, LANES), F32),
        jax.ShapeDtypeStruct((t_rows, 2 * MLSTM_WIDTH), F32),
        jax.ShapeDtypeStruct((t_rows, MLSTM_WIDTH), BF16),
        jax.ShapeDtypeStruct((t_rows, MLSTM_WIDTH), BF16),
        jax.ShapeDtypeStruct((t_rows, D_MODEL), BF16),
        jax.ShapeDtypeStruct((t_rows, D_MODEL), BF16),
    )
    return pl.pallas_call(
        _in_proj_body,
        grid=(t_rows // tm,),
        in_specs=[row(D_MODEL), row(1), const((1, D_MODEL)), const((D_MODEL, COL_END)),
                  const((16, LANES)), const((LANES, LANES)), const((1, LANES)), const((1, LANES))],
        out_specs=[qblock_t(ATT_Q_WIDTH), row(ATT_KV_WIDTH),
                   pl.BlockSpec((1, ATT_KV_WIDTH, tm), lambda i: (i, 0, 0)),
                   qblock_t(IDX_HEADS * IDX_DIM)] + [row(s.shape[1]) for s in out_shapes[4:]],
        out_shape=out_shapes,
        compiler_params=pltpu.CompilerParams(
            dimension_semantics=("arbitrary",), vmem_limit_bytes=VMEM_LIMIT_BYTES),
        name="in_proj",
    )(x2, pos2, g_norm_mix.astype(F32)[None, :], w, _rope_consts(), bd, gq, gk)


def _bit_transpose32(words):
    a = list(words)
    j, m = 16, 0x0000FFFF
    while j:
        k = 0
        while k < WORD_BITS:
            t = (a[k] ^ lax.shift_right_logical(a[k + j], jnp.int32(j))) & m
            a[k] = a[k] ^ t
            a[k + j] = a[k + j] ^ lax.shift_left(t, jnp.int32(j))
            k = (k + j + 1) & ~j
        j >>= 1
        m = m ^ (m << j)
    return a


def _dsa_body(aq_ref, iq_ref, small_ref, k_ref, vt_ref, ik_ref, out_ref,
              key_scr, plane_scr, qi_scr, q2_scr, p_scr, acc_scr, *, topk, tk):
    blk = pl.program_id(1)
    nq = Q_BLOCK
    rep = ATT_HEADS // ATT_KV_HEADS
    groups_per_tile = tk // PLANE_GROUP
    n_tiles = lax.shift_right_logical(blk * nq + nq + tk - 1, int(np.log2(tk)))
    idx_scale = (IDX_DIM * IDX_HEADS) ** -0.5
    plane_rows = plane_scr.shape[1]

    @pl.when((pl.program_id(0) == 0) & (blk == 0))
    def _():
        plane_scr[...] = jnp.zeros_like(plane_scr)

    for h in range(IDX_HEADS):
        qi_scr[:, h * nq:(h + 1) * nq] = iq_ref[0, h * IDX_DIM:(h + 1) * IDX_DIM, :]
    w_rows = small_ref[...].T[SMALL_IW:SMALL_IW + IDX_HEADS, :] * idx_scale

    q_chunk = lax.shift_right_logical(
        blk * nq + lax.broadcasted_iota(I32, (1, nq), 1), int(np.log2(CHUNK)))
    key_row = lax.broadcasted_iota(I32, (tk, 1), 0)

    def score_tile(t, carry):
        start = pl.multiple_of(t * tk, tk)
        rel = jnp.dot(ik_ref[pl.ds(start, tk), :], qi_scr[...], preferred_element_type=F32)
        sc = jnp.maximum(rel[:, 0:nq], 0.0) * w_rows[0:1]
        for h in range(1, IDX_HEADS):
            sc = sc + jnp.maximum(rel[:, h * nq:(h + 1) * nq], 0.0) * w_rows[h:h + 1]
        visible = lax.shift_right_logical(start + key_row, int(np.log2(CHUNK))) <= q_chunk
        sc = jnp.where(visible, sc, NEG_INF)
        bits = lax.bitcast_convert_type(sc, I32)
        key = jnp.where(bits < 0, bits ^ 0x7FFFFFFF, bits)
        key_scr[pl.ds(start, tk), :] = key
        biased = key ^ _INT_MIN
        for g in range(groups_per_tile):
            words = [biased[g * PLANE_GROUP + i * SUBLANES:g * PLANE_GROUP + (i + 1) * SUBLANES, :]
                     for i in range(WORD_BITS)]
            planes = _bit_transpose32(words)
            row0 = pl.multiple_of((t * groups_per_tile + g) * SUBLANES, SUBLANES)
            for p in range(WORD_BITS):
                plane_scr[p, pl.ds(row0, SUBLANES), :] = planes[p]
        return carry

    lax.fori_loop(0, n_tiles, score_tile, 0)

    group_of_row = lax.shift_right_logical(
        lax.broadcasted_iota(I32, (plane_rows, nq), 0), int(np.log2(SUBLANES)))
    alive = jnp.where(group_of_row < n_tiles * groups_per_tile, -1, 0)
    k_rem = jnp.full((1, nq), float(topk), F32)
    thr_u = jnp.zeros((1, nq), I32)

    def lane_count(mask_words):
        pc = lax.population_count(mask_words)
        part = jnp.sum(pc.reshape(plane_rows // SUBLANES, SUBLANES, nq), axis=0)
        return jnp.sum(part.astype(F32), axis=0, keepdims=True)

    for p in range(WORD_BITS):
        bit_value = _INT_MIN if p == 0 else 1 << (WORD_BITS - 1 - p)
        ones = alive & plane_scr[p]
        c_ones = lane_count(ones)
        take = c_ones >= k_rem
        alive = jnp.where(take, ones, alive ^ ones)
        k_rem = jnp.where(take, k_rem, k_rem - c_ones)
        thr_u = jnp.where(take, thr_u | bit_value, thr_u)
    thr = thr_u ^ _INT_MIN
    quota = k_rem

    q2_scr[...] = jnp.zeros_like(q2_scr)
    for h in range(ATT_HEADS):
        g = h // rep
        q2_scr[g * ATT_HEAD_DIM:(g + 1) * ATT_HEAD_DIM, h * nq:(h + 1) * nq] = (
            aq_ref[0, h * ATT_HEAD_DIM:(h + 1) * ATT_HEAD_DIM, :])

    def attend(with_ties):
        acc_scr[...] = jnp.zeros_like(acc_scr)
        if with_ties:
            rr = lax.broadcasted_iota(I32, (tk, tk), 0)
            cc = lax.broadcasted_iota(I32, (tk, tk), 1)
            earlier = jnp.where(cc < rr, 1.0, 0.0).astype(BF16)

        def tile(t, carry):
            m_all, l_all, eq_seen = carry
            start = pl.multiple_of(t * tk, tk)
            key = key_scr[pl.ds(start, tk), :]
            valid = key > _KEY_VALID
            if with_ties:
                eq = key == thr
                eqf = jnp.where(eq, 1.0, 0.0)
                rank = jnp.dot(earlier, eqf.astype(BF16), preferred_element_type=F32) + eq_seen
                sel = valid & ((key > thr) | (eq & (rank < quota)))
                eq_seen = eq_seen + jnp.sum(eqf, axis=0, keepdims=True)
            else:
                sel = valid & (key >= thr)
            bias = jnp.where(sel, 0.0, NEG_INF)
            s = jnp.dot(k_ref[pl.ds(start, tk), :], q2_scr[...], preferred_element_type=F32)
            m_rows, l_rows, a_rows = [], [], []
            for h in range(ATT_HEADS):
                cols = slice(h * nq, (h + 1) * nq)
                sh = s[:, cols] + bias
                m_old = m_all[h:h + 1]
                m_new = jnp.maximum(m_old, jnp.max(sh, axis=0, keepdims=True))
                p = jnp.exp2(sh - m_new)
                alpha = jnp.exp2(m_old - m_new)
                l_rows.append(alpha * l_all[h:h + 1] + jnp.sum(p, axis=0, keepdims=True))
                p_scr[:, cols] = p.astype(BF16)
                m_rows.append(m_new)
                a_rows.append(alpha)
            acc_scr[...] = (acc_scr[...] * jnp.concatenate(a_rows, axis=1)
                            + jnp.dot(vt_ref[t], p_scr[...], preferred_element_type=F32))
            return jnp.concatenate(m_rows, axis=0), jnp.concatenate(l_rows, axis=0), eq_seen

        m_all, l_all, _ = lax.fori_loop(
            0, n_tiles, tile,
            (jnp.full((ATT_HEADS, nq), NEG_INF, F32), jnp.zeros((ATT_HEADS, nq), F32),
             jnp.zeros((1, nq), F32)))
        inv_l = 1.0 / l_all
        for j in range(ATT_HEADS // 2):
            g = (2 * j) // rep
            halves = [acc_scr[g * ATT_HEAD_DIM:(g + 1) * ATT_HEAD_DIM, h * nq:(h + 1) * nq]
                      * inv_l[h:h + 1] for h in (2 * j, 2 * j + 1)]
            out_ref[:, j * LANES:(j + 1) * LANES] = jnp.concatenate(halves, axis=0).T.astype(BF16)

    tie_lanes = jnp.where((lane_count(alive) > quota) & (thr > _KEY_NEG_INF), 1.0, 0.0)
    need_ties = jnp.max(tie_lanes) > 0.5

    @pl.when(need_ties)
    def _():
        attend(True)

    @pl.when(jnp.logical_not(need_ties))
    def _():
        attend(False)


def _dsa(aq_t, iq_t, small, ak, av_t, ik, bsz, seq):
    nb = seq // Q_BLOCK
    tk = KEY_TILE
    topk = min(TOPK_MAX, seq // 4)
    assert seq % tk == 0 and tk >= topk and tk & (tk - 1) == 0 and tk % PLANE_GROUP == 0
    nkt = seq // tk

    def qspec_t(width):
        return pl.BlockSpec((1, width, Q_BLOCK), lambda b, j: (b * nb + j, 0, 0))

    def qspec(width):
        return pl.BlockSpec((Q_BLOCK, width), lambda b, j: (b * nb + j, 0))

    def kspec(width):
        return pl.BlockSpec((seq, width), lambda b, j: (b, 0))

    cols = ATT_HEADS * Q_BLOCK
    return pl.pallas_call(
        functools.partial(_dsa_body, topk=topk, tk=tk),
        grid=(bsz, nb),
        in_specs=[qspec_t(ATT_Q_WIDTH), qspec_t(IDX_HEADS * IDX_DIM), qspec(LANES),
                  kspec(ATT_KV_WIDTH),
                  pl.BlockSpec((nkt, ATT_KV_WIDTH, tk), lambda b, j: (b, 0, 0)),
                  kspec(IDX_DIM)],
        out_specs=qspec(ATT_Q_WIDTH),
        out_shape=jax.ShapeDtypeStruct((bsz * seq, ATT_Q_WIDTH), BF16),
        scratch_shapes=[
            pltpu.VMEM((seq, Q_BLOCK), I32),
            pltpu.VMEM((WORD_BITS, seq // WORD_BITS, Q_BLOCK), I32),
            pltpu.VMEM((IDX_DIM, cols), BF16),
            pltpu.VMEM((ATT_KV_WIDTH, cols), BF16),
            pltpu.VMEM((tk, cols), BF16),
            pltpu.VMEM((ATT_KV_WIDTH, cols), F32),
        ],
        compiler_params=pltpu.CompilerParams(
            dimension_semantics=("arbitrary", "arbitrary"), vmem_limit_bytes=VMEM_LIMIT_BYTES),
        name="dsa_attention",
    )(aq_t, iq_t, small, ak, av_t, ik)


def _mlstm_body(mqk_ref, mv_ref, og_ref, small_ref, cw_ref, cb_ref, gb_ref, gn_ref, y_ref,
                ubuf, c_scr, n_scr, m_scr, *, chunk):
    ln = chunk
    d = MLSTM_HEAD_DIM
    halo = SUBLANES

    @pl.when(pl.program_id(1) == 0)
    def _():
        ubuf[0:halo, :] = jnp.zeros((halo, 2 * MLSTM_WIDTH), F32)
        c_scr[...] = jnp.zeros_like(c_scr)
        n_scr[...] = jnp.zeros_like(n_scr)
        m_scr[...] = jnp.zeros_like(m_scr)

    ubuf[halo:halo + ln, :] = mqk_ref[...]
    conv = cb_ref[...]
    for j in range(CONV_WIDTH):
        off = halo - (CONV_WIDTH - 1) + j
        conv = conv + cw_ref[j:j + 1, :] * ubuf[off:off + ln, :]
    ubuf[0:halo, :] = ubuf[ln:ln + halo, :]
    qk = conv * _sigmoid(conv)
    q_all = qk[:, :MLSTM_WIDTH]
    k_all = qk[:, MLSTM_WIDTH:] * (d ** -0.5)

    sm = small_ref[...] + gb_ref[...]
    lf = jnp.minimum(sm, 0.0) - jnp.log(1.0 + jnp.exp(-jnp.abs(sm)))
    sm_t = sm.T
    lf_t = lf.T
    rr = lax.broadcasted_iota(I32, (ln, ln), 0)
    cc = lax.broadcasted_iota(I32, (ln, ln), 1)
    causal = cc <= rr
    tri = jnp.where(causal, 1.0, 0.0).astype(BF16)
    tri_t = jnp.where(rr <= cc, 1.0, 0.0).astype(BF16)
    b_cols = _dot_exact_lhs(tri, lf)
    b_rows = _dot_exact_rhs(lf_t, tri_t)

    for h in range(MLSTM_HEADS):
        hs = slice(h * d, (h + 1) * d)
        ig_col = sm[:, SMALL_MI + h:SMALL_MI + h + 1]
        ig_row = sm_t[SMALL_MI + h:SMALL_MI + h + 1, :]
        b_col = b_cols[:, SMALL_MF + h:SMALL_MF + h + 1]
        b_row = b_rows[SMALL_MF + h:SMALL_MF + h + 1, :]
        m_st = m_scr[h:h + 1, 0:1]
        c_st = c_scr[h]
        n_st = n_scr[h:h + 1, :]

        dmat = jnp.where(causal, b_col - b_row + ig_row, NEG_INF)
        inter = b_col + m_st
        m_t = jnp.maximum(inter, jnp.max(dmat, axis=-1, keepdims=True))
        w_intra = jnp.exp(dmat - m_t)
        w_inter = jnp.exp(inter - m_t)
        qf = q_all[:, hs]
        kf = k_all[:, hs]
        qh = qf.astype(BF16)
        vh = mv_ref[:, hs]
        s = _dot_nt(qh, kf.astype(BF16)) * w_intra
        num = (w_inter * jnp.dot(qh, c_st.astype(BF16), preferred_element_type=F32)
               + jnp.dot(s.astype(BF16), vh, preferred_element_type=F32))
        den = (w_inter * jnp.sum(qf * n_st, axis=-1, keepdims=True)
               + jnp.sum(s, axis=-1, keepdims=True))
        hh = num / jnp.maximum(jnp.abs(den), jnp.exp(-m_t))

        b_last = b_col[ln - 1:ln, :]
        gdec = b_last - b_col + ig_col
        m_new = jnp.maximum(b_last + m_st, jnp.max(gdec, axis=0, keepdims=True))
        decay = jnp.exp(b_last + m_st - m_new)
        kw = kf * jnp.exp(gdec - m_new)
        c_scr[h] = decay * c_st + jnp.dot(kw.T.astype(BF16), vh, preferred_element_type=F32)
        n_scr[h:h + 1, :] = decay * n_st + jnp.sum(kw, axis=0, keepdims=True)
        m_scr[h:h + 1, :] = jnp.broadcast_to(m_new, (1, LANES))

        ms = jnp.mean(hh * hh, axis=-1, keepdims=True)
        yh = hh * lax.rsqrt(ms + EPS) * gn_ref[:, hs]
        y_ref[:, hs] = (og_ref[:, hs].astype(F32) * yh).astype(BF16)


def _mlstm(mqk, mv, og, small, w_conv, b_conv, b_igate, b_fgate, g_norm_out, bsz, seq):
    ln = min(MLSTM_CHUNK, seq)
    nc = seq // ln
    gbias = jnp.zeros((1, LANES), F32)
    gbias = gbias.at[0, SMALL_MI:SMALL_MI + MLSTM_HEADS].set(b_igate.astype(F32))
    gbias = gbias.at[0, SMALL_MF:SMALL_MF + MLSTM_HEADS].set(b_fgate.astype(F32))

    def row(width):
        return pl.BlockSpec((ln, width), lambda b, c: (b * nc + c, 0))

    def const(shape):
        return pl.BlockSpec(shape, lambda b, c: (0, 0))

    return pl.pallas_call(
        functools.partial(_mlstm_body, chunk=ln),
        grid=(bsz, nc),
        in_specs=[row(2 * MLSTM_WIDTH), row(MLSTM_WIDTH), row(MLSTM_WIDTH), row(LANES),
                  const((CONV_WIDTH, 2 * MLSTM_WIDTH)), const((1, 2 * MLSTM_WIDTH)),
                  const((1, LANES)), const((1, MLSTM_WIDTH))],
        out_specs=row(MLSTM_WIDTH),
        out_shape=jax.ShapeDtypeStruct((bsz * seq, MLSTM_WIDTH), BF16),
        scratch_shapes=[
            pltpu.VMEM((ln + SUBLANES, 2 * MLSTM_WIDTH), F32),
            pltpu.VMEM((MLSTM_HEADS, MLSTM_HEAD_DIM, MLSTM_HEAD_DIM), F32),
            pltpu.VMEM((SUBLANES, MLSTM_HEAD_DIM), F32),
            pltpu.VMEM((SUBLANES, LANES), F32),
        ],
        compiler_params=pltpu.CompilerParams(
            dimension_semantics=("arbitrary", "arbitrary"), vmem_limit_bytes=VMEM_LIMIT_BYTES),
        name="mlstm",
    )(mqk, mv, og, small, w_conv.astype(F32), b_conv.astype(F32)[None, :], gbias,
      g_norm_out.astype(F32)[None, :])


def _merge_body(x_ref, ya_ref, ym_ref, ga_ref, gm_ref, wpa_ref, wpm_ref, wo_ref, gmoe_ref,
                wrh_ref, wrl_ref, br_ref, x1_ref, xn_ref, ri_ref, rf_ref):
    pa = jnp.dot(ya_ref[...], wpa_ref[...], preferred_element_type=F32)
    pm = jnp.dot(ym_ref[...], wpm_ref[...], preferred_element_type=F32)
    merged = ga_ref[...].astype(F32) * pa + gm_ref[...].astype(F32) * pm
    x1 = x_ref[...] + jnp.dot(merged.astype(BF16), wo_ref[...], preferred_element_type=F32)
    x1_ref[...] = x1
    ms = jnp.mean(x1 * x1, axis=-1, keepdims=True)
    xn = x1 * lax.rsqrt(ms + EPS) * gmoe_ref[...]
    xn_ref[...] = xn

    xh = xn.astype(BF16)
    xl = (xn - xh.astype(F32)).astype(BF16)
    logits = (jnp.dot(xh, wrh_ref[...], preferred_element_type=F32)
              + jnp.dot(xh, wrl_ref[...], preferred_element_type=F32)
              + jnp.dot(xl, wrh_ref[...], preferred_element_type=F32)) + br_ref[...]
    lane = lax.broadcasted_iota(I32, logits.shape, 1).astype(F32)

    def first_argmax(vals, mask):
        v = jnp.where(mask, vals, -jnp.inf)
        top = jnp.max(v, axis=-1, keepdims=True)
        idx = jnp.min(jnp.where(mask & (v == top), lane, float(LANES)), axis=-1, keepdims=True)
        return top, idx

    is_group = lane < N_GROUPS
    g_max, g_sel = first_argmax(logits, is_group)
    g_den = jnp.sum(jnp.where(is_group, jnp.exp(logits - g_max), 0.0), axis=-1, keepdims=True)
    g_top = 1.0 / g_den
    lo = N_GROUPS + g_sel * EXPERTS_PER_GROUP
    in_group = (lane >= lo) & (lane < lo + EXPERTS_PER_GROUP)
    v1, i1 = first_argmax(logits, in_group)
    v2, i2 = first_argmax(logits, in_group & (lane != i1))
    e2 = jnp.exp(v2 - v1)
    p1 = 1.0 / (1.0 + e2)
    gate1 = p1 * g_top
    gate2 = (e2 * p1) * g_top
    ri_ref[...] = jnp.where(lane == 0, i1 - N_GROUPS,
                            jnp.where(lane == 1, i2 - N_GROUPS, 0.0)).astype(I32)
    rf_ref[...] = jnp.where(lane == 0, gate1, jnp.where(lane == 1, gate2, 0.0))


def _merge(x2, y_att, y_ml, ga, gm, w_proj_att, w_proj_mlstm, w_out, g_norm_moe,
           w_router_group, b_router_group, w_router_expert, b_router_expert):
    t_rows = x2.shape[0]
    tm = min(MERGE_ROWS, t_rows)
    n_r = N_GROUPS + N_EXPERTS
    wr = jnp.concatenate([w_router_group, w_router_expert,
                          jnp.zeros((D_MODEL, LANES - n_r), F32)], axis=-1).astype(F32)
    wr_hi = wr.astype(BF16)
    wr_lo = (wr - wr_hi.astype(F32)).astype(BF16)
    br =jnp.concatenate([b_router_group, b_router_expert,
                          jnp.zeros((LANES - n_r,), F32)]).astype(F32)[None, :]

    def row(width):
        return pl.BlockSpec((tm, width), lambda i: (i, 0))

    def const(shape):
        return pl.BlockSpec(shape, lambda i: (0, 0))

    out_shapes = (
        jax.ShapeDtypeStruct((t_rows, D_MODEL), F32),
        jax.ShapeDtypeStruct((t_rows, D_MODEL), F32),
        jax.ShapeDtypeStruct((t_rows, LANES), I32),
        jax.ShapeDtypeStruct((t_rows, LANES), F32),
    )
    return pl.pallas_call(
        _merge_body,
        grid=(t_rows // tm,),
        in_specs=[row(D_MODEL), row(ATT_Q_WIDTH), row(MLSTM_WIDTH), row(D_MODEL), row(D_MODEL),
                  const((ATT_Q_WIDTH, D_MODEL)), const((MLSTM_WIDTH, D_MODEL)),
                  const((D_MODEL, D_MODEL)), const((1, D_MODEL)), const((D_MODEL, LANES)),
                  const((D_MODEL, LANES)), const((1, LANES))],
        out_specs=[row(s.shape[1]) for s in out_shapes],
        out_shape=out_shapes,
        compiler_params=pltpu.CompilerParams(
            dimension_semantics=("arbitrary",), vmem_limit_bytes=VMEM_LIMIT_BYTES),
        name="merge_route",
    )(x2, y_att, y_ml, ga, gm, w_proj_att.astype(BF16), w_proj_mlstm.astype(BF16),
      w_out.astype(BF16), g_norm_moe.astype(F32)[None, :], wr_hi, wr_lo, br)


def _dispatch_body(tok_cur_ref, tok_nxt_ref, xn_hbm, xs_ref, xbuf, sem, *, rows):
    i = pl.program_id(0)
    nblk = pl.num_programs(0)
    slot = lax.rem(i, 2)
    nslot = 1 - slot

    def row_copy(tok, dst_slot, r):
        return pltpu.make_async_copy(
            xn_hbm.at[pl.ds(tok, 1), :], xbuf.at[dst_slot, pl.ds(r, 1), :], sem.at[dst_slot])

    def all_rows(dst_slot):
        return pltpu.make_async_copy(
            xn_hbm.at[pl.ds(0, rows), :], xbuf.at[dst_slot], sem.at[dst_slot])

    def gather(tok_ref, dst_slot):
        for r in range(rows):
            row_copy(tok_ref[0, 0, r], dst_slot, r).start(priority=r % DMA_PRIORITIES)

    @pl.when(i == 0)
    def _():
        gather(tok_cur_ref, 0)

    all_rows(slot).wait()
    gather(tok_nxt_ref, nslot)
    xs_ref[...] = xbuf[slot].astype(BF16)

    @pl.when(i == nblk - 1)
    def _():
        all_rows(nslot).wait()


def _dispatch(xn, buf_tok):
    rows = DISPATCH_ROWS
    cap = buf_tok.shape[0]
    nblk = cap // rows
    tok3 = buf_tok.reshape(nblk, 1, rows)
    tspec = functools.partial(pl.BlockSpec, (1, 1, rows), memory_space=pltpu.SMEM)
    return pl.pallas_call(
        functools.partial(_dispatch_body, rows=rows),
        grid=(nblk,),
        in_specs=[tspec(lambda i: (i, 0, 0)),
                  tspec(lambda i: (jnp.minimum(i + 1, nblk - 1), 0, 0)),
                  pl.BlockSpec(memory_space=pl.ANY)],
        out_specs=pl.BlockSpec((rows, D_MODEL), lambda i: (i, 0)),
        out_shape=jax.ShapeDtypeStruct((cap, D_MODEL), BF16),
        scratch_shapes=[pltpu.VMEM((2, rows, D_MODEL), F32), pltpu.SemaphoreType.DMA((2,))],
        compiler_params=pltpu.CompilerParams(
            dimension_semantics=("arbitrary",), vmem_limit_bytes=VMEM_LIMIT_BYTES),
        name="moe_dispatch",
    )(tok3, tok3, xn)


def _expert_body(be_ref, xs_ref, wg_ref, wu_ref, wd_ref, ys_ref):
    xb = xs_ref[...]
    a = jnp.dot(xb, wg_ref[0], preferred_element_type=F32)
    u = jnp.dot(xb, wu_ref[0], preferred_element_type=F32)
    hid = (a * _sigmoid(a) * u).astype(BF16)
    ys_ref[...] = jnp.dot(hid, wd_ref[0], preferred_element_type=F32)


def _experts(xs, block_e, w_gate, w_up, w_down):
    rows = EXPERT_ROWS
    cap = xs.shape[0]
    nblk = cap // rows
    grid_spec = pltpu.PrefetchScalarGridSpec(
        num_scalar_prefetch=1,
        grid=(nblk,),
        in_specs=[
            pl.BlockSpec((rows, D_MODEL), lambda i, be: (i, 0)),
            pl.BlockSpec((1, D_MODEL, D_EXPERT), lambda i, be: (be[i], 0, 0)),
            pl.BlockSpec((1, D_MODEL, D_EXPERT), lambda i, be: (be[i], 0, 0)),
            pl.BlockSpec((1, D_EXPERT, D_MODEL), lambda i, be: (be[i], 0, 0)),
        ],
        out_specs=pl.BlockSpec((rows, D_MODEL), lambda i, be: (i, 0)),
    )
    return pl.pallas_call(
        _expert_body,
        grid_spec=grid_spec,
        out_shape=jax.ShapeDtypeStruct((cap, D_MODEL), F32),
        compiler_params=pltpu.CompilerParams(
            dimension_semantics=("arbitrary",), vmem_limit_bytes=VMEM_LIMIT_BYTES),
        name="moe_experts",
    )(block_e, xs, w_gate.astype(BF16), w_up.astype(BF16), w_down.astype(BF16))


def _combine_body(dcur_ref, dnxt_ref, x1_ref, rf_ref, ys_hbm, out_ref, ybuf, sem, *, rows):
    i = pl.program_id(0)
    nt = pl.num_programs(0)
    slot = lax.rem(i, 2)
    nslot = 1 - slot

    def row_copy(src, dst_slot, k, r):
        return pltpu.make_async_copy(
            ys_hbm.at[pl.ds(src, 1), :], ybuf.at[dst_slot, k, pl.ds(r, 1), :], sem.at[dst_slot])

    def wait_all(dst_slot):
        for k in range(TOP_K_EXPERTS):
            pltpu.make_async_copy(
                ys_hbm.at[pl.ds(0, rows), :], ybuf.at[dst_slot, k], sem.at[dst_slot]).wait()

    def gather(dest_ref, dst_slot):
        for r in range(rows):
            for k in range(TOP_K_EXPERTS):
                row_copy(dest_ref[0, 0, TOP_K_EXPERTS * r + k], dst_slot, k, r).start(
                    priority=k % DMA_PRIORITIES)

    @pl.when(i == 0)
    def _():
        gather(dcur_ref, 0)

    wait_all(slot)
    gather(dnxt_ref, nslot)
    gates = rf_ref[...]
    out = x1_ref[...]
    for k in range(TOP_K_EXPERTS):
        out = out + gates[:, k:k + 1] * ybuf[slot, k]
    out_ref[...] = out

    @pl.when(i == nt - 1)
    def _():
        wait_all(nslot)


def _combine(x1, rf, ys, dest):
    t_rows = x1.shape[0]
    rows = min(COMBINE_ROWS, t_rows)
    nt = t_rows // rows
    dest3 = dest.reshape(nt, 1, rows * TOP_K_EXPERTS)
    dspec = functools.partial(pl.BlockSpec, (1, 1, rows * TOP_K_EXPERTS), memory_space=pltpu.SMEM)
    return pl.pallas_call(
        functools.partial(_combine_body, rows=rows),
        grid=(nt,),
        in_specs=[
            dspec(lambda i: (i, 0, 0)),
            dspec(lambda i: (jnp.minimum(i + 1, nt - 1), 0, 0)),
            pl.BlockSpec((rows, D_MODEL), lambda i: (i, 0)),
            pl.BlockSpec((rows, LANES), lambda i: (i, 0)),
            pl.BlockSpec(memory_space=pl.ANY),
        ],
        out_specs=pl.BlockSpec((rows, D_MODEL), lambda i: (i, 0)),
        out_shape=jax.ShapeDtypeStruct((t_rows, D_MODEL), F32),
        scratch_shapes=[pltpu.VMEM((2, TOP_K_EXPERTS, rows, D_MODEL), F32),
                        pltpu.SemaphoreType.DMA((2,))],
        compiler_params=pltpu.CompilerParams(
            dimension_semantics=("arbitrary",), vmem_limit_bytes=VMEM_LIMIT_BYTES),
        name="moe_combine",
    )(dest3, dest3, x1, rf, ys)


def _route_plan(expert_id):
    t_rows = expert_id.shape[0]
    m = t_rows * TOP_K_EXPERTS
    rows = EXPERT_ROWS
    e_flat = expert_id.reshape(m)
    onehot = (e_flat[:, None] == jnp.arange(N_EXPERTS, dtype=I32)[None, :]).astype(I32)
    running = jnp.cumsum(onehot, axis=0)
    counts = running[-1]
    rank = jnp.sum(onehot * running, axis=1) - 1
    padded = ((counts + rows - 1) // rows) * rows
    pends = jnp.cumsum(padded)
    pstarts = pends - padded
    step = max(rows, DISPATCH_ROWS)
    cap = ((m + N_EXPERTS * rows + step - 1) // step) * step
    nblk = cap // rows
    blk_start = jnp.arange(nblk, dtype=I32) * rows
    block_e = jnp.minimum(jnp.sum((pends[None, :] <= blk_start[:, None]).astype(I32), axis=1),
                          N_EXPERTS - 1)
    dest = (jnp.sum(onehot * pstarts[None, :], axis=1) + rank).astype(I32)
    order = jnp.argsort(e_flat, stable=True).astype(I32)
    starts = jnp.cumsum(counts) - counts
    slot = jnp.arange(cap, dtype=I32)
    slot_e = jnp.repeat(block_e, rows)
    off = slot - pstarts[slot_e]
    live = (off < counts[slot_e]) & (slot < pends[-1])
    src = jnp.clip(starts[slot_e] + off, 0, m - 1)
    buf_tok = jnp.where(live, order[src] // TOP_K_EXPERTS, 0).astype(I32)
    return buf_tok, block_e, dest


def kernel(x, positions, g_norm_mix, w_in, g_q_att, g_k_att, w_conv_mlstm, b_conv_mlstm, b_igate,
           b_fgate, g_norm_mlstm_out, w_proj_att, w_proj_mlstm, w_out, g_norm_moe, w_router_group,
           b_router_group, w_router_expert, b_router_expert, w_exp_gate, w_exp_up, w_exp_down):
    bsz, seq, dm = x.shape
    depth = g_norm_mix.shape[0]
    x2 = x.reshape(bsz * seq, dm)
    pos2 = positions.reshape(bsz * seq, 1).astype(I32)
    for l in range(depth):
        aq, ak, av, iq, ik, small, mqk, mv, og, ga, gm = _in_proj(
            x2, pos2, g_norm_mix[l], w_in[l], g_q_att[l], g_k_att[l])
        y_att = _dsa(aq, iq, small, ak, av, ik, bsz, seq)
        y_ml = _mlstm(mqk, mv, og, small, w_conv_mlstm[l], b_conv_mlstm[l], b_igate[l], b_fgate[l],
                      g_norm_mlstm_out[l], bsz, seq)
        x1, xn, ri, rf = _merge(x2, y_att, y_ml, ga, gm, w_proj_att[l], w_proj_mlstm[l], w_out[l],
                                g_norm_moe[l], w_router_group[l], b_router_group[l],
                                w_router_expert[l], b_router_expert[l])
        buf_tok, block_e, dest = _route_plan(ri[:, :TOP_K_EXPERTS])
        xs = _dispatch(xn, buf_tok)
        ys = _experts(xs, block_e, w_exp_gate[l], w_exp_up[l], w_exp_down[l])
        x2 = _combine(x1, rf, ys, dest)
    return x2.reshape(bsz, seq, dm)
```

```python
import functools

import numpy as np
import jax
import jax.numpy as jnp
from jax import lax
from jax.experimental import pallas as pl
from jax.experimental.pallas import tpu as pltpu

F32 = jnp.float32
BF16 = jnp.bfloat16
I32 = jnp.int32

D_MODEL = 1024
CHUNK = 64
Q_BLOCK = 128
EPS = 1e-6
NEG_INF = -1e30

ATT_HEADS = 8
ATT_KV_HEADS = 2
ATT_HEAD_DIM = 64
ATT_Q_WIDTH = ATT_HEADS * ATT_HEAD_DIM
ATT_KV_WIDTH = ATT_KV_HEADS * ATT_HEAD_DIM
IDX_HEADS = 8
IDX_DIM = 32
TOPK_MAX = 256
ROPE_THETA = 500000.0
ROPE_FRACTION = 4

MLSTM_HEADS = 4
MLSTM_HEAD_DIM = 128
MLSTM_WIDTH = MLSTM_HEADS * MLSTM_HEAD_DIM
CONV_WIDTH = 4

N_GROUPS = 4
EXPERTS_PER_GROUP = 8
N_EXPERTS = N_GROUPS * EXPERTS_PER_GROUP
TOP_K_EXPERTS = 2
D_EXPERT = 512

LANES = 128
SUBLANES = 8
VMEM_LIMIT_BYTES = 56 * 1024 * 1024

SMALL_IK = 0
SMALL_IW = IDX_DIM
SMALL_MI = SMALL_IW + IDX_HEADS
SMALL_MF = SMALL_MI + MLSTM_HEADS

COL_AQ = 0
COL_AK = COL_AQ + ATT_Q_WIDTH
COL_AV = COL_AK + ATT_KV_WIDTH
COL_IQ = COL_AV + ATT_KV_WIDTH
COL_SMALL = COL_IQ + IDX_HEADS * IDX_DIM
COL_MQK = COL_SMALL + LANES
COL_MV = COL_MQK + 2 * MLSTM_WIDTH
COL_MO = COL_MV + MLSTM_WIDTH
COL_GA = COL_MO + MLSTM_WIDTH
COL_GM = COL_GA + D_MODEL
COL_END = COL_GM + D_MODEL

KEY_TILE = 512
MLSTM_CHUNK = 256
MERGE_ROWS = 512
EXPERT_ROWS = 256
COMBINE_ROWS = 256
WORD_BITS = 32
PLANE_GROUP = WORD_BITS * SUBLANES

DMA_PRIORITIES = 2

LOG2_E = 1.4426950408889634
_INT_MIN = -(2 ** 31)


def _sortable_key_const(v):
    b = int(np.array(v, np.float32).view(np.int32))
    return b if b >= 0 else b ^ 0x7FFFFFFF


_KEY_NEG_INF = _sortable_key_const(NEG_INF)
_KEY_VALID = _sortable_key_const(NEG_INF * 0.5)


def _split3(a):
    hi = a.astype(BF16)
    r1 = a - hi.astype(F32)
    mid = r1.astype(BF16)
    lo = (r1 - mid.astype(F32)).astype(BF16)
    return hi, mid, lo


def _dot_exact_rhs(a, b_bf16):
    hi, mid, lo = _split3(a)
    d = functools.partial(jnp.dot, preferred_element_type=F32)
    return d(hi, b_bf16) + d(mid, b_bf16) + d(lo, b_bf16)


def _dot_exact_lhs(a_bf16, b):
    hi, mid, lo = _split3(b)
    d = functools.partial(jnp.dot, preferred_element_type=F32)
    return d(a_bf16, hi) + d(a_bf16, mid) + d(a_bf16, lo)


def _dot_nt(a, b):
    return lax.dot_general(a, b, (((1,), (1,)), ((), ())), preferred_element_type=F32)


def _sigmoid(x):
    return 1.0 / (1.0 + jnp.exp(-x))


def _in_proj_body(x_ref, pos_ref, g_ref, w_ref, rc_ref, bd_ref, gq_ref, gk_ref,
                  aq_ref, ak_ref, av_ref, iq_ref, ik_ref, small_ref, mqk_ref, mv_ref,
                  og_ref, ga_ref, gm_ref):
    x = x_ref[...]
    ms = jnp.mean(x * x, axis=-1, keepdims=True)
    h = (x * lax.rsqrt(ms + EPS) * g_ref[...]).astype(BF16)
    posf = pos_ref[...].astype(F32)

    def proj(lo, hi):
        return jnp.dot(h, w_ref[:, lo:hi], preferred_element_type=F32)

    def rope(t, cos, s_up, s_dn, half):
        n = t.shape[-1]
        return t * cos + pltpu.roll(t, half, 1) * s_up + pltpu.roll(t, n - half, 1) * s_dn

    def head_norm(t, gain):
        ssum = _dot_exact_rhs(t * t, bd_ref[...])
        return t * lax.rsqrt(ssum * (1.0 / ATT_HEAD_DIM) + EPS) * gain

    ang_a = posf * rc_ref[0:1, :]
    cos_a, sin_a = jnp.cos(ang_a), jnp.sin(ang_a)
    up_a, dn_a = sin_a * rc_ref[1:2, :], sin_a * rc_ref[2:3, :]
    ang_i = posf * rc_ref[3:4, :]
    cos_i, sin_i = jnp.cos(ang_i), jnp.sin(ang_i)
    up_i, dn_i = sin_i * rc_ref[4:5, :], sin_i * rc_ref[5:6, :]
    small_mask = rc_ref[6:7, :]
    cos_s = 1.0 + (cos_i - 1.0) * small_mask
    up_s, dn_s = up_i * small_mask, dn_i * small_mask

    att_half = ATT_HEAD_DIM // ROPE_FRACTION // 2
    idx_half = IDX_DIM // ROPE_FRACTION // 2
    q_scale = ATT_HEAD_DIM ** -0.5 * LOG2_E
    tm = x.shape[0]

    def store_qblock_t(ref, c, t):
        for r in range(tm // Q_BLOCK):
            ref[r, c * LANES:(c + 1) * LANES, :] = t[r * Q_BLOCK:(r + 1) * Q_BLOCK, :].T.astype(BF16)

    for c in range(ATT_Q_WIDTH // LANES):
        t = proj(COL_AQ + c * LANES, COL_AQ + (c + 1) * LANES)
        t = rope(head_norm(t, gq_ref[...]), cos_a, up_a, dn_a, att_half)
        store_qblock_t(aq_ref, c, t * q_scale)
    t = proj(COL_AK, COL_AK + LANES)
    ak_ref[...] = rope(head_norm(t, gk_ref[...]), cos_a, up_a, dn_a, att_half).astype(BF16)
    av_ref[0] = proj(COL_AV, COL_AV + LANES).T.astype(BF16)
    for c in range(IDX_HEADS * IDX_DIM // LANES):
        t = proj(COL_IQ + c * LANES, COL_IQ + (c + 1) * LANES)
        store_qblock_t(iq_ref, c, rope(t, cos_i, up_i, dn_i, idx_half))
    t = rope(proj(COL_SMALL, COL_SMALL + LANES), cos_s, up_s, dn_s, idx_half)
    small_ref[...] = t
    ik_ref[...] = t[:, SMALL_IK:SMALL_IK + IDX_DIM].astype(BF16)
    for c in range(2 * MLSTM_WIDTH // 512):
        mqk_ref[:, c * 512:(c + 1) * 512] = proj(COL_MQK + c * 512, COL_MQK + (c + 1) * 512)
    mv_ref[...] = proj(COL_MV, COL_MV + MLSTM_WIDTH).astype(BF16)
    og_ref[...] = _sigmoid(proj(COL_MO, COL_MO + MLSTM_WIDTH)).astype(BF16)
    for c in range(D_MODEL // 512):
        ga_ref[:, c * 512:(c + 1) * 512] = _sigmoid(
            proj(COL_GA + c * 512, COL_GA + (c + 1) * 512)).astype(BF16)
        gm_ref[:, c * 512:(c + 1) * 512] = _sigmoid(
            proj(COL_GM + c * 512, COL_GM + (c + 1) * 512)).astype(BF16)


def _rope_consts():
    lane = np.arange(LANES)

    def inv_freq(rot):
        half = rot // 2
        return jnp.float32(ROPE_THETA) ** (-jnp.arange(half, dtype=F32) * 2.0 / rot)

    rot_a = ATT_HEAD_DIM // ROPE_FRACTION
    rot_i = IDX_DIM // ROPE_FRACTION
    fa, fi = inv_freq(rot_a), inv_freq(rot_i)
    ja, ji = lane % ATT_HEAD_DIM, lane % IDX_DIM
    rc = jnp.zeros((16, LANES), F32)
    rc = rc.at[0].set(jnp.where(ja < rot_a, fa[ja % (rot_a // 2)], 0.0))
    rc = rc.at[1].set(jnp.asarray(((ja >= rot_a // 2) & (ja < rot_a)).astype(np.float32)))
    rc = rc.at[2].set(jnp.asarray(-(ja < rot_a // 2).astype(np.float32)))
    rc = rc.at[3].set(jnp.where(ji < rot_i, fi[ji % (rot_i // 2)], 0.0))
    rc = rc.at[4].set(jnp.asarray(((ji >= rot_i // 2) & (ji < rot_i)).astype(np.float32)))
    rc = rc.at[5].set(jnp.asarray(-(ji < rot_i // 2).astype(np.float32)))
    rc = rc.at[6].set(jnp.asarray((lane < IDX_DIM).astype(np.float32)))
    return rc


def _pack_w_in(w_in):
    sizes = (ATT_Q_WIDTH, ATT_KV_WIDTH, ATT_KV_WIDTH, IDX_HEADS * IDX_DIM, IDX_DIM, IDX_HEADS,
             MLSTM_WIDTH, MLSTM_WIDTH, MLSTM_WIDTH, MLSTM_WIDTH, MLSTM_HEADS, MLSTM_HEADS,
             D_MODEL, D_MODEL)
    pts = np.cumsum(sizes)[:-1].tolist()
    (aq, ak, av, iq, ik, iw, mq, mk, mv, mo, mi, mf, ga, gm) = jnp.split(w_in, pts, axis=-1)
    pad = jnp.zeros((D_MODEL, LANES - (SMALL_MF + MLSTM_HEADS)), w_in.dtype)
    small = jnp.concatenate([ik, iw, mi, mf, pad], axis=-1)
    return jnp.concatenate([aq, ak, av, iq, small, mq, mk, mv, mo, ga, gm], axis=-1).astype(BF16)


def _in_proj(x2, pos2, g_norm_mix, w_in, g_q_att, g_k_att):
    t_rows = x2.shape[0]
    tm = KEY_TILE
    w = _pack_w_in(w_in)
    lane = np.arange(LANES)
    bd = jnp.asarray((lane[:, None] // ATT_HEAD_DIM == lane[None, :] // ATT_HEAD_DIM)
                     .astype(np.float32)).astype(BF16)
    reps = LANES // ATT_HEAD_DIM
    gq = jnp.tile(g_q_att.astype(F32), reps)[None, :]
    gk = jnp.tile(g_k_att.astype(F32), reps)[None, :]

    def row(width):
        return pl.BlockSpec((tm, width), lambda i: (i, 0))

    def const(shape):
        return pl.BlockSpec(shape, lambda i: (0, 0))

    qb = tm // Q_BLOCK

    def qblock_t(width):
        return pl.BlockSpec((qb, width, Q_BLOCK), lambda i: (i, 0, 0))

    out_shapes = (
        jax.ShapeDtypeStruct((t_rows // Q_BLOCK, ATT_Q_WIDTH, Q_BLOCK), BF16),
        jax.ShapeDtypeStruct((t_rows, ATT_KV_WIDTH), BF16),
        jax.ShapeDtypeStruct((t_rows // tm, ATT_KV_WIDTH, tm), BF16),
        jax.ShapeDtypeStruct((t_rows // Q_BLOCK, IDX_HEADS * IDX_DIM, Q_BLOCK), BF16),
        jax.ShapeDtypeStruct((t_rows, IDX_DIM), BF16),
        jax.ShapeDtypeStruct((t_rows, LANES), F32),
        jax.ShapeDtypeStruct((t_rows, 2 * MLSTM_WIDTH), F32),
        jax.ShapeDtypeStruct((t_rows, MLSTM_WIDTH), BF16),
        jax.ShapeDtypeStruct((t_rows, MLSTM_WIDTH), BF16),
        jax.ShapeDtypeStruct((t_rows, D_MODEL), BF16),
        jax.ShapeDtypeStruct((t_rows, D_MODEL), BF16),
    )
    return pl.pallas_call(
        _in_proj_body,
        grid=(t_rows // tm,),
        in_specs=[row(D_MODEL), row(1), const((1, D_MODEL)), const((D_MODEL, COL_END)),
                  const((16, LANES)), const((LANES, LANES)), const((1, LANES)), const((1, LANES))],
        out_specs=[qblock_t(ATT_Q_WIDTH), row(ATT_KV_WIDTH),
                   pl.BlockSpec((1, ATT_KV_WIDTH, tm), lambda i: (i, 0, 0)),
                   qblock_t(IDX_HEADS * IDX_DIM)] + [row(s.shape[1]) for s in out_shapes[4:]],
        out_shape=out_shapes,
        compiler_params=pltpu.CompilerParams(
            dimension_semantics=("arbitrary",), vmem_limit_bytes=VMEM_LIMIT_BYTES),
        name="in_proj",
    )(x2, pos2, g_norm_mix.astype(F32)[None, :], w, _rope_consts(), bd, gq, gk)


def _bit_transpose32(words):
    a = list(words)
    j, m = 16, 0x0000FFFF
    while j:
        k = 0
        while k < WORD_BITS:
            t = (a[k] ^ lax.shift_right_logical(a[k + j], jnp.int32(j))) & m
            a[k] = a[k] ^ t
            a[k + j] = a[k + j] ^ lax.shift_left(t, jnp.int32(j))
            k = (k + j + 1) & ~j
        j >>= 1
        m = m ^ (m << j)
    return a


def _dsa_body(aq_ref, iq_ref, small_ref, k_ref, vt_ref, ik_ref, out_ref,
              key_scr, plane_scr, qi_scr, q2_scr, p_scr, acc_scr, *, topk, tk):
    blk = pl.program_id(1)
    nq = Q_BLOCK
    rep = ATT_HEADS // ATT_KV_HEADS
    groups_per_tile = tk // PLANE_GROUP
    n_tiles = lax.shift_right_logical(blk * nq + nq + tk - 1, int(np.log2(tk)))
    idx_scale = (IDX_DIM * IDX_HEADS) ** -0.5
    plane_rows = plane_scr.shape[1]

    @pl.when((pl.program_id(0) == 0) & (blk == 0))
    def _():
        plane_scr[...] = jnp.zeros_like(plane_scr)

    for h in range(IDX_HEADS):
        qi_scr[:, h * nq:(h + 1) * nq] = iq_ref[0, h * IDX_DIM:(h + 1) * IDX_DIM, :]
    w_rows = small_ref[...].T[SMALL_IW:SMALL_IW + IDX_HEADS, :] * idx_scale

    q_chunk = lax.shift_right_logical(
        blk * nq + lax.broadcasted_iota(I32, (1, nq), 1), int(np.log2(CHUNK)))
    key_row = lax.broadcasted_iota(I32, (tk, 1), 0)

    def score_tile(t, carry):
        start = pl.multiple_of(t * tk, tk)
        rel = jnp.dot(ik_ref[pl.ds(start, tk), :], qi_scr[...], preferred_element_type=F32)
        sc = jnp.maximum(rel[:, 0:nq], 0.0) * w_rows[0:1]
        for h in range(1, IDX_HEADS):
            sc = sc + jnp.maximum(rel[:, h * nq:(h + 1) * nq], 0.0) * w_rows[h:h + 1]
        visible = lax.shift_right_logical(start + key_row, int(np.log2(CHUNK))) <= q_chunk
        sc = jnp.where(visible, sc, NEG_INF)
        bits = lax.bitcast_convert_type(sc, I32)
        key = jnp.where(bits < 0, bits ^ 0x7FFFFFFF, bits)
        key_scr[pl.ds(start, tk), :] = key
        biased = key ^ _INT_MIN
        for g in range(groups_per_tile):
            words = [biased[g * PLANE_GROUP + i * SUBLANES:g * PLANE_GROUP + (i + 1) * SUBLANES, :]
                     for i in range(WORD_BITS)]
            planes = _bit_transpose32(words)
            row0 = pl.multiple_of((t * groups_per_tile + g) * SUBLANES, SUBLANES)
            for p in range(WORD_BITS):
                plane_scr[p, pl.ds(row0, SUBLANES), :] = planes[p]
        return carry

    lax.fori_loop(0, n_tiles, score_tile, 0)

    group_of_row = lax.shift_right_logical(
        lax.broadcasted_iota(I32, (plane_rows, nq), 0), int(np.log2(SUBLANES)))
    alive = jnp.where(group_of_row < n_tiles * groups_per_tile, -1, 0)
    k_rem = jnp.full((1, nq), float(topk), F32)
    thr_u = jnp.zeros((1, nq), I32)

    def lane_count(mask_words):
        pc = lax.population_count(mask_words)
        part = jnp.sum(pc.reshape(plane_rows // SUBLANES, SUBLANES, nq), axis=0)
        return jnp.sum(part.astype(F32), axis=0, keepdims=True)

    for p in range(WORD_BITS):
        bit_value = _INT_MIN if p == 0 else 1 << (WORD_BITS - 1 - p)
        ones = alive & plane_scr[p]
        c_ones = lane_count(ones)
        take = c_ones >= k_rem
        alive = jnp.where(take, ones, alive ^ ones)
        k_rem = jnp.where(take, k_rem, k_rem - c_ones)
        thr_u = jnp.where(take, thr_u | bit_value, thr_u)
    thr = thr_u ^ _INT_MIN
    quota = k_rem

    q2_scr[...] = jnp.zeros_like(q2_scr)
    for h in range(ATT_HEADS):
        g = h // rep
        q2_scr[g * ATT_HEAD_DIM:(g + 1) * ATT_HEAD_DIM, h * nq:(h + 1) * nq] = (
            aq_ref[0, h * ATT_HEAD_DIM:(h + 1) * ATT_HEAD_DIM, :])

    def attend(with_ties):
        acc_scr[...] = jnp.zeros_like(acc_scr)
        if with_ties:
            rr = lax.broadcasted_iota(I32, (tk, tk), 0)
            cc = lax.broadcasted_iota(I32, (tk, tk), 1)
            earlier = jnp.where(cc < rr, 1.0, 0.0).astype(BF16)

        def tile(t, carry):
            m_all, l_all, eq_seen = carry
            start = pl.multiple_of(t * tk, tk)
            key = key_scr[pl.ds(start, tk), :]
            valid = key > _KEY_VALID
            if with_ties:
                eq = key == thr
                eqf = jnp.where(eq, 1.0, 0.0)
                rank = jnp.dot(earlier, eqf.astype(BF16), preferred_element_type=F32) + eq_seen
                sel = valid & ((key > thr) | (eq & (rank < quota)))
                eq_seen = eq_seen + jnp.sum(eqf, axis=0, keepdims=True)
            else:
                sel = valid & (key >= thr)
            bias = jnp.where(sel, 0.0, NEG_INF)
            s = jnp.dot(k_ref[pl.ds(start, tk), :], q2_scr[...], preferred_element_type=F32)
            m_rows, l_rows, a_rows = [], [], []
            for h in range(ATT_HEADS):
                cols = slice(h * nq, (h + 1) * nq)
                sh = s[:, cols] + bias
                m_old = m_all[h:h + 1]
                m_new = jnp.maximum(m_old, jnp.max(sh, axis=0, keepdims=True))
                p = jnp.exp2(sh - m_new)
                alpha = jnp.exp2(m_old - m_new)
                l_rows.append(alpha * l_all[h:h + 1] + jnp.sum(p, axis=0, keepdims=True))
                p_scr[:, cols] = p.astype(BF16)
                m_rows.append(m_new)
                a_rows.append(alpha)
            acc_scr[...] = (acc_scr[...] * jnp.concatenate(a_rows, axis=1)
                            + jnp.dot(vt_ref[t], p_scr[...], preferred_element_type=F32))
            return jnp.concatenate(m_rows, axis=0), jnp.concatenate(l_rows, axis=0), eq_seen

        m_all, l_all, _ = lax.fori_loop(
            0, n_tiles, tile,
            (jnp.full((ATT_HEADS, nq), NEG_INF, F32), jnp.zeros((ATT_HEADS, nq), F32),
             jnp.zeros((1, nq), F32)))
        inv_l = 1.0 / l_all
        for j in range(ATT_HEADS // 2):
            g = (2 * j) // rep
            halves = [acc_scr[g * ATT_HEAD_DIM:(g + 1) * ATT_HEAD_DIM, h * nq:(h + 1) * nq]
                      * inv_l[h:h + 1] for h in (2 * j, 2 * j + 1)]
            out_ref[:, j * LANES:(j + 1) * LANES] = jnp.concatenate(halves, axis=0).T.astype(BF16)

    tie_lanes = jnp.where((lane_count(alive) > quota) & (thr > _KEY_NEG_INF), 1.0, 0.0)
    need_ties = jnp.max(tie_lanes) > 0.5

    @pl.when(need_ties)
    def _():
        attend(True)

    @pl.when(jnp.logical_not(need_ties))
    def _():
        attend(False)


def _dsa(aq_t, iq_t, small, ak, av_t, ik, bsz, seq):
    nb = seq // Q_BLOCK
    tk = KEY_TILE
    topk = min(TOPK_MAX, seq // 4)
    assert seq % tk == 0 and tk >= topk and tk & (tk - 1) == 0 and tk % PLANE_GROUP == 0
    nkt = seq // tk

    def qspec_t(width):
        return pl.BlockSpec((1, width, Q_BLOCK), lambda b, j: (b * nb + j, 0, 0))

    def qspec(width):
        return pl.BlockSpec((Q_BLOCK, width), lambda b, j: (b * nb + j, 0))

    def kspec(width):
        return pl.BlockSpec((seq, width), lambda b, j: (b, 0))

    cols = ATT_HEADS * Q_BLOCK
    return pl.pallas_call(
        functools.partial(_dsa_body, topk=topk, tk=tk),
        grid=(bsz, nb),
        in_specs=[qspec_t(ATT_Q_WIDTH), qspec_t(IDX_HEADS * IDX_DIM), qspec(LANES),
                  kspec(ATT_KV_WIDTH),
                  pl.BlockSpec((nkt, ATT_KV_WIDTH, tk), lambda b, j: (b, 0, 0)),
                  kspec(IDX_DIM)],
        out_specs=qspec(ATT_Q_WIDTH),
        out_shape=jax.ShapeDtypeStruct((bsz * seq, ATT_Q_WIDTH), BF16),
        scratch_shapes=[
            pltpu.VMEM((seq, Q_BLOCK), I32),
            pltpu.VMEM((WORD_BITS, seq // WORD_BITS, Q_BLOCK), I32),
            pltpu.VMEM((IDX_DIM, cols), BF16),
            pltpu.VMEM((ATT_KV_WIDTH, cols), BF16),
            pltpu.VMEM((tk, cols), BF16),
            pltpu.VMEM((ATT_KV_WIDTH, cols), F32),
        ],
        compiler_params=pltpu.CompilerParams(
            dimension_semantics=("arbitrary", "arbitrary"), vmem_limit_bytes=VMEM_LIMIT_BYTES),
        name="dsa_attention",
    )(aq_t, iq_t, small, ak, av_t, ik)


def _mlstm_body(mqk_ref, mv_ref, og_ref, small_ref, cw_ref, cb_ref, gb_ref, gn_ref, y_ref,
                ubuf, c_scr, n_scr, m_scr, *, chunk):
    ln = chunk
    d = MLSTM_HEAD_DIM
    halo = SUBLANES

    @pl.when(pl.program_id(1) == 0)
    def _():
        ubuf[0:halo, :] = jnp.zeros((halo, 2 * MLSTM_WIDTH), F32)
        c_scr[...] = jnp.zeros_like(c_scr)
        n_scr[...] = jnp.zeros_like(n_scr)
        m_scr[...] = jnp.zeros_like(m_scr)

    ubuf[halo:halo + ln, :] = mqk_ref[...]
    conv = cb_ref[...]
    for j in range(CONV_WIDTH):
        off = halo - (CONV_WIDTH - 1) + j
        conv = conv + cw_ref[j:j + 1, :] * ubuf[off:off + ln, :]
    ubuf[0:halo, :] = ubuf[ln:ln + halo, :]
    qk = conv * _sigmoid(conv)
    q_all = qk[:, :MLSTM_WIDTH]
    k_all = qk[:, MLSTM_WIDTH:] * (d ** -0.5)

    sm = small_ref[...] + gb_ref[...]
    lf = jnp.minimum(sm, 0.0) - jnp.log(1.0 + jnp.exp(-jnp.abs(sm)))
    sm_t = sm.T
    lf_t = lf.T
    rr = lax.broadcasted_iota(I32, (ln, ln), 0)
    cc = lax.broadcasted_iota(I32, (ln, ln), 1)
    causal = cc <= rr
    tri = jnp.where(causal, 1.0, 0.0).astype(BF16)
    tri_t = jnp.where(rr <= cc, 1.0, 0.0).astype(BF16)
    b_cols = _dot_exact_lhs(tri, lf)
    b_rows = _dot_exact_rhs(lf_t, tri_t)

    for h in range(MLSTM_HEADS):
        hs = slice(h * d, (h + 1) * d)
        ig_col = sm[:, SMALL_MI + h:SMALL_MI + h + 1]
        ig_row = sm_t[SMALL_MI + h:SMALL_MI + h + 1, :]
        b_col = b_cols[:, SMALL_MF + h:SMALL_MF + h + 1]
        b_row = b_rows[SMALL_MF + h:SMALL_MF + h + 1, :]
        m_st = m_scr[h:h + 1, 0:1]
        c_st = c_scr[h]
        n_st = n_scr[h:h + 1, :]

        dmat = jnp.where(causal, b_col - b_row + ig_row, NEG_INF)
        inter = b_col + m_st
        m_t = jnp.maximum(inter, jnp.max(dmat, axis=-1, keepdims=True))
        w_intra = jnp.exp(dmat - m_t)
        w_inter = jnp.exp(inter - m_t)
        qf = q_all[:, hs]
        kf = k_all[:, hs]
        qh = qf.astype(BF16)
        vh = mv_ref[:, hs]
        s = _dot_nt(qh, kf.astype(BF16)) * w_intra
        num = (w_inter * jnp.dot(qh, c_st.astype(BF16), preferred_element_type=F32)
               + jnp.dot(s.astype(BF16), vh, preferred_element_type=F32))
        den = (w_inter * jnp.sum(qf * n_st, axis=-1, keepdims=True)
               + jnp.sum(s, axis=-1, keepdims=True))
        hh = num / jnp.maximum(jnp.abs(den), jnp.exp(-m_t))

        b_last = b_col[ln - 1:ln, :]
        gdec = b_last - b_col + ig_col
        m_new = jnp.maximum(b_last + m_st, jnp.max(gdec, axis=0, keepdims=True))
        decay = jnp.exp(b_last + m_st - m_new)
        kw = kf * jnp.exp(gdec - m_new)
        c_scr[h] = decay * c_st + jnp.dot(kw.T.astype(BF16), vh, preferred_element_type=F32)
        n_scr[h:h + 1, :] = decay * n_st + jnp.sum(kw, axis=0, keepdims=True)
        m_scr[h:h + 1, :] = jnp.broadcast_to(m_new, (1, LANES))

        ms = jnp.mean(hh * hh, axis=-1, keepdims=True)
        yh = hh * lax.rsqrt(ms + EPS) * gn_ref[:, hs]
        y_ref[:, hs] = (og_ref[:, hs].astype(F32) * yh).astype(BF16)


def _mlstm(mqk, mv, og, small, w_conv, b_conv, b_igate, b_fgate, g_norm_out, bsz, seq):
    ln = min(MLSTM_CHUNK, seq)
    nc = seq // ln
    gbias = jnp.zeros((1, LANES), F32)
    gbias = gbias.at[0, SMALL_MI:SMALL_MI + MLSTM_HEADS].set(b_igate.astype(F32))
    gbias = gbias.at[0, SMALL_MF:SMALL_MF + MLSTM_HEADS].set(b_fgate.astype(F32))

    def row(width):
        return pl.BlockSpec((ln, width), lambda b, c: (b * nc + c, 0))

    def const(shape):
        return pl.BlockSpec(shape, lambda b, c: (0, 0))

    return pl.pallas_call(
        functools.partial(_mlstm_body, chunk=ln),
        grid=(bsz, nc),
        in_specs=[row(2 * MLSTM_WIDTH), row(MLSTM_WIDTH), row(MLSTM_WIDTH), row(LANES),
                  const((CONV_WIDTH, 2 * MLSTM_WIDTH)), const((1, 2 * MLSTM_WIDTH)),
                  const((1, LANES)), const((1, MLSTM_WIDTH))],
        out_specs=row(MLSTM_WIDTH),
        out_shape=jax.ShapeDtypeStruct((bsz * seq, MLSTM_WIDTH), BF16),
        scratch_shapes=[
            pltpu.VMEM((ln + SUBLANES, 2 * MLSTM_WIDTH), F32),
            pltpu.VMEM((MLSTM_HEADS, MLSTM_HEAD_DIM, MLSTM_HEAD_DIM), F32),
            pltpu.VMEM((SUBLANES, MLSTM_HEAD_DIM), F32),
            pltpu.VMEM((SUBLANES, LANES), F32),
        ],
        compiler_params=pltpu.CompilerParams(
            dimension_semantics=("arbitrary", "arbitrary"), vmem_limit_bytes=VMEM_LIMIT_BYTES),
        name="mlstm",
    )(mqk, mv, og, small, w_conv.astype(F32), b_conv.astype(F32)[None, :], gbias,
      g_norm_out.astype(F32)[None, :])


def _merge_body(x_ref, ya_ref, ym_ref, ga_ref, gm_ref, wpa_ref, wpm_ref, wo_ref, gmoe_ref,
                wrh_ref, wrl_ref, br_ref, x1_ref, xn_ref, ri_ref, rf_ref):
    pa = jnp.dot(ya_ref[...], wpa_ref[...], preferred_element_type=F32)
    pm = jnp.dot(ym_ref[...], wpm_ref[...], preferred_element_type=F32)
    merged = ga_ref[...].astype(F32) * pa + gm_ref[...].astype(F32) * pm
    x1 = x_ref[...] + jnp.dot(merged.astype(BF16), wo_ref[...], preferred_element_type=F32)
    x1_ref[...] = x1
    ms = jnp.mean(x1 * x1, axis=-1, keepdims=True)
    xn = x1 * lax.rsqrt(ms + EPS) * gmoe_ref[...]
    xn_ref[...] = xn

    xh = xn.astype(BF16)
    xl = (xn - xh.astype(F32)).astype(BF16)
    logits = (jnp.dot(xh, wrh_ref[...], preferred_element_type=F32)
              + jnp.dot(xh, wrl_ref[...], preferred_element_type=F32)
              + jnp.dot(xl, wrh_ref[...], preferred_element_type=F32)) + br_ref[...]
    lane = lax.broadcasted_iota(I32, logits.shape, 1).astype(F32)

    def first_argmax(vals, mask):
        v = jnp.where(mask, vals, -jnp.inf)
        top = jnp.max(v, axis=-1, keepdims=True)
        idx = jnp.min(jnp.where(mask & (v == top), lane, float(LANES)), axis=-1, keepdims=True)
        return top, idx

    is_group = lane < N_GROUPS
    g_max, g_sel = first_argmax(logits, is_group)
    g_den = jnp.sum(jnp.where(is_group, jnp.exp(logits - g_max), 0.0), axis=-1, keepdims=True)
    g_top = 1.0 / g_den
    lo = N_GROUPS + g_sel * EXPERTS_PER_GROUP
    in_group = (lane >= lo) & (lane < lo + EXPERTS_PER_GROUP)
    v1, i1 = first_argmax(logits, in_group)
    v2, i2 = first_argmax(logits, in_group & (lane != i1))
    e2 = jnp.exp(v2 - v1)
    p1 = 1.0 / (1.0 + e2)
    gate1 = p1 * g_top
    gate2 = (e2 * p1) * g_top
    ri_ref[...] = jnp.where(lane == 0, i1 - N_GROUPS,
                            jnp.where(lane == 1, i2 - N_GROUPS, 0.0)).astype(I32)
    rf_ref[...] = jnp.where(lane == 0, gate1, jnp.where(lane == 1, gate2, 0.0))


def _merge(x2, y_att, y_ml, ga, gm, w_proj_att, w_proj_mlstm, w_out, g_norm_moe,
           w_router_group, b_router_group, w_router_expert, b_router_expert):
    t_rows = x2.shape[0]
    tm = min(MERGE_ROWS, t_rows)
    n_r = N_GROUPS + N_EXPERTS
    wr = jnp.concatenate([w_router_group, w_router_expert,
                          jnp.zeros((D_MODEL, LANES - n_r), F32)], axis=-1).astype(F32)
    wr_hi = wr.astype(BF16)
    wr_lo = (wr - wr_hi.astype(F32)).astype(BF16)
    br =jnp.concatenate([b_router_group, b_router_expert,
                          jnp.zeros((LANES - n_r,), F32)]).astype(F32)[None, :]

    def row(width):
        return pl.BlockSpec((tm, width), lambda i: (i, 0))

    def const(shape):
        return pl.BlockSpec(shape, lambda i: (0, 0))

    out_shapes = (
        jax.ShapeDtypeStruct((t_rows, D_MODEL), F32),
        jax.ShapeDtypeStruct((t_rows, D_MODEL), F32),
        jax.ShapeDtypeStruct((t_rows, LANES), I32),
        jax.ShapeDtypeStruct((t_rows, LANES), F32),
    )
    return pl.pallas_call(
        _merge_body,
        grid=(t_rows // tm,),
        in_specs=[row(D_MODEL), row(ATT_Q_WIDTH), row(MLSTM_WIDTH), row(D_MODEL), row(D_MODEL),
                  const((ATT_Q_WIDTH, D_MODEL)), const((MLSTM_WIDTH, D_MODEL)),
                  const((D_MODEL, D_MODEL)), const((1, D_MODEL)), const((D_MODEL, LANES)),
                  const((D_MODEL, LANES)), const((1, LANES))],
        out_specs=[row(s.shape[1]) for s in out_shapes],
        out_shape=out_shapes,
        compiler_params=pltpu.CompilerParams(
            dimension_semantics=("arbitrary",), vmem_limit_bytes=VMEM_LIMIT_BYTES),
        name="merge_route",
    )(x2, y_att, y_ml, ga, gm, w_proj_att.astype(BF16), w_proj_mlstm.astype(BF16),
      w_out.astype(BF16), g_norm_moe.astype(F32)[None, :], wr_hi, wr_lo, br)


def _expert_body(be_ref, tok_cur_ref, tok_nxt_ref, xn_hbm, wg_ref, wu_ref, wd_ref,
                 ys_ref, xbuf, wg_bf, wu_bf, wd_bf, sem, *, rows):
    i = pl.program_id(0)
    nblk = pl.num_programs(0)
    slot = lax.rem(i, 2)
    nslot = 1 - slot

    @pl.when((i == 0) | (be_ref[i] != be_ref[jnp.maximum(i - 1, 0)]))
    def _():
        wg_bf[...] = wg_ref[0].astype(BF16)
        wu_bf[...] = wu_ref[0].astype(BF16)
        wd_bf[...] = wd_ref[0].astype(BF16)

    def row_copy(tok, dst_slot, r):
        return pltpu.make_async_copy(
            xn_hbm.at[pl.ds(tok, 1), :], xbuf.at[dst_slot, pl.ds(r, 1), :], sem.at[dst_slot])

    def all_rows(dst_slot):
        return pltpu.make_async_copy(
            xn_hbm.at[pl.ds(0, rows), :], xbuf.at[dst_slot], sem.at[dst_slot])

    @pl.when(i == 0)
    def _():
        for r in range(rows):
            row_copy(tok_cur_ref[0, 0, r], 0, r).start(priority=r % DMA_PRIORITIES)

    all_rows(slot).wait()

    def prefetch(part):
        for r in range(part * rows // 4, (part + 1) * rows // 4):
            row_copy(tok_nxt_ref[0, 0, r], nslot, r).start(priority=r % DMA_PRIORITIES)

    prefetch(0)
    xb = xbuf[slot].astype(BF16)
    a = jnp.dot(xb, wg_bf[...], preferred_element_type=F32)
    prefetch(1)
    u = jnp.dot(xb, wu_bf[...], preferred_element_type=F32)
    prefetch(2)
    hid = (a * _sigmoid(a) * u).astype(BF16)
    prefetch(3)
    ys_ref[...] = jnp.dot(hid, wd_bf[...], preferred_element_type=F32)

    @pl.when(i == nblk - 1)
    def _():
        all_rows(nslot).wait()


def _experts(xn, buf_tok, block_e, w_gate, w_up, w_down):
    rows = EXPERT_ROWS
    cap = buf_tok.shape[0]
    nblk = cap // rows
    tok3 = buf_tok.reshape(nblk, 1, rows)
    grid_spec = pltpu.PrefetchScalarGridSpec(
        num_scalar_prefetch=1,
        grid=(nblk,),
        in_specs=[
            pl.BlockSpec((1, 1, rows), lambda i, be: (i, 0, 0), memory_space=pltpu.SMEM),
            pl.BlockSpec((1, 1, rows), lambda i, be: (jnp.minimum(i + 1, nblk - 1), 0, 0),
                         memory_space=pltpu.SMEM),
            pl.BlockSpec(memory_space=pl.ANY),
            pl.BlockSpec((1, D_MODEL, D_EXPERT), lambda i, be: (be[i], 0, 0)),
            pl.BlockSpec((1, D_MODEL, D_EXPERT), lambda i, be: (be[i], 0, 0)),
            pl.BlockSpec((1, D_EXPERT, D_MODEL), lambda i, be: (be[i], 0, 0)),
        ],
        out_specs=pl.BlockSpec((rows, D_MODEL), lambda i, be: (i, 0)),
        scratch_shapes=[pltpu.VMEM((2, rows, D_MODEL), F32),
                        pltpu.VMEM((D_MODEL, D_EXPERT), BF16),
                        pltpu.VMEM((D_MODEL, D_EXPERT), BF16),
                        pltpu.VMEM((D_EXPERT, D_MODEL), BF16),
                        pltpu.SemaphoreType.DMA((2,))],
    )
    return pl.pallas_call(
        functools.partial(_expert_body, rows=rows),
        grid_spec=grid_spec,
        out_shape=jax.ShapeDtypeStruct((cap, D_MODEL), F32),
        compiler_params=pltpu.CompilerParams(
            dimension_semantics=("arbitrary",), vmem_limit_bytes=VMEM_LIMIT_BYTES),
        name="moe_experts",
    )(block_e, tok3, tok3, xn, w_gate.astype(F32), w_up.astype(F32), w_down.astype(F32))


def _combine_body(dcur_ref, dnxt_ref, x1_ref, rf_ref, ys_hbm, out_ref, ybuf, sem, *, rows):
    i = pl.program_id(0)
    nt = pl.num_programs(0)
    slot = lax.rem(i, 2)
    nslot = 1 - slot

    def row_copy(src, dst_slot, k, r):
        return pltpu.make_async_copy(
            ys_hbm.at[pl.ds(src, 1), :], ybuf.at[dst_slot, k, pl.ds(r, 1), :], sem.at[dst_slot])

    def wait_all(dst_slot):
        for k in range(TOP_K_EXPERTS):
            pltpu.make_async_copy(
                ys_hbm.at[pl.ds(0, rows), :], ybuf.at[dst_slot, k], sem.at[dst_slot]).wait()

    def gather(dest_ref, dst_slot):
        for r in range(rows):
            for k in range(TOP_K_EXPERTS):
                row_copy(dest_ref[0, 0, TOP_K_EXPERTS * r + k], dst_slot, k, r).start(
                    priority=k % DMA_PRIORITIES)

    @pl.when(i == 0)
    def _():
        gather(dcur_ref, 0)

    wait_all(slot)
    gather(dnxt_ref, nslot)
    gates = rf_ref[...]
    out = x1_ref[...]
    for k in range(TOP_K_EXPERTS):
        out = out + gates[:, k:k + 1] * ybuf[slot, k]
    out_ref[...] = out

    @pl.when(i == nt - 1)
    def _():
        wait_all(nslot)


def _combine(x1, rf, ys, dest):
    t_rows = x1.shape[0]
    rows = min(COMBINE_ROWS, t_rows)
    nt = t_rows // rows
    dest3 = dest.reshape(nt, 1, rows * TOP_K_EXPERTS)
    dspec = functools.partial(pl.BlockSpec, (1, 1, rows * TOP_K_EXPERTS), memory_space=pltpu.SMEM)
    return pl.pallas_call(
        functools.partial(_combine_body, rows=rows),
        grid=(nt,),
        in_specs=[
            dspec(lambda i: (i, 0, 0)),
            dspec(lambda i: (jnp.minimum(i + 1, nt - 1), 0, 0)),
            pl.BlockSpec((rows, D_MODEL), lambda i: (i, 0)),
            pl.BlockSpec((rows, LANES), lambda i: (i, 0)),
            pl.BlockSpec(memory_space=pl.ANY),
        ],
        out_specs=pl.BlockSpec((rows, D_MODEL), lambda i: (i, 0)),
        out_shape=jax.ShapeDtypeStruct((t_rows, D_MODEL), F32),
        scratch_shapes=[pltpu.VMEM((2, TOP_K_EXPERTS, rows, D_MODEL), F32),
                        pltpu.SemaphoreType.DMA((2,))],
        compiler_params=pltpu.CompilerParams(
            dimension_semantics=("arbitrary",), vmem_limit_bytes=VMEM_LIMIT_BYTES),
        name="moe_combine",
    )(dest3, dest3, x1, rf, ys)


def _route_plan(expert_id):
    t_rows = expert_id.shape[0]
    m = t_rows * TOP_K_EXPERTS
    rows = EXPERT_ROWS
    e_flat = expert_id.reshape(m)
    onehot = (e_flat[:, None] == jnp.arange(N_EXPERTS, dtype=I32)[None, :]).astype(I32)
    running = jnp.cumsum(onehot, axis=0)
    counts = running[-1]
    rank = jnp.sum(onehot * running, axis=1) - 1
    padded = ((counts + rows - 1) // rows) * rows
    pends = jnp.cumsum(padded)
    pstarts = pends - padded
    cap = ((m + N_EXPERTS * rows + rows - 1) // rows) * rows
    nblk = cap // rows
    blk_start = jnp.arange(nblk, dtype=I32) * rows
    block_e = jnp.minimum(jnp.sum((pends[None, :] <= blk_start[:, None]).astype(I32), axis=1),
                          N_EXPERTS - 1)
    dest = (jnp.sum(onehot * pstarts[None, :], axis=1) + rank).astype(I32)
    order = jnp.argsort(e_flat, stable=True).astype(I32)
    starts = jnp.cumsum(counts) - counts
    slot = jnp.arange(cap, dtype=I32)
    off = slot - jnp.repeat(pstarts[block_e], rows)
    live = (off < jnp.repeat(counts[block_e], rows)) & (slot < pends[-1])
    src = jnp.clip(jnp.repeat(starts[block_e], rows) + off, 0, m - 1)
    buf_tok = jnp.where(live, order[src] // TOP_K_EXPERTS, slot % t_rows).astype(I32)
    return buf_tok, block_e, dest


def kernel(x, positions, g_norm_mix, w_in, g_q_att, g_k_att, w_conv_mlstm, b_conv_mlstm, b_igate,
           b_fgate, g_norm_mlstm_out, w_proj_att, w_proj_mlstm, w_out, g_norm_moe, w_router_group,
           b_router_group, w_router_expert, b_router_expert, w_exp_gate, w_exp_up, w_exp_down):
    bsz, seq, dm = x.shape
    depth = g_norm_mix.shape[0]
    x2 = x.reshape(bsz * seq, dm)
    pos2 = positions.reshape(bsz * seq, 1).astype(I32)
    for l in range(depth):
        aq, ak, av, iq, ik, small, mqk, mv, og, ga, gm = _in_proj(
            x2, pos2, g_norm_mix[l], w_in[l], g_q_att[l], g_k_att[l])
        y_att = _dsa(aq, iq, small, ak, av, ik, bsz, seq)
        y_ml = _mlstm(mqk, mv, og, small, w_conv_mlstm[l], b_conv_mlstm[l], b_igate[l], b_fgate[l],
                      g_norm_mlstm_out[l], bsz, seq)
        x1, xn, ri, rf = _merge(x2, y_att, y_ml, ga, gm, w_proj_att[l], w_proj_mlstm[l], w_out[l],
                                g_norm_moe[l], w_router_group[l], b_router_group[l],
                                w_router_expert[l], b_router_expert[l])
        buf_tok, block_e, dest = _route_plan(ri[:, :TOP_K_EXPERTS])
        ys = _experts(xn, buf_tok, block_e, w_exp_gate[l], w_exp_up[l], w_exp_down[l])
        x2 = _combine(x1, rf, ys, dest)
    return x2.reshape(bsz, seq, dm)
```

```python
import functools

import numpy as np
import jax
import jax.numpy as jnp
from jax import lax
from jax.experimental import pallas as pl
from jax.experimental.pallas import tpu as pltpu

F32 = jnp.float32
BF16 = jnp.bfloat16
I32 = jnp.int32

D_MODEL = 1024
CHUNK = 64
Q_BLOCK = 128
EPS = 1e-6
NEG_INF = -1e30

ATT_HEADS = 8
ATT_KV_HEADS = 2
ATT_HEAD_DIM = 64
ATT_Q_WIDTH = ATT_HEADS * ATT_HEAD_DIM
ATT_KV_WIDTH = ATT_KV_HEADS * ATT_HEAD_DIM
IDX_HEADS = 8
IDX_DIM = 32
TOPK_MAX = 256
ROPE_THETA = 500000.0
ROPE_FRACTION = 4

MLSTM_HEADS = 4
MLSTM_HEAD_DIM = 128
MLSTM_WIDTH = MLSTM_HEADS * MLSTM_HEAD_DIM
CONV_WIDTH = 4

N_GROUPS = 4
EXPERTS_PER_GROUP = 8
N_EXPERTS = N_GROUPS * EXPERTS_PER_GROUP
TOP_K_EXPERTS = 2
D_EXPERT = 512

LANES = 128
SUBLANES = 8
VMEM_LIMIT_BYTES = 56 * 1024 * 1024

SMALL_IK = 0
SMALL_IW = IDX_DIM
SMALL_MI = SMALL_IW + IDX_HEADS
SMALL_MF = SMALL_MI + MLSTM_HEADS

COL_AQ = 0
COL_AK = COL_AQ + ATT_Q_WIDTH
COL_AV = COL_AK + ATT_KV_WIDTH
COL_IQ = COL_AV + ATT_KV_WIDTH
COL_SMALL = COL_IQ + IDX_HEADS * IDX_DIM
COL_MQK = COL_SMALL + LANES
COL_MV = COL_MQK + 2 * MLSTM_WIDTH
COL_MO = COL_MV + MLSTM_WIDTH
COL_GA = COL_MO + MLSTM_WIDTH
COL_GM = COL_GA + D_MODEL
COL_END = COL_GM + D_MODEL

KEY_TILE = 512
MLSTM_CHUNK = 256
MERGE_ROWS = 512
EXPERT_ROWS = 256
COMBINE_ROWS = 256
WORD_BITS = 32
PLANE_GROUP = WORD_BITS * SUBLANES

DMA_PRIORITIES = 2

LOG2_E = 1.4426950408889634
_INT_MIN = -(2 ** 31)


def _sortable_key_const(v):
    b = int(np.array(v, np.float32).view(np.int32))
    return b if b >= 0 else b ^ 0x7FFFFFFF


_KEY_NEG_INF = _sortable_key_const(NEG_INF)
_KEY_VALID = _sortable_key_const(NEG_INF * 0.5)


def _split3(a):
    hi = a.astype(BF16)
    r1 = a - hi.astype(F32)
    mid = r1.astype(BF16)
    lo = (r1 - mid.astype(F32)).astype(BF16)
    return hi, mid, lo


def _dot_exact_rhs(a, b_bf16):
    hi, mid, lo = _split3(a)
    d = functools.partial(jnp.dot, preferred_element_type=F32)
    return d(hi, b_bf16) + d(mid, b_bf16) + d(lo, b_bf16)


def _dot_exact_lhs(a_bf16, b):
    hi, mid, lo = _split3(b)
    d = functools.partial(jnp.dot, preferred_element_type=F32)
    return d(a_bf16, hi) + d(a_bf16, mid) + d(a_bf16, lo)


def _dot_nt(a, b):
    return lax.dot_general(a, b, (((1,), (1,)), ((), ())), preferred_element_type=F32)


def _sigmoid(x):
    return 0.5 * jnp.tanh(0.5 * x) + 0.5


def _in_proj_body(x_ref, pos_ref, g_ref, w_ref, rc_ref, bd_ref, gq_ref, gk_ref,
                  aq_ref, ak_ref, av_ref, iq_ref, ik_ref, small_ref, mqk_ref, mv_ref,
                  og_ref, ga_ref, gm_ref):
    x = x_ref[...]
    ms = jnp.mean(x * x, axis=-1, keepdims=True)
    h = (x * lax.rsqrt(ms + EPS) * g_ref[...]).astype(BF16)
    posf = pos_ref[...].astype(F32)

    def proj(lo, hi):
        return jnp.dot(h, w_ref[:, lo:hi], preferred_element_type=F32)

    def rope(t, cos, s_up, s_dn, half):
        n = t.shape[-1]
        return t * cos + pltpu.roll(t, half, 1) * s_up + pltpu.roll(t, n - half, 1) * s_dn

    def head_norm(t, gain):
        ssum = _dot_exact_rhs(t * t, bd_ref[...])
        return t * lax.rsqrt(ssum * (1.0 / ATT_HEAD_DIM) + EPS) * gain

    ang_a = posf * rc_ref[0:1, :]
    cos_a, sin_a = jnp.cos(ang_a), jnp.sin(ang_a)
    up_a, dn_a = sin_a * rc_ref[1:2, :], sin_a * rc_ref[2:3, :]
    ang_i = posf * rc_ref[3:4, :]
    cos_i, sin_i = jnp.cos(ang_i), jnp.sin(ang_i)
    up_i, dn_i = sin_i * rc_ref[4:5, :], sin_i * rc_ref[5:6, :]
    small_mask = rc_ref[6:7, :]
    cos_s = 1.0 + (cos_i - 1.0) * small_mask
    up_s, dn_s = up_i * small_mask, dn_i * small_mask

    att_half = ATT_HEAD_DIM // ROPE_FRACTION // 2
    idx_half = IDX_DIM // ROPE_FRACTION // 2
    q_scale = ATT_HEAD_DIM ** -0.5 * LOG2_E
    tm = x.shape[0]

    def store_qblock_t(ref, c, t):
        for r in range(tm // Q_BLOCK):
            ref[r, c * LANES:(c + 1) * LANES, :] = t[r * Q_BLOCK:(r + 1) * Q_BLOCK, :].T.astype(BF16)

    for c in range(ATT_Q_WIDTH // LANES):
        t = proj(COL_AQ + c * LANES, COL_AQ + (c + 1) * LANES)
        t = rope(head_norm(t, gq_ref[...]), cos_a, up_a, dn_a, att_half)
        store_qblock_t(aq_ref, c, t * q_scale)
    t = proj(COL_AK, COL_AK + LANES)
    ak_ref[...] = rope(head_norm(t, gk_ref[...]), cos_a, up_a, dn_a, att_half).astype(BF16)
    av_ref[0] = proj(COL_AV, COL_AV + LANES).T.astype(BF16)
    for c in range(IDX_HEADS * IDX_DIM // LANES):
        t = proj(COL_IQ + c * LANES, COL_IQ + (c + 1) * LANES)
        store_qblock_t(iq_ref, c, rope(t, cos_i, up_i, dn_i, idx_half))
    t = rope(proj(COL_SMALL, COL_SMALL + LANES), cos_s, up_s, dn_s, idx_half)
    small_ref[...] = t
    ik_ref[...] = t[:, SMALL_IK:SMALL_IK + IDX_DIM].astype(BF16)
    for c in range(2 * MLSTM_WIDTH // 512):
        mqk_ref[:, c * 512:(c + 1) * 512] = proj(COL_MQK + c * 512, COL_MQK + (c + 1) * 512)
    mv_ref[...] = proj(COL_MV, COL_MV + MLSTM_WIDTH).astype(BF16)
    og_ref[...] = _sigmoid(proj(COL_MO, COL_MO + MLSTM_WIDTH)).astype(BF16)
    for c in range(D_MODEL // 512):
        ga_ref[:, c * 512:(c + 1) * 512] = proj(
            COL_GA + c * 512, COL_GA + (c + 1) * 512).astype(BF16)
        gm_ref[:, c * 512:(c + 1) * 512] = proj(
            COL_GM + c * 512, COL_GM + (c + 1) * 512).astype(BF16)


def _rope_consts():
    lane = np.arange(LANES)

    def inv_freq(rot):
        half = rot // 2
        return jnp.float32(ROPE_THETA) ** (-jnp.arange(half, dtype=F32) * 2.0 / rot)

    rot_a = ATT_HEAD_DIM // ROPE_FRACTION
    rot_i = IDX_DIM // ROPE_FRACTION
    fa, fi = inv_freq(rot_a), inv_freq(rot_i)
    ja, ji = lane % ATT_HEAD_DIM, lane % IDX_DIM
    rc = jnp.zeros((16, LANES), F32)
    rc = rc.at[0].set(jnp.where(ja < rot_a, fa[ja % (rot_a // 2)], 0.0))
    rc = rc.at[1].set(jnp.asarray(((ja >= rot_a // 2) & (ja < rot_a)).astype(np.float32)))
    rc = rc.at[2].set(jnp.asarray(-(ja < rot_a // 2).astype(np.float32)))
    rc = rc.at[3].set(jnp.where(ji < rot_i, fi[ji % (rot_i // 2)], 0.0))
    rc = rc.at[4].set(jnp.asarray(((ji >= rot_i // 2) & (ji < rot_i)).astype(np.float32)))
    rc = rc.at[5].set(jnp.asarray(-(ji < rot_i // 2).astype(np.float32)))
    rc = rc.at[6].set(jnp.asarray((lane < IDX_DIM).astype(np.float32)))
    return rc


def _pack_w_in(w_in):
    sizes = (ATT_Q_WIDTH, ATT_KV_WIDTH, ATT_KV_WIDTH, IDX_HEADS * IDX_DIM, IDX_DIM, IDX_HEADS,
             MLSTM_WIDTH, MLSTM_WIDTH, MLSTM_WIDTH, MLSTM_WIDTH, MLSTM_HEADS, MLSTM_HEADS,
             D_MODEL, D_MODEL)
    pts = np.cumsum(sizes)[:-1].tolist()
    (aq, ak, av, iq, ik, iw, mq, mk, mv, mo, mi, mf, ga, gm) = jnp.split(w_in, pts, axis=-1)
    pad = jnp.zeros((D_MODEL, LANES - (SMALL_MF + MLSTM_HEADS)), w_in.dtype)
    small = jnp.concatenate([ik, iw, mi, mf, pad], axis=-1)
    return jnp.concatenate([aq, ak, av, iq, small, mq, mk, mv, mo, ga, gm], axis=-1).astype(BF16)


def _in_proj(x2, pos2, g_norm_mix, w_in, g_q_att, g_k_att):
    t_rows = x2.shape[0]
    tm = KEY_TILE
    w = _pack_w_in(w_in)
    lane = np.arange(LANES)
    bd = jnp.asarray((lane[:, None] // ATT_HEAD_DIM == lane[None, :] // ATT_HEAD_DIM)
                     .astype(np.float32)).astype(BF16)
    reps = LANES // ATT_HEAD_DIM
    gq = jnp.tile(g_q_att.astype(F32), reps)[None, :]
    gk = jnp.tile(g_k_att.astype(F32), reps)[None, :]

    def row(width):
        return pl.BlockSpec((tm, width), lambda i: (i, 0))

    def const(shape):
        return pl.BlockSpec(shape, lambda i: (0, 0))

    qb = tm // Q_BLOCK

    def qblock_t(width):
        return pl.BlockSpec((qb, width, Q_BLOCK), lambda i: (i, 0, 0))

    out_shapes = (
        jax.ShapeDtypeStruct((t_rows // Q_BLOCK, ATT_Q_WIDTH, Q_BLOCK), BF16),
        jax.ShapeDtypeStruct((t_rows, ATT_KV_WIDTH), BF16),
        jax.ShapeDtypeStruct((t_rows // tm, ATT_KV_WIDTH, tm), BF16),
        jax.ShapeDtypeStruct((t_rows // Q_BLOCK, IDX_HEADS * IDX_DIM, Q_BLOCK), BF16),
        jax.ShapeDtypeStruct((t_rows, IDX_DIM), BF16),
        jax.ShapeDtypeStruct((t_rows, LANES), F32),
        jax.ShapeDtypeStruct((t_rows, 2 * MLSTM_WIDTH), F32),
        jax.ShapeDtypeStruct((t_rows, MLSTM_WIDTH), BF16),
        jax.ShapeDtypeStruct((t_rows, MLSTM_WIDTH), BF16),
        jax.ShapeDtypeStruct((t_rows, D_MODEL), BF16),
        jax.ShapeDtypeStruct((t_rows, D_MODEL), BF16),
    )
    return pl.pallas_call(
        _in_proj_body,
        grid=(t_rows // tm,),
        in_specs=[row(D_MODEL), row(1), const((1, D_MODEL)), const((D_MODEL, COL_END)),
                  const((16, LANES)), const((LANES, LANES)), const((1, LANES)), const((1, LANES))],
        out_specs=[qblock_t(ATT_Q_WIDTH), row(ATT_KV_WIDTH),
                   pl.BlockSpec((1, ATT_KV_WIDTH, tm), lambda i: (i, 0, 0)),
                   qblock_t(IDX_HEADS * IDX_DIM)] + [row(s.shape[1]) for s in out_shapes[4:]],
        out_shape=out_shapes,
        compiler_params=pltpu.CompilerParams(
            dimension_semantics=("arbitrary",), vmem_limit_bytes=VMEM_LIMIT_BYTES),
        name="in_proj",
    )(x2, pos2, g_norm_mix.astype(F32)[None, :], w, _rope_consts(), bd, gq, gk)


def _bit_transpose32(words):
    a = list(words)
    j, m = 16, 0x0000FFFF
    while j:
        k = 0
        while k < WORD_BITS:
            t = (a[k] ^ lax.shift_right_logical(a[k + j], jnp.int32(j))) & m
            a[k] = a[k] ^ t
            a[k + j] = a[k + j] ^ lax.shift_left(t, jnp.int32(j))
            k = (k + j + 1) & ~j
        j >>= 1
        m = m ^ (m << j)
    return a


def _dsa_body(aq_ref, iq_ref, small_ref, k_ref, vt_ref, ik_ref, out_ref,
              key_scr, plane_scr, qi_scr, q2_scr, p_scr, acc_scr, *, topk, tk):
    blk = pl.program_id(1)
    nq = Q_BLOCK
    rep = ATT_HEADS // ATT_KV_HEADS
    groups_per_tile = tk // PLANE_GROUP
    n_tiles = lax.shift_right_logical(blk * nq + nq + tk - 1, int(np.log2(tk)))
    idx_scale = (IDX_DIM * IDX_HEADS) ** -0.5
    plane_rows = plane_scr.shape[1]

    @pl.when((pl.program_id(0) == 0) & (blk == 0))
    def _():
        plane_scr[...] = jnp.zeros_like(plane_scr)

    for h in range(IDX_HEADS):
        qi_scr[:, h * nq:(h + 1) * nq] = iq_ref[0, h * IDX_DIM:(h + 1) * IDX_DIM, :]
    w_rows = small_ref[...].T[SMALL_IW:SMALL_IW + IDX_HEADS, :] * idx_scale

    q_chunk = lax.shift_right_logical(
        blk * nq + lax.broadcasted_iota(I32, (1, nq), 1), int(np.log2(CHUNK)))
    key_row = lax.broadcasted_iota(I32, (tk, 1), 0)

    def score_tile(t, carry):
        start = pl.multiple_of(t * tk, tk)
        rel = jnp.dot(ik_ref[pl.ds(start, tk), :], qi_scr[...], preferred_element_type=F32)
        sc = jnp.maximum(rel[:, 0:nq], 0.0) * w_rows[0:1]
        for h in range(1, IDX_HEADS):
            sc = sc + jnp.maximum(rel[:, h * nq:(h + 1) * nq], 0.0) * w_rows[h:h + 1]
        visible = lax.shift_right_logical(start + key_row, int(np.log2(CHUNK))) <= q_chunk
        sc = jnp.where(visible, sc, NEG_INF)
        bits = lax.bitcast_convert_type(sc, I32)
        key = jnp.where(bits < 0, bits ^ 0x7FFFFFFF, bits)
        key_scr[pl.ds(start, tk), :] = key
        biased = key ^ _INT_MIN
        for g in range(groups_per_tile):
            words = [biased[g * PLANE_GROUP + i * SUBLANES:g * PLANE_GROUP + (i + 1) * SUBLANES, :]
                     for i in range(WORD_BITS)]
            planes = _bit_transpose32(words)
            row0 = pl.multiple_of((t * groups_per_tile + g) * SUBLANES, SUBLANES)
            for p in range(WORD_BITS):
                plane_scr[p, pl.ds(row0, SUBLANES), :] = planes[p]
        return carry

    lax.fori_loop(0, n_tiles, score_tile, 0)

    group_of_row = lax.shift_right_logical(
        lax.broadcasted_iota(I32, (plane_rows, nq), 0), int(np.log2(SUBLANES)))
    alive = jnp.where(group_of_row < n_tiles * groups_per_tile, -1, 0)
    k_rem = jnp.full((1, nq), float(topk), F32)
    thr_u = jnp.zeros((1, nq), I32)

    def lane_count(mask_words):
        pc = lax.population_count(mask_words)
        part = jnp.sum(pc.reshape(plane_rows // SUBLANES, SUBLANES, nq), axis=0)
        return jnp.sum(part.astype(F32), axis=0, keepdims=True)

    for p in range(WORD_BITS):
        bit_value = _INT_MIN if p == 0 else 1 << (WORD_BITS - 1 - p)
        ones = alive & plane_scr[p]
        c_ones = lane_count(ones)
        take = c_ones >= k_rem
        alive = jnp.where(take, ones, alive ^ ones)
        k_rem = jnp.where(take, k_rem, k_rem - c_ones)
        thr_u = jnp.where(take, thr_u | bit_value, thr_u)
    thr = thr_u ^ _INT_MIN
    quota = k_rem

    q2_scr[...] = jnp.zeros_like(q2_scr)
    for h in range(ATT_HEADS):
        g = h // rep
        q2_scr[g * ATT_HEAD_DIM:(g + 1) * ATT_HEAD_DIM, h * nq:(h + 1) * nq] = (
            aq_ref[0, h * ATT_HEAD_DIM:(h + 1) * ATT_HEAD_DIM, :])

    def attend(with_ties):
        acc_scr[...] = jnp.zeros_like(acc_scr)
        if with_ties:
            rr = lax.broadcasted_iota(I32, (tk, tk), 0)
            cc = lax.broadcasted_iota(I32, (tk, tk), 1)
            earlier = jnp.where(cc < rr, 1.0, 0.0).astype(BF16)

        def tile(t, carry):
            m_all, l_all, eq_seen = carry
            start = pl.multiple_of(t * tk, tk)
            key = key_scr[pl.ds(start, tk), :]
            valid = key > _KEY_VALID
            if with_ties:
                eq = key == thr
                eqf = jnp.where(eq, 1.0, 0.0)
                rank = jnp.dot(earlier, eqf.astype(BF16), preferred_element_type=F32) + eq_seen
                sel = valid & ((key > thr) | (eq & (rank < quota)))
                eq_seen = eq_seen + jnp.sum(eqf, axis=0, keepdims=True)
            else:
                sel = valid & (key >= thr)
            bias = jnp.where(sel, 0.0, NEG_INF)
            s = jnp.dot(k_ref[pl.ds(start, tk), :], q2_scr[...], preferred_element_type=F32)
            m_rows, l_rows, a_rows = [], [], []
            for h in range(ATT_HEADS):
                cols = slice(h * nq, (h + 1) * nq)
                sh = s[:, cols] + bias
                m_old = m_all[h:h + 1]
                m_new = jnp.maximum(m_old, jnp.max(sh, axis=0, keepdims=True))
                p = jnp.exp2(sh - m_new)
                alpha = jnp.exp2(m_old - m_new)
                l_rows.append(alpha * l_all[h:h + 1] + jnp.sum(p, axis=0, keepdims=True))
                p_scr[:, cols] = p.astype(BF16)
                m_rows.append(m_new)
                a_rows.append(alpha)
            acc_scr[...] = (acc_scr[...] * jnp.concatenate(a_rows, axis=1)
                            + jnp.dot(vt_ref[t], p_scr[...], preferred_element_type=F32))
            return jnp.concatenate(m_rows, axis=0), jnp.concatenate(l_rows, axis=0), eq_seen

        m_all, l_all, _ = lax.fori_loop(
            0, n_tiles, tile,
            (jnp.full((ATT_HEADS, nq), NEG_INF, F32), jnp.zeros((ATT_HEADS, nq), F32),
             jnp.zeros((1, nq), F32)))
        inv_l = 1.0 / l_all
        for j in range(ATT_HEADS // 2):
            g = (2 * j) // rep
            halves = [acc_scr[g * ATT_HEAD_DIM:(g + 1) * ATT_HEAD_DIM, h * nq:(h + 1) * nq]
                      * inv_l[h:h + 1] for h in (2 * j, 2 * j + 1)]
            out_ref[:, j * LANES:(j + 1) * LANES] = jnp.concatenate(halves, axis=0).T.astype(BF16)

    tie_lanes = jnp.where((lane_count(alive) > quota) & (thr > _KEY_NEG_INF), 1.0, 0.0)
    need_ties = jnp.max(tie_lanes) > 0.5

    @pl.when(need_ties)
    def _():
        attend(True)

    @pl.when(jnp.logical_not(need_ties))
    def _():
        attend(False)


def _dsa(aq_t, iq_t, small, ak, av_t, ik, bsz, seq):
    nb = seq // Q_BLOCK
    tk = KEY_TILE
    topk = min(TOPK_MAX, seq // 4)
    assert seq % tk == 0 and tk >= topk and tk & (tk - 1) == 0 and tk % PLANE_GROUP == 0
    nkt = seq // tk

    def qspec_t(width):
        return pl.BlockSpec((1, width, Q_BLOCK), lambda b, j: (b * nb + j, 0, 0))

    def qspec(width):
        return pl.BlockSpec((Q_BLOCK, width), lambda b, j: (b * nb + j, 0))

    def kspec(width):
        return pl.BlockSpec((seq, width), lambda b, j: (b, 0))

    cols = ATT_HEADS * Q_BLOCK
    return pl.pallas_call(
        functools.partial(_dsa_body, topk=topk, tk=tk),
        grid=(bsz, nb),
        in_specs=[qspec_t(ATT_Q_WIDTH), qspec_t(IDX_HEADS * IDX_DIM), qspec(LANES),
                  kspec(ATT_KV_WIDTH),
                  pl.BlockSpec((nkt, ATT_KV_WIDTH, tk), lambda b, j: (b, 0, 0)),
                  kspec(IDX_DIM)],
        out_specs=qspec(ATT_Q_WIDTH),
        out_shape=jax.ShapeDtypeStruct((bsz * seq, ATT_Q_WIDTH), BF16),
        scratch_shapes=[
            pltpu.VMEM((seq, Q_BLOCK), I32),
            pltpu.VMEM((WORD_BITS, seq // WORD_BITS, Q_BLOCK), I32),
            pltpu.VMEM((IDX_DIM, cols), BF16),
            pltpu.VMEM((ATT_KV_WIDTH, cols), BF16),
            pltpu.VMEM((tk, cols), BF16),
            pltpu.VMEM((ATT_KV_WIDTH, cols), F32),
        ],
        compiler_params=pltpu.CompilerParams(
            dimension_semantics=("arbitrary", "arbitrary"), vmem_limit_bytes=VMEM_LIMIT_BYTES),
        name="dsa_attention",
    )(aq_t, iq_t, small, ak, av_t, ik)


def _mlstm_body(mqk_ref, mv_ref, og_ref, small_ref, cw_ref, cb_ref, gb_ref, gn_ref, y_ref,
                ubuf, c_scr, n_scr, m_scr, *, chunk):
    ln = chunk
    d = MLSTM_HEAD_DIM
    halo = SUBLANES

    @pl.when(pl.program_id(1) == 0)
    def _():
        ubuf[0:halo, :] = jnp.zeros((halo, 2 * MLSTM_WIDTH), F32)
        c_scr[...] = jnp.zeros_like(c_scr)
        n_scr[...] = jnp.zeros_like(n_scr)
        m_scr[...] = jnp.zeros_like(m_scr)

    ubuf[halo:halo + ln, :] = mqk_ref[...]
    conv = cb_ref[...]
    for j in range(CONV_WIDTH):
        off = halo - (CONV_WIDTH - 1) + j
        conv = conv + cw_ref[j:j + 1, :] * ubuf[off:off + ln, :]
    ubuf[0:halo, :] = ubuf[ln:ln + halo, :]
    qk = conv * _sigmoid(conv)
    q_all = qk[:, :MLSTM_WIDTH]
    k_all = qk[:, MLSTM_WIDTH:] * (d ** -0.5)

    sm = small_ref[...] + gb_ref[...]
    lf = jnp.minimum(sm, 0.0) - jnp.log(1.0 + jnp.exp(-jnp.abs(sm)))
    sm_t = sm.T
    lf_t = lf.T
    rr = lax.broadcasted_iota(I32, (ln, ln), 0)
    cc = lax.broadcasted_iota(I32, (ln, ln), 1)
    causal = cc <= rr
    tri = jnp.where(causal, 1.0, 0.0).astype(BF16)
    tri_t = jnp.where(rr <= cc, 1.0, 0.0).astype(BF16)
    b_cols = _dot_exact_lhs(tri, lf)
    b_rows = _dot_exact_rhs(lf_t, tri_t)

    for h in range(MLSTM_HEADS):
        hs = slice(h * d, (h + 1) * d)
        ig_col = sm[:, SMALL_MI + h:SMALL_MI + h + 1]
        ig_row = sm_t[SMALL_MI + h:SMALL_MI + h + 1, :]
        b_col = b_cols[:, SMALL_MF + h:SMALL_MF + h + 1]
        b_row = b_rows[SMALL_MF + h:SMALL_MF + h + 1, :]
        m_st = m_scr[h:h + 1, 0:1]
        c_st = c_scr[h]
        n_st = n_scr[h:h + 1, :]

        dmat = jnp.where(causal, b_col - b_row + ig_row, NEG_INF)
        inter = b_col + m_st
        m_t = jnp.maximum(inter, jnp.max(dmat, axis=-1, keepdims=True))
        w_intra = jnp.exp(dmat - m_t)
        w_inter = jnp.exp(inter - m_t)
        qf = q_all[:, hs]
        kf = k_all[:, hs]
        qh = qf.astype(BF16)
        vh = mv_ref[:, hs]
        s = _dot_nt(qh, kf.astype(BF16)) * w_intra
        num = (w_inter * jnp.dot(qh, c_st.astype(BF16), preferred_element_type=F32)
               + jnp.dot(s.astype(BF16), vh, preferred_element_type=F32))
        den = (w_inter * jnp.sum(qf * n_st, axis=-1, keepdims=True)
               + jnp.sum(s, axis=-1, keepdims=True))
        hh = num / jnp.maximum(jnp.abs(den), jnp.exp(-m_t))

        b_last = b_col[ln - 1:ln, :]
        gdec = b_last - b_col + ig_col
        m_new = jnp.maximum(b_last + m_st, jnp.max(gdec, axis=0, keepdims=True))
        decay = jnp.exp(b_last + m_st - m_new)
        kw = kf * jnp.exp(gdec - m_new)
        c_scr[h] = decay * c_st + jnp.dot(kw.T.astype(BF16), vh, preferred_element_type=F32)
        n_scr[h:h + 1, :] = decay * n_st + jnp.sum(kw, axis=0, keepdims=True)
        m_scr[h:h + 1, :] = jnp.broadcast_to(m_new, (1, LANES))

        ms = jnp.mean(hh * hh, axis=-1, keepdims=True)
        yh = hh * lax.rsqrt(ms + EPS) * gn_ref[:, hs]
        y_ref[:, hs] = (og_ref[:, hs].astype(F32) * yh).astype(BF16)


def _mlstm(mqk, mv, og, small, w_conv, b_conv, b_igate, b_fgate, g_norm_out, bsz, seq):
    ln = min(MLSTM_CHUNK, seq)
    nc = seq // ln
    gbias = jnp.zeros((1, LANES), F32)
    gbias = gbias.at[0, SMALL_MI:SMALL_MI + MLSTM_HEADS].set(b_igate.astype(F32))
    gbias = gbias.at[0, SMALL_MF:SMALL_MF + MLSTM_HEADS].set(b_fgate.astype(F32))

    def row(width):
        return pl.BlockSpec((ln, width), lambda b, c: (b * nc + c, 0))

    def const(shape):
        return pl.BlockSpec(shape, lambda b, c: (0, 0))

    return pl.pallas_call(
        functools.partial(_mlstm_body, chunk=ln),
        grid=(bsz, nc),
        in_specs=[row(2 * MLSTM_WIDTH), row(MLSTM_WIDTH), row(MLSTM_WIDTH), row(LANES),
                  const((CONV_WIDTH, 2 * MLSTM_WIDTH)), const((1, 2 * MLSTM_WIDTH)),
                  const((1, LANES)), const((1, MLSTM_WIDTH))],
        out_specs=row(MLSTM_WIDTH),
        out_shape=jax.ShapeDtypeStruct((bsz * seq, MLSTM_WIDTH), BF16),
        scratch_shapes=[
            pltpu.VMEM((ln + SUBLANES, 2 * MLSTM_WIDTH), F32),
            pltpu.VMEM((MLSTM_HEADS, MLSTM_HEAD_DIM, MLSTM_HEAD_DIM), F32),
            pltpu.VMEM((SUBLANES, MLSTM_HEAD_DIM), F32),
            pltpu.VMEM((SUBLANES, LANES), F32),
        ],
        compiler_params=pltpu.CompilerParams(
            dimension_semantics=("arbitrary", "arbitrary"), vmem_limit_bytes=VMEM_LIMIT_BYTES),
        name="mlstm",
    )(mqk, mv, og, small, w_conv.astype(F32), b_conv.astype(F32)[None, :], gbias,
      g_norm_out.astype(F32)[None, :])


def _merge_body(x_ref, ya_ref, ym_ref, ga_ref, gm_ref, wpa_ref, wpm_ref, wo_ref, gmoe_ref,
                wrh_ref, wrl_ref, br_ref, x1_ref, xn_ref, ri_ref, rf_ref):
    pa = jnp.dot(ya_ref[...], wpa_ref[...], preferred_element_type=F32)
    pm = jnp.dot(ym_ref[...], wpm_ref[...], preferred_element_type=F32)
    merged = (_sigmoid(ga_ref[...].astype(F32)) * pa
              + _sigmoid(gm_ref[...].astype(F32)) * pm)
    x1 = x_ref[...] + jnp.dot(merged.astype(BF16), wo_ref[...], preferred_element_type=F32)
    x1_ref[...] = x1
    ms = jnp.mean(x1 * x1, axis=-1, keepdims=True)
    xn = x1 * lax.rsqrt(ms + EPS) * gmoe_ref[...]
    xn_ref[...] = xn

    xh = xn.astype(BF16)
    xl = (xn - xh.astype(F32)).astype(BF16)
    logits = (jnp.dot(xh, wrh_ref[...], preferred_element_type=F32)
              + jnp.dot(xh, wrl_ref[...], preferred_element_type=F32)
              + jnp.dot(xl, wrh_ref[...], preferred_element_type=F32)) + br_ref[...]
    lane = lax.broadcasted_iota(I32, logits.shape, 1).astype(F32)

    def first_argmax(vals, mask):
        v = jnp.where(mask, vals, -jnp.inf)
        top = jnp.max(v, axis=-1, keepdims=True)
        idx = jnp.min(jnp.where(mask & (v == top), lane, float(LANES)), axis=-1, keepdims=True)
        return top, idx

    is_group = lane < N_GROUPS
    g_max, g_sel = first_argmax(logits, is_group)
    g_den = jnp.sum(jnp.where(is_group, jnp.exp(logits - g_max), 0.0), axis=-1, keepdims=True)
    g_top = 1.0 / g_den
    lo = N_GROUPS + g_sel * EXPERTS_PER_GROUP
    in_group = (lane >= lo) & (lane < lo + EXPERTS_PER_GROUP)
    v1, i1 = first_argmax(logits, in_group)
    v2, i2 = first_argmax(logits, in_group & (lane != i1))
    e2 = jnp.exp(v2 - v1)
    p1 = 1.0 / (1.0 + e2)
    gate1 = p1 * g_top
    gate2 = (e2 * p1) * g_top
    ri_ref[...] = jnp.where(lane == 0, i1 - N_GROUPS,
                            jnp.where(lane == 1, i2 - N_GROUPS, 0.0)).astype(I32)
    rf_ref[...] = jnp.where(lane == 0, gate1, jnp.where(lane == 1, gate2, 0.0))


def _merge(x2, y_att, y_ml, ga, gm, w_proj_att, w_proj_mlstm, w_out, g_norm_moe,
           w_router_group, b_router_group, w_router_expert, b_router_expert):
    t_rows = x2.shape[0]
    tm = min(MERGE_ROWS, t_rows)
    n_r = N_GROUPS + N_EXPERTS
    wr = jnp.concatenate([w_router_group, w_router_expert,
                          jnp.zeros((D_MODEL, LANES - n_r), F32)], axis=-1).astype(F32)
    wr_hi = wr.astype(BF16)
    wr_lo = (wr - wr_hi.astype(F32)).astype(BF16)
    br =jnp.concatenate([b_router_group, b_router_expert,
                          jnp.zeros((LANES - n_r,), F32)]).astype(F32)[None, :]

    def row(width):
        return pl.BlockSpec((tm, width), lambda i: (i, 0))

    def const(shape):
        return pl.BlockSpec(shape, lambda i: (0, 0))

    out_shapes = (
        jax.ShapeDtypeStruct((t_rows, D_MODEL), F32),
        jax.ShapeDtypeStruct((t_rows, D_MODEL), F32),
        jax.ShapeDtypeStruct((t_rows, LANES), I32),
        jax.ShapeDtypeStruct((t_rows, LANES), F32),
    )
    return pl.pallas_call(
        _merge_body,
        grid=(t_rows // tm,),
        in_specs=[row(D_MODEL), row(ATT_Q_WIDTH), row(MLSTM_WIDTH), row(D_MODEL), row(D_MODEL),
                  const((ATT_Q_WIDTH, D_MODEL)), const((MLSTM_WIDTH, D_MODEL)),
                  const((D_MODEL, D_MODEL)), const((1, D_MODEL)), const((D_MODEL, LANES)),
                  const((D_MODEL, LANES)), const((1, LANES))],
        out_specs=[row(s.shape[1]) for s in out_shapes],
        out_shape=out_shapes,
        compiler_params=pltpu.CompilerParams(
            dimension_semantics=("arbitrary",), vmem_limit_bytes=VMEM_LIMIT_BYTES),
        name="merge_route",
    )(x2, y_att, y_ml, ga, gm, w_proj_att.astype(BF16), w_proj_mlstm.astype(BF16),
      w_out.astype(BF16), g_norm_moe.astype(F32)[None, :], wr_hi, wr_lo, br)


def _expert_body(be_ref, tok_cur_ref, tok_nxt_ref, xn_hbm, wg_ref, wu_ref, wd_ref,
                 ys_ref, xbuf, wg_bf, wu_bf, wd_bf, sem, *, rows):
    i = pl.program_id(0)
    nblk = pl.num_programs(0)
    slot = lax.rem(i, 2)
    nslot = 1 - slot

    @pl.when((i == 0) | (be_ref[i] != be_ref[jnp.maximum(i - 1, 0)]))
    def _():
        wg_bf[...] = wg_ref[0].astype(BF16)
        wu_bf[...] = wu_ref[0].astype(BF16)
        wd_bf[...] = wd_ref[0].astype(BF16)

    def row_copy(tok, dst_slot, r):
        return pltpu.make_async_copy(
            xn_hbm.at[pl.ds(tok, 1), :], xbuf.at[dst_slot, pl.ds(r, 1), :], sem.at[dst_slot])

    def all_rows(dst_slot):
        return pltpu.make_async_copy(
            xn_hbm.at[pl.ds(0, rows), :], xbuf.at[dst_slot], sem.at[dst_slot])

    @pl.when(i == 0)
    def _():
        for r in range(rows):
            row_copy(tok_cur_ref[0, 0, r], 0, r).start(priority=r % DMA_PRIORITIES)

    all_rows(slot).wait()

    def prefetch(part):
        for r in range(part * rows // 4, (part + 1) * rows // 4):
            row_copy(tok_nxt_ref[0, 0, r], nslot, r).start(priority=r % DMA_PRIORITIES)

    prefetch(0)
    xb = xbuf[slot].astype(BF16)
    a = jnp.dot(xb, wg_bf[...], preferred_element_type=F32)
    prefetch(1)
    u = jnp.dot(xb, wu_bf[...], preferred_element_type=F32)
    prefetch(2)
    hid = (a * _sigmoid(a) * u).astype(BF16)
    prefetch(3)
    ys_ref[...] = jnp.dot(hid, wd_bf[...], preferred_element_type=F32)

    @pl.when(i == nblk - 1)
    def _():
        all_rows(nslot).wait()


def _experts(xn, buf_tok, block_e, w_gate, w_up, w_down):
    rows = EXPERT_ROWS
    cap = buf_tok.shape[0]
    nblk = cap // rows
    tok3 = buf_tok.reshape(nblk, 1, rows)
    grid_spec = pltpu.PrefetchScalarGridSpec(
        num_scalar_prefetch=1,
        grid=(nblk,),
        in_specs=[
            pl.BlockSpec((1, 1, rows), lambda i, be: (i, 0, 0), memory_space=pltpu.SMEM),
            pl.BlockSpec((1, 1, rows), lambda i, be: (jnp.minimum(i + 1, nblk - 1), 0, 0),
                         memory_space=pltpu.SMEM),
            pl.BlockSpec(memory_space=pl.ANY),
            pl.BlockSpec((1, D_MODEL, D_EXPERT), lambda i, be: (be[i], 0, 0)),
            pl.BlockSpec((1, D_MODEL, D_EXPERT), lambda i, be: (be[i], 0, 0)),
            pl.BlockSpec((1, D_EXPERT, D_MODEL), lambda i, be: (be[i], 0, 0)),
        ],
        out_specs=pl.BlockSpec((rows, D_MODEL), lambda i, be: (i, 0)),
        scratch_shapes=[pltpu.VMEM((2, rows, D_MODEL), F32),
                        pltpu.VMEM((D_MODEL, D_EXPERT), BF16),
                        pltpu.VMEM((D_MODEL, D_EXPERT), BF16),
                        pltpu.VMEM((D_EXPERT, D_MODEL), BF16),
                        pltpu.SemaphoreType.DMA((2,))],
    )
    return pl.pallas_call(
        functools.partial(_expert_body, rows=rows),
        grid_spec=grid_spec,
        out_shape=jax.ShapeDtypeStruct((cap, D_MODEL), F32),
        compiler_params=pltpu.CompilerParams(
            dimension_semantics=("arbitrary",), vmem_limit_bytes=VMEM_LIMIT_BYTES),
        name="moe_experts",
    )(block_e, tok3, tok3, xn, w_gate.astype(F32), w_up.astype(F32), w_down.astype(F32))


def _combine_body(dcur_ref, dnxt_ref, x1_ref, rf_ref, ys_hbm, out_ref, ybuf, sem, *, rows):
    i = pl.program_id(0)
    nt = pl.num_programs(0)
    slot = lax.rem(i, 2)
    nslot = 1 - slot

    def row_copy(src, dst_slot, k, r):
        return pltpu.make_async_copy(
            ys_hbm.at[pl.ds(src, 1), :], ybuf.at[dst_slot, k, pl.ds(r, 1), :], sem.at[dst_slot])

    def wait_all(dst_slot):
        for k in range(TOP_K_EXPERTS):
            pltpu.make_async_copy(
                ys_hbm.at[pl.ds(0, rows), :], ybuf.at[dst_slot, k], sem.at[dst_slot]).wait()

    def gather(dest_ref, dst_slot):
        for r in range(rows):
            for k in range(TOP_K_EXPERTS):
                row_copy(dest_ref[0, 0, TOP_K_EXPERTS * r + k], dst_slot, k, r).start(
                    priority=k % DMA_PRIORITIES)

    @pl.when(i == 0)
    def _():
        gather(dcur_ref, 0)

    wait_all(slot)
    gather(dnxt_ref, nslot)
    gates = rf_ref[...]
    out = x1_ref[...]
    for k in range(TOP_K_EXPERTS):
        out = out + gates[:, k:k + 1] * ybuf[slot, k]
    out_ref[...] = out

    @pl.when(i == nt - 1)
    def _():
        wait_all(nslot)


def _combine(x1, rf, ys, dest):
    t_rows = x1.shape[0]
    rows = min(COMBINE_ROWS, t_rows)
    nt = t_rows // rows
    dest3 = dest.reshape(nt, 1, rows * TOP_K_EXPERTS)
    dspec = functools.partial(pl.BlockSpec, (1, 1, rows * TOP_K_EXPERTS), memory_space=pltpu.SMEM)
    return pl.pallas_call(
        functools.partial(_combine_body, rows=rows),
        grid=(nt,),
        in_specs=[
            dspec(lambda i: (i, 0, 0)),
            dspec(lambda i: (jnp.minimum(i + 1, nt - 1), 0, 0)),
            pl.BlockSpec((rows, D_MODEL), lambda i: (i, 0)),
            pl.BlockSpec((rows, LANES), lambda i: (i, 0)),
            pl.BlockSpec(memory_space=pl.ANY),
        ],
        out_specs=pl.BlockSpec((rows, D_MODEL), lambda i: (i, 0)),
        out_shape=jax.ShapeDtypeStruct((t_rows, D_MODEL), F32),
        scratch_shapes=[pltpu.VMEM((2, TOP_K_EXPERTS, rows, D_MODEL), F32),
                        pltpu.SemaphoreType.DMA((2,))],
        compiler_params=pltpu.CompilerParams(
            dimension_semantics=("arbitrary",), vmem_limit_bytes=VMEM_LIMIT_BYTES),
        name="moe_combine",
    )(dest3, dest3, x1, rf, ys)


def _route_plan(expert_id):
    t_rows = expert_id.shape[0]
    m = t_rows * TOP_K_EXPERTS
    rows = EXPERT_ROWS
    e_flat = expert_id.reshape(m)
    onehot = (e_flat[:, None] == jnp.arange(N_EXPERTS, dtype=I32)[None, :]).astype(I32)
    running = jnp.cumsum(onehot, axis=0)
    counts = running[-1]
    rank = jnp.sum(onehot * running, axis=1) - 1
    padded = ((counts + rows - 1) // rows) * rows
    pends = jnp.cumsum(padded)
    pstarts = pends - padded
    cap = ((m + N_EXPERTS * rows + rows - 1) // rows) * rows
    nblk = cap // rows
    blk_start = jnp.arange(nblk, dtype=I32) * rows
    block_e = jnp.minimum(jnp.sum((pends[None, :] <= blk_start[:, None]).astype(I32), axis=1),
                          N_EXPERTS - 1)
    dest = (jnp.sum(onehot * pstarts[None, :], axis=1) + rank).astype(I32)
    order = jnp.argsort(e_flat, stable=True).astype(I32)
    starts = jnp.cumsum(counts) - counts
    slot = jnp.arange(cap, dtype=I32)
    off = slot - jnp.repeat(pstarts[block_e], rows)
    live = (off < jnp.repeat(counts[block_e], rows)) & (slot < pends[-1])
    src = jnp.clip(jnp.repeat(starts[block_e], rows) + off, 0, m - 1)
    buf_tok = jnp.where(live, order[src] // TOP_K_EXPERTS, slot % t_rows).astype(I32)
    return buf_tok, block_e, dest


def kernel(x, positions, g_norm_mix, w_in, g_q_att, g_k_att, w_conv_mlstm, b_conv_mlstm, b_igate,
           b_fgate, g_norm_mlstm_out, w_proj_att, w_proj_mlstm, w_out, g_norm_moe, w_router_group,
           b_router_group, w_router_expert, b_router_expert, w_exp_gate, w_exp_up, w_exp_down):
    bsz, seq, dm = x.shape
    depth = g_norm_mix.shape[0]
    x2 = x.reshape(bsz * seq, dm)
    pos2 = positions.reshape(bsz * seq, 1).astype(I32)
    for l in range(depth):
        aq, ak, av, iq, ik, small, mqk, mv, og, ga, gm = _in_proj(
            x2, pos2, g_norm_mix[l], w_in[l], g_q_att[l], g_k_att[l])
        y_att = _dsa(aq, iq, small, ak, av, ik, bsz, seq)
        y_ml = _mlstm(mqk, mv, og, small, w_conv_mlstm[l], b_conv_mlstm[l], b_igate[l], b_fgate[l],
                      g_norm_mlstm_out[l], bsz, seq)
        x1, xn, ri, rf = _merge(x2, y_att, y_ml, ga, gm, w_proj_att[l], w_proj_mlstm[l], w_out[l],
                                g_norm_moe[l], w_router_group[l], b_router_group[l],
                                w_router_expert[l], b_router_expert[l])
        buf_tok, block_e, dest = _route_plan(ri[:, :TOP_K_EXPERTS])
        ys = _experts(xn, buf_tok, block_e, w_exp_gate[l], w_exp_up[l], w_exp_down[l])
        x2 = _combine(x1, rf, ys, dest)
    return x2.reshape(bsz, seq, dm)
```

```python
import functools

import numpy as np
import jax
import jax.numpy as jnp
from jax import lax
from jax.experimental import pallas as pl
from jax.experimental.pallas import tpu as pltpu

F32 = jnp.float32
BF16 = jnp.bfloat16
I32 = jnp.int32

D_MODEL = 1024
CHUNK = 64
Q_BLOCK = 128
EPS = 1e-6
NEG_INF = -1e30

ATT_HEADS = 8
ATT_KV_HEADS = 2
ATT_HEAD_DIM = 64
ATT_Q_WIDTH = ATT_HEADS * ATT_HEAD_DIM
ATT_KV_WIDTH = ATT_KV_HEADS * ATT_HEAD_DIM
IDX_HEADS = 8
IDX_DIM = 32
TOPK_MAX = 256
ROPE_THETA = 500000.0
ROPE_FRACTION = 4

MLSTM_HEADS = 4
MLSTM_HEAD_DIM = 128
MLSTM_WIDTH = MLSTM_HEADS * MLSTM_HEAD_DIM
CONV_WIDTH = 4

N_GROUPS = 4
EXPERTS_PER_GROUP = 8
N_EXPERTS = N_GROUPS * EXPERTS_PER_GROUP
TOP_K_EXPERTS = 2
D_EXPERT = 512

LANES = 128
SUBLANES = 8
VMEM_LIMIT_BYTES = 56 * 1024 * 1024

SMALL_IK = 0
SMALL_IW = IDX_DIM
SMALL_MI = SMALL_IW + IDX_HEADS
SMALL_MF = SMALL_MI + MLSTM_HEADS

COL_AQ = 0
COL_AK = COL_AQ + ATT_Q_WIDTH
COL_AV = COL_AK + ATT_KV_WIDTH
COL_IQ = COL_AV + ATT_KV_WIDTH
COL_SMALL = COL_IQ + IDX_HEADS * IDX_DIM
COL_MQK = COL_SMALL + LANES
COL_MV = COL_MQK + 2 * MLSTM_WIDTH
COL_MO = COL_MV + MLSTM_WIDTH
COL_GA = COL_MO + MLSTM_WIDTH
COL_GM = COL_GA + D_MODEL
COL_END = COL_GM + D_MODEL

KEY_TILE = 512
MLSTM_CHUNK = 256
MERGE_ROWS = 512
EXPERT_ROWS = 256
COMBINE_ROWS = 512
WORD_BITS = 32
PLANE_GROUP = WORD_BITS * SUBLANES
VT_ROWS = ATT_KV_WIDTH + 16

DMA_PRIORITIES = 2

LOG2_E = 1.4426950408889634
_INT_MIN = -(2 ** 31)


def _sortable_key_const(v):
    b = int(np.array(v, np.float32).view(np.int32))
    return b if b >= 0 else b ^ 0x7FFFFFFF


_KEY_NEG_INF = _sortable_key_const(NEG_INF)
_KEY_VALID = _sortable_key_const(NEG_INF * 0.5)


def _split3(a):
    hi = a.astype(BF16)
    r1 = a - hi.astype(F32)
    mid = r1.astype(BF16)
    lo = (r1 - mid.astype(F32)).astype(BF16)
    return hi, mid, lo


def _dot_exact_rhs(a, b_bf16):
    hi, mid, lo = _split3(a)
    d = functools.partial(jnp.dot, preferred_element_type=F32)
    return d(hi, b_bf16) + d(mid, b_bf16) + d(lo, b_bf16)


def _dot_exact_lhs(a_bf16, b):
    hi, mid, lo = _split3(b)
    d = functools.partial(jnp.dot, preferred_element_type=F32)
    return d(a_bf16, hi) + d(a_bf16, mid) + d(a_bf16, lo)


def _dot_nt(a, b):
    return lax.dot_general(a, b, (((1,), (1,)), ((), ())), preferred_element_type=F32)


def _sigmoid(x):
    return 0.5 * jnp.tanh(0.5 * x) + 0.5


def _in_proj_body(x_ref, pos_ref, g_ref, w_ref, rc_ref, bd_ref, gq_ref, gk_ref,
                  aq_ref, ak_ref, av_ref, iq_ref, ik_ref, small_ref, mqk_ref, mv_ref,
                  og_ref, ga_ref, gm_ref):
    x = x_ref[...]
    ms = jnp.mean(x * x, axis=-1, keepdims=True)
    h = (x * lax.rsqrt(ms + EPS) * g_ref[...]).astype(BF16)
    posf = pos_ref[...].astype(F32)

    def proj(lo, hi):
        return jnp.dot(h, w_ref[:, lo:hi], preferred_element_type=F32)

    def rope(t, cos, s_up, s_dn, half):
        n = t.shape[-1]
        return t * cos + pltpu.roll(t, half, 1) * s_up + pltpu.roll(t, n - half, 1) * s_dn

    def head_norm(t, gain):
        ssum = jnp.dot((t * t).astype(BF16), bd_ref[...], preferred_element_type=F32)
        return t * lax.rsqrt(ssum * (1.0 / ATT_HEAD_DIM) + EPS) * gain

    ang_a = posf * rc_ref[0:1, :]
    cos_a, sin_a = jnp.cos(ang_a), jnp.sin(ang_a)
    up_a, dn_a = sin_a * rc_ref[1:2, :], sin_a * rc_ref[2:3, :]
    ang_i = posf * rc_ref[3:4, :]
    cos_i, sin_i = jnp.cos(ang_i), jnp.sin(ang_i)
    up_i, dn_i = sin_i * rc_ref[4:5, :], sin_i * rc_ref[5:6, :]
    small_mask = rc_ref[6:7, :]
    cos_s = 1.0 + (cos_i - 1.0) * small_mask
    up_s, dn_s = up_i * small_mask, dn_i * small_mask

    att_half = ATT_HEAD_DIM // ROPE_FRACTION // 2
    idx_half = IDX_DIM // ROPE_FRACTION // 2
    q_scale = ATT_HEAD_DIM ** -0.5 * LOG2_E
    tm = x.shape[0]

    def store_qblock_t(ref, c, t):
        for r in range(tm // Q_BLOCK):
            ref[r, c * LANES:(c + 1) * LANES, :] = t[r * Q_BLOCK:(r + 1) * Q_BLOCK, :].T.astype(BF16)

    for c in range(ATT_Q_WIDTH // LANES):
        t = proj(COL_AQ + c * LANES, COL_AQ + (c + 1) * LANES)
        t = rope(head_norm(t, gq_ref[...]), cos_a, up_a, dn_a, att_half)
        store_qblock_t(aq_ref, c, t * q_scale)
    t = proj(COL_AK, COL_AK + LANES)
    ak_ref[...] = rope(head_norm(t, gk_ref[...]), cos_a, up_a, dn_a, att_half).astype(BF16)
    av_ref[0, 0:ATT_KV_WIDTH, :] = proj(COL_AV, COL_AV + LANES).T.astype(BF16)
    av_ref[0, ATT_KV_WIDTH:VT_ROWS, :] = jnp.ones((VT_ROWS - ATT_KV_WIDTH, tm), BF16)
    for c in range(IDX_HEADS * IDX_DIM // LANES):
        t = proj(COL_IQ + c * LANES, COL_IQ + (c + 1) * LANES)
        store_qblock_t(iq_ref, c, rope(t, cos_i, up_i, dn_i, idx_half))
    t = rope(proj(COL_SMALL, COL_SMALL + LANES), cos_s, up_s, dn_s, idx_half)
    small_ref[...] = t
    ik_ref[...] = t[:, SMALL_IK:SMALL_IK + IDX_DIM].astype(BF16)
    for c in range(2 * MLSTM_WIDTH // 512):
        mqk_ref[:, c * 512:(c + 1) * 512] = proj(COL_MQK + c * 512, COL_MQK + (c + 1) * 512)
    mv_ref[...] = proj(COL_MV, COL_MV + MLSTM_WIDTH).astype(BF16)
    og_ref[...] = _sigmoid(proj(COL_MO, COL_MO + MLSTM_WIDTH)).astype(BF16)
    for c in range(D_MODEL // 512):
        ga_ref[:, c * 512:(c + 1) * 512] = proj(
            COL_GA + c * 512, COL_GA + (c + 1) * 512).astype(BF16)
        gm_ref[:, c * 512:(c + 1) * 512] = proj(
            COL_GM + c * 512, COL_GM + (c + 1) * 512).astype(BF16)


def _rope_consts():
    lane = np.arange(LANES)

    def inv_freq(rot):
        half = rot // 2
        return jnp.float32(ROPE_THETA) ** (-jnp.arange(half, dtype=F32) * 2.0 / rot)

    rot_a = ATT_HEAD_DIM // ROPE_FRACTION
    rot_i = IDX_DIM // ROPE_FRACTION
    fa, fi = inv_freq(rot_a), inv_freq(rot_i)
    ja, ji = lane % ATT_HEAD_DIM, lane % IDX_DIM
    rc = jnp.zeros((16, LANES), F32)
    rc = rc.at[0].set(jnp.where(ja < rot_a, fa[ja % (rot_a // 2)], 0.0))
    rc = rc.at[1].set(jnp.asarray(((ja >= rot_a // 2) & (ja < rot_a)).astype(np.float32)))
    rc = rc.at[2].set(jnp.asarray(-(ja < rot_a // 2).astype(np.float32)))
    rc = rc.at[3].set(jnp.where(ji < rot_i, fi[ji % (rot_i // 2)], 0.0))
    rc = rc.at[4].set(jnp.asarray(((ji >= rot_i // 2) & (ji < rot_i)).astype(np.float32)))
    rc = rc.at[5].set(jnp.asarray(-(ji < rot_i // 2).astype(np.float32)))
    rc = rc.at[6].set(jnp.asarray((lane < IDX_DIM).astype(np.float32)))
    return rc


def _pack_w_in(w_in):
    sizes = (ATT_Q_WIDTH, ATT_KV_WIDTH, ATT_KV_WIDTH, IDX_HEADS * IDX_DIM, IDX_DIM, IDX_HEADS,
             MLSTM_WIDTH, MLSTM_WIDTH, MLSTM_WIDTH, MLSTM_WIDTH, MLSTM_HEADS, MLSTM_HEADS,
             D_MODEL, D_MODEL)
    pts = np.cumsum(sizes)[:-1].tolist()
    (aq, ak, av, iq, ik, iw, mq, mk, mv, mo, mi, mf, ga, gm) = jnp.split(w_in, pts, axis=-1)
    pad = jnp.zeros((D_MODEL, LANES - (SMALL_MF + MLSTM_HEADS)), w_in.dtype)
    small = jnp.concatenate([ik, iw, mi, mf, pad], axis=-1)
    return jnp.concatenate([aq, ak, av, iq, small, mq, mk, mv, mo, ga, gm], axis=-1).astype(BF16)


def _in_proj(x2, pos2, g_norm_mix, w_in, g_q_att, g_k_att):
    t_rows = x2.shape[0]
    tm = KEY_TILE
    w = _pack_w_in(w_in)
    lane = np.arange(LANES)
    bd = jnp.asarray((lane[:, None] // ATT_HEAD_DIM == lane[None, :] // ATT_HEAD_DIM)
                     .astype(np.float32)).astype(BF16)
    reps = LANES // ATT_HEAD_DIM
    gq = jnp.tile(g_q_att.astype(F32), reps)[None, :]
    gk = jnp.tile(g_k_att.astype(F32), reps)[None, :]

    def row(width):
        return pl.BlockSpec((tm, width), lambda i: (i, 0))

    def const(shape):
        return pl.BlockSpec(shape, lambda i: (0, 0))

    qb = tm // Q_BLOCK

    def qblock_t(width):
        return pl.BlockSpec((qb, width, Q_BLOCK), lambda i: (i, 0, 0))

    out_shapes = (
        jax.ShapeDtypeStruct((t_rows // Q_BLOCK, ATT_Q_WIDTH, Q_BLOCK), BF16),
        jax.ShapeDtypeStruct((t_rows, ATT_KV_WIDTH), BF16),
        jax.ShapeDtypeStruct((t_rows // tm, VT_ROWS, tm), BF16),
        jax.ShapeDtypeStruct((t_rows // Q_BLOCK, IDX_HEADS * IDX_DIM, Q_BLOCK), BF16),
        jax.ShapeDtypeStruct((t_rows, IDX_DIM), BF16),
        jax.ShapeDtypeStruct((t_rows, LANES), F32),
        jax.ShapeDtypeStruct((t_rows, 2 * MLSTM_WIDTH), F32),
        jax.ShapeDtypeStruct((t_rows, MLSTM_WIDTH), BF16),
        jax.ShapeDtypeStruct((t_rows, MLSTM_WIDTH), BF16),
        jax.ShapeDtypeStruct((t_rows, D_MODEL), BF16),
        jax.ShapeDtypeStruct((t_rows, D_MODEL), BF16),
    )
    return pl.pallas_call(
        _in_proj_body,
        grid=(t_rows // tm,),
        in_specs=[row(D_MODEL), row(1), const((1, D_MODEL)), const((D_MODEL, COL_END)),
                  const((16, LANES)), const((LANES, LANES)), const((1, LANES)), const((1, LANES))],
        out_specs=[qblock_t(ATT_Q_WIDTH), row(ATT_KV_WIDTH),
                   pl.BlockSpec((1, VT_ROWS, tm), lambda i: (i, 0, 0)),
                   qblock_t(IDX_HEADS * IDX_DIM)] + [row(s.shape[1]) for s in out_shapes[4:]],
        out_shape=out_shapes,
        compiler_params=pltpu.CompilerParams(
            dimension_semantics=("arbitrary",), vmem_limit_bytes=VMEM_LIMIT_BYTES),
        name="in_proj",
    )(x2, pos2, g_norm_mix.astype(F32)[None, :], w, _rope_consts(), bd, gq, gk)


def _bit_transpose32(words):
    a = list(words)
    j, m = 16, 0x0000FFFF
    while j:
        k = 0
        while k < WORD_BITS:
            t = (a[k] ^ lax.shift_right_logical(a[k + j], jnp.int32(j))) & m
            a[k] = a[k] ^ t
            a[k + j] = a[k + j] ^ lax.shift_left(t, jnp.int32(j))
            k = (k + j + 1) & ~j
        j >>= 1
        m = m ^ (m << j)
    return a


def _dsa_body(aq_ref, iq_ref, small_ref, k_ref, vt_ref, ik_ref, out_ref,
              key_scr, plane_scr, qi_scr, q2_scr, p_scr, acc_scr, *, topk, tk):
    blk = pl.program_id(1)
    nq = Q_BLOCK
    rep = ATT_HEADS // ATT_KV_HEADS
    groups_per_tile = tk // PLANE_GROUP
    n_tiles = lax.shift_right_logical(blk * nq + nq + tk - 1, int(np.log2(tk)))
    idx_scale = (IDX_DIM * IDX_HEADS) ** -0.5
    plane_rows = plane_scr.shape[1]

    @pl.when((pl.program_id(0) == 0) & (blk == 0))
    def _():
        plane_scr[...] = jnp.zeros_like(plane_scr)

    for h in range(IDX_HEADS):
        qi_scr[:, h * nq:(h + 1) * nq] = iq_ref[0, h * IDX_DIM:(h + 1) * IDX_DIM, :]
    w_rows = small_ref[...].T[SMALL_IW:SMALL_IW + IDX_HEADS, :] * idx_scale

    q_chunk = lax.shift_right_logical(
        blk * nq + lax.broadcasted_iota(I32, (1, nq), 1), int(np.log2(CHUNK)))
    key_row = lax.broadcasted_iota(I32, (tk, 1), 0)

    def score_tile(t, carry):
        start = pl.multiple_of(t * tk, tk)
        rel = jnp.dot(ik_ref[pl.ds(start, tk), :], qi_scr[...], preferred_element_type=F32)
        sc = jnp.maximum(rel[:, 0:nq], 0.0) * w_rows[0:1]
        for h in range(1, IDX_HEADS):
            sc = sc + jnp.maximum(rel[:, h * nq:(h + 1) * nq], 0.0) * w_rows[h:h + 1]
        visible = lax.shift_right_logical(start + key_row, int(np.log2(CHUNK))) <= q_chunk
        sc = jnp.where(visible, sc, NEG_INF)
        bits = lax.bitcast_convert_type(sc, I32)
        key = jnp.where(bits < 0, bits ^ 0x7FFFFFFF, bits)
        key_scr[pl.ds(start, tk), :] = key
        biased = key ^ _INT_MIN
        for g in range(groups_per_tile):
            words = [biased[g * PLANE_GROUP + i * SUBLANES:g * PLANE_GROUP + (i + 1) * SUBLANES, :]
                     for i in range(WORD_BITS)]
            planes = _bit_transpose32(words)
            row0 = pl.multiple_of((t * groups_per_tile + g) * SUBLANES, SUBLANES)
            for p in range(WORD_BITS):
                plane_scr[p, pl.ds(row0, SUBLANES), :] = planes[p]
        return carry

    lax.fori_loop(0, n_tiles, score_tile, 0)

    group_of_row = lax.shift_right_logical(
        lax.broadcasted_iota(I32, (plane_rows, nq), 0), int(np.log2(SUBLANES)))
    alive = jnp.where(group_of_row < n_tiles * groups_per_tile, -1, 0)
    k_rem = jnp.full((1, nq), float(topk), F32)
    thr_u = jnp.zeros((1, nq), I32)

    def lane_count(mask_words):
        pc = lax.population_count(mask_words)
        part = jnp.sum(pc.reshape(plane_rows // SUBLANES, SUBLANES, nq), axis=0)
        return jnp.sum(part.astype(F32), axis=0, keepdims=True)

    for p in range(WORD_BITS):
        bit_value = _INT_MIN if p == 0 else 1 << (WORD_BITS - 1 - p)
        ones = alive & plane_scr[p]
        c_ones = lane_count(ones)
        take = c_ones >= k_rem
        alive = jnp.where(take, ones, alive ^ ones)
        k_rem = jnp.where(take, k_rem, k_rem - c_ones)
        thr_u = jnp.where(take, thr_u | bit_value, thr_u)
    thr = thr_u ^ _INT_MIN
    quota = k_rem

    q2_scr[...] = jnp.zeros_like(q2_scr)
    for h in range(ATT_HEADS):
        g = h // rep
        q2_scr[g * ATT_HEAD_DIM:(g + 1) * ATT_HEAD_DIM, h * nq:(h + 1) * nq] = (
            aq_ref[0, h * ATT_HEAD_DIM:(h + 1) * ATT_HEAD_DIM, :])

    def attend(with_ties):
        acc_scr[...] = jnp.zeros_like(acc_scr)
        if with_ties:
            rr = lax.broadcasted_iota(I32, (tk, tk), 0)
            cc = lax.broadcasted_iota(I32, (tk, tk), 1)
            earlier = jnp.where(cc < rr, 1.0, 0.0).astype(BF16)

        def tile(t, carry):
            m_all, eq_seen = carry
            start = pl.multiple_of(t * tk, tk)
            key = key_scr[pl.ds(start, tk), :]
            valid = key > _KEY_VALID
            if with_ties:
                eq = key == thr
                eqf = jnp.where(eq, 1.0, 0.0)
                rank = jnp.dot(earlier, eqf.astype(BF16), preferred_element_type=F32) + eq_seen
                sel = valid & ((key > thr) | (eq & (rank < quota)))
                eq_seen = eq_seen + jnp.sum(eqf, axis=0, keepdims=True)
            else:
                sel = valid & (key >= thr)
            bias = jnp.where(sel, 0.0, NEG_INF)
            s = jnp.dot(k_ref[pl.ds(start, tk), :], q2_scr[...], preferred_element_type=F32)
            m_rows, a_rows = [], []
            for h in range(ATT_HEADS):
                cols = slice(h * nq, (h + 1) * nq)
                sh = s[:, cols] + bias
                m_old = m_all[h:h + 1]
                m_new = jnp.maximum(m_old, jnp.max(sh, axis=0, keepdims=True))
                p_scr[:, cols] = jnp.exp2(sh - m_new).astype(BF16)
                m_rows.append(m_new)
                a_rows.append(jnp.exp2(m_old - m_new))
            acc_scr[...] = (acc_scr[...] * jnp.concatenate(a_rows, axis=1)
                            + jnp.dot(vt_ref[t], p_scr[...], preferred_element_type=F32))
            return jnp.concatenate(m_rows, axis=0), eq_seen

        lax.fori_loop(0, n_tiles, tile,
                      (jnp.full((ATT_HEADS, nq), NEG_INF, F32), jnp.zeros((1, nq), F32)))
        inv_l = 1.0 / acc_scr[ATT_KV_WIDTH:ATT_KV_WIDTH + 1, :]
        for j in range(ATT_HEADS // 2):
            g = (2 * j) // rep
            halves = [acc_scr[g * ATT_HEAD_DIM:(g + 1) * ATT_HEAD_DIM, h * nq:(h + 1) * nq]
                      * inv_l[:, h * nq:(h + 1) * nq] for h in (2 * j, 2 * j + 1)]
            out_ref[:, j * LANES:(j + 1) * LANES] = jnp.concatenate(halves, axis=0).T.astype(BF16)

    tie_lanes = jnp.where((lane_count(alive) > quota) & (thr > _KEY_NEG_INF), 1.0, 0.0)
    need_ties = jnp.max(tie_lanes) > 0.5

    @pl.when(need_ties)
    def _():
        attend(True)

    @pl.when(jnp.logical_not(need_ties))
    def _():
        attend(False)


def _dsa(aq_t, iq_t, small, ak, av_t, ik, bsz, seq):
    nb = seq // Q_BLOCK
    tk = KEY_TILE
    topk = min(TOPK_MAX, seq // 4)
    assert seq % tk == 0 and tk >= topk and tk & (tk - 1) == 0 and tk % PLANE_GROUP == 0
    nkt = seq // tk

    def qspec_t(width):
        return pl.BlockSpec((1, width, Q_BLOCK), lambda b, j: (b * nb + j, 0, 0))

    def qspec(width):
        return pl.BlockSpec((Q_BLOCK, width), lambda b, j: (b * nb + j, 0))

    def kspec(width):
        return pl.BlockSpec((seq, width), lambda b, j: (b, 0))

    cols = ATT_HEADS * Q_BLOCK
    return pl.pallas_call(
        functools.partial(_dsa_body, topk=topk, tk=tk),
        grid=(bsz, nb),
        in_specs=[qspec_t(ATT_Q_WIDTH), qspec_t(IDX_HEADS * IDX_DIM), qspec(LANES),
                  kspec(ATT_KV_WIDTH),
                  pl.BlockSpec((nkt, VT_ROWS, tk), lambda b, j: (b, 0, 0)),
                  kspec(IDX_DIM)],
        out_specs=qspec(ATT_Q_WIDTH),
        out_shape=jax.ShapeDtypeStruct((bsz * seq, ATT_Q_WIDTH), BF16),
        scratch_shapes=[
            pltpu.VMEM((seq, Q_BLOCK), I32),
            pltpu.VMEM((WORD_BITS, seq // WORD_BITS, Q_BLOCK), I32),
            pltpu.VMEM((IDX_DIM, cols), BF16),
            pltpu.VMEM((ATT_KV_WIDTH, cols), BF16),
            pltpu.VMEM((tk, cols), BF16),
            pltpu.VMEM((VT_ROWS, cols), F32),
        ],
        compiler_params=pltpu.CompilerParams(
            dimension_semantics=("arbitrary", "arbitrary"), vmem_limit_bytes=VMEM_LIMIT_BYTES),
        name="dsa_attention",
    )(aq_t, iq_t, small, ak, av_t, ik)


def _mlstm_body(mqk_ref, mv_ref, og_ref, small_ref, cw_ref, cb_ref, gb_ref, gn_ref, y_ref,
                ubuf, c_scr, n_scr, m_scr, *, chunk):
    ln = chunk
    d = MLSTM_HEAD_DIM
    halo = SUBLANES

    @pl.when(pl.program_id(1) == 0)
    def _():
        ubuf[0:halo, :] = jnp.zeros((halo, 2 * MLSTM_WIDTH), F32)
        c_scr[...] = jnp.zeros_like(c_scr)
        n_scr[...] = jnp.zeros_like(n_scr)
        m_scr[...] = jnp.zeros_like(m_scr)

    ubuf[halo:halo + ln, :] = mqk_ref[...]
    conv = cb_ref[...]
    for j in range(CONV_WIDTH):
        off = halo - (CONV_WIDTH - 1) + j
        conv = conv + cw_ref[j:j + 1, :] * ubuf[off:off + ln, :]
    ubuf[0:halo, :] = ubuf[ln:ln + halo, :]
    qk = conv * _sigmoid(conv)
    q_all = qk[:, :MLSTM_WIDTH]
    k_all = qk[:, MLSTM_WIDTH:] * (d ** -0.5)

    sm = small_ref[...] + gb_ref[...]
    lf = jnp.minimum(sm, 0.0) - jnp.log(1.0 + jnp.exp(-jnp.abs(sm)))
    sm_t = sm.T
    lf_t = lf.T
    rr = lax.broadcasted_iota(I32, (ln, ln), 0)
    cc = lax.broadcasted_iota(I32, (ln, ln), 1)
    causal = cc <= rr
    tri = jnp.where(causal, 1.0, 0.0).astype(BF16)
    tri_t = jnp.where(rr <= cc, 1.0, 0.0).astype(BF16)
    b_cols = _dot_exact_lhs(tri, lf)
    b_rows = _dot_exact_rhs(lf_t, tri_t)

    for h in range(MLSTM_HEADS):
        hs = slice(h * d, (h + 1) * d)
        ig_col = sm[:, SMALL_MI + h:SMALL_MI + h + 1]
        ig_row = sm_t[SMALL_MI + h:SMALL_MI + h + 1, :]
        b_col = b_cols[:, SMALL_MF + h:SMALL_MF + h + 1]
        b_row = b_rows[SMALL_MF + h:SMALL_MF + h + 1, :]
        m_st = m_scr[h:h + 1, 0:1]
        c_st = c_scr[h]
        n_st = n_scr[h:h + 1, :]

        dmat = jnp.where(causal, b_col - b_row + ig_row, NEG_INF)
        inter = b_col + m_st
        m_t = jnp.maximum(inter, jnp.max(dmat, axis=-1, keepdims=True))
        w_intra = jnp.exp(dmat - m_t)
        w_inter = jnp.exp(inter - m_t)
        qf = q_all[:, hs]
        kf = k_all[:, hs]
        qh = qf.astype(BF16)
        vh = mv_ref[:, hs]
        s = _dot_nt(qh, kf.astype(BF16)) * w_intra
        num = (w_inter * jnp.dot(qh, c_st.astype(BF16), preferred_element_type=F32)
               + jnp.dot(s.astype(BF16), vh, preferred_element_type=F32))
        den = (w_inter * jnp.sum(qf * n_st, axis=-1, keepdims=True)
               + jnp.sum(s, axis=-1, keepdims=True))
        hh = num / jnp.maximum(jnp.abs(den), jnp.exp(-m_t))

        b_last = b_col[ln - 1:ln, :]
        gdec = b_last - b_col + ig_col
        m_new = jnp.maximum(b_last + m_st, jnp.max(gdec, axis=0, keepdims=True))
        decay = jnp.exp(b_last + m_st - m_new)
        kw = kf * jnp.exp(gdec - m_new)
        c_scr[h] = decay * c_st + jnp.dot(kw.T.astype(BF16), vh, preferred_element_type=F32)
        n_scr[h:h + 1, :] = decay * n_st + jnp.sum(kw, axis=0, keepdims=True)
        m_scr[h:h + 1, :] = jnp.broadcast_to(m_new, (1, LANES))

        ms = jnp.mean(hh * hh, axis=-1, keepdims=True)
        yh = hh * lax.rsqrt(ms + EPS) * gn_ref[:, hs]
        y_ref[:, hs] = (og_ref[:, hs].astype(F32) * yh).astype(BF16)


def _mlstm(mqk, mv, og, small, w_conv, b_conv, b_igate, b_fgate, g_norm_out, bsz, seq):
    ln = min(MLSTM_CHUNK, seq)
    nc = seq // ln
    gbias = jnp.zeros((1, LANES), F32)
    gbias = gbias.at[0, SMALL_MI:SMALL_MI + MLSTM_HEADS].set(b_igate.astype(F32))
    gbias = gbias.at[0, SMALL_MF:SMALL_MF + MLSTM_HEADS].set(b_fgate.astype(F32))

    def row(width):
        return pl.BlockSpec((ln, width), lambda b, c: (b * nc + c, 0))

    def const(shape):
        return pl.BlockSpec(shape, lambda b, c: (0, 0))

    return pl.pallas_call(
        functools.partial(_mlstm_body, chunk=ln),
        grid=(bsz, nc),
        in_specs=[row(2 * MLSTM_WIDTH), row(MLSTM_WIDTH), row(MLSTM_WIDTH), row(LANES),
                  const((CONV_WIDTH, 2 * MLSTM_WIDTH)), const((1, 2 * MLSTM_WIDTH)),
                  const((1, LANES)), const((1, MLSTM_WIDTH))],
        out_specs=row(MLSTM_WIDTH),
        out_shape=jax.ShapeDtypeStruct((bsz * seq, MLSTM_WIDTH), BF16),
        scratch_shapes=[
            pltpu.VMEM((ln + SUBLANES, 2 * MLSTM_WIDTH), F32),
            pltpu.VMEM((MLSTM_HEADS, MLSTM_HEAD_DIM, MLSTM_HEAD_DIM), F32),
            pltpu.VMEM((SUBLANES, MLSTM_HEAD_DIM), F32),
            pltpu.VMEM((SUBLANES, LANES), F32),
        ],
        compiler_params=pltpu.CompilerParams(
            dimension_semantics=("arbitrary", "arbitrary"), vmem_limit_bytes=VMEM_LIMIT_BYTES),
        name="mlstm",
    )(mqk, mv, og, small, w_conv.astype(F32), b_conv.astype(F32)[None, :], gbias,
      g_norm_out.astype(F32)[None, :])


def _merge_body(x_ref, ya_ref, ym_ref, ga_ref, gm_ref, wpa_ref, wpm_ref, wo_ref, gmoe_ref,
                wrh_ref, wrl_ref, br_ref, x1_ref, xn_ref, ri_ref, rf_ref):
    pa = jnp.dot(ya_ref[...], wpa_ref[...], preferred_element_type=F32)
    pm = jnp.dot(ym_ref[...], wpm_ref[...], preferred_element_type=F32)
    merged = (_sigmoid(ga_ref[...].astype(F32)) * pa
              + _sigmoid(gm_ref[...].astype(F32)) * pm)
    x1 = x_ref[...] + jnp.dot(merged.astype(BF16), wo_ref[...], preferred_element_type=F32)
    x1_ref[...] = x1
    ms = jnp.mean(x1 * x1, axis=-1, keepdims=True)
    xn = x1 * lax.rsqrt(ms + EPS) * gmoe_ref[...]
    xn_ref[...] = xn

    xh = xn.astype(BF16)
    xl = (xn - xh.astype(F32)).astype(BF16)
    logits = (jnp.dot(xh, wrh_ref[...], preferred_element_type=F32)
              + jnp.dot(xh, wrl_ref[...], preferred_element_type=F32)
              + jnp.dot(xl, wrh_ref[...], preferred_element_type=F32)) + br_ref[...]
    lane = lax.broadcasted_iota(I32, logits.shape, 1).astype(F32)

    def first_argmax(vals, mask):
        v = jnp.where(mask, vals, -jnp.inf)
        top = jnp.max(v, axis=-1, keepdims=True)
        idx = jnp.min(jnp.where(mask & (v == top), lane, float(LANES)), axis=-1, keepdims=True)
        return top, idx

    is_group = lane < N_GROUPS
    g_max, g_sel = first_argmax(logits, is_group)
    g_den = jnp.sum(jnp.where(is_group, jnp.exp(logits - g_max), 0.0), axis=-1, keepdims=True)
    g_top = 1.0 / g_den
    lo = N_GROUPS + g_sel * EXPERTS_PER_GROUP
    in_group = (lane >= lo) & (lane < lo + EXPERTS_PER_GROUP)
    v1, i1 = first_argmax(logits, in_group)
    v2, i2 = first_argmax(logits, in_group & (lane != i1))
    e2 = jnp.exp(v2 - v1)
    p1 = 1.0 / (1.0 + e2)
    gate1 = p1 * g_top
    gate2 = (e2 * p1) * g_top
    ri_ref[...] = jnp.where(lane == 0, i1 - N_GROUPS,
                            jnp.where(lane == 1, i2 - N_GROUPS, 0.0)).astype(I32)
    rf_ref[...] = jnp.where(lane == 0, gate1, jnp.where(lane == 1, gate2, 0.0))


def _merge(x2, y_att, y_ml, ga, gm, w_proj_att, w_proj_mlstm, w_out, g_norm_moe,
           w_router_group, b_router_group, w_router_expert, b_router_expert):
    t_rows = x2.shape[0]
    tm = min(MERGE_ROWS, t_rows)
    n_r = N_GROUPS + N_EXPERTS
    wr = jnp.concatenate([w_router_group, w_router_expert,
                          jnp.zeros((D_MODEL, LANES - n_r), F32)], axis=-1).astype(F32)
    wr_hi = wr.astype(BF16)
    wr_lo = (wr - wr_hi.astype(F32)).astype(BF16)
    br =jnp.concatenate([b_router_group, b_router_expert,
                          jnp.zeros((LANES - n_r,), F32)]).astype(F32)[None, :]

    def row(width):
        return pl.BlockSpec((tm, width), lambda i: (i, 0))

    def const(shape):
        return pl.BlockSpec(shape, lambda i: (0, 0))

    out_shapes = (
        jax.ShapeDtypeStruct((t_rows, D_MODEL), F32),
        jax.ShapeDtypeStruct((t_rows, D_MODEL), F32),
        jax.ShapeDtypeStruct((t_rows, LANES), I32),
        jax.ShapeDtypeStruct((t_rows, LANES), F32),
    )
    return pl.pallas_call(
        _merge_body,
        grid=(t_rows // tm,),
        in_specs=[row(D_MODEL), row(ATT_Q_WIDTH), row(MLSTM_WIDTH), row(D_MODEL), row(D_MODEL),
                  const((ATT_Q_WIDTH, D_MODEL)), const((MLSTM_WIDTH, D_MODEL)),
                  const((D_MODEL, D_MODEL)), const((1, D_MODEL)), const((D_MODEL, LANES)),
                  const((D_MODEL, LANES)), const((1, LANES))],
        out_specs=[row(s.shape[1]) for s in out_shapes],
        out_shape=out_shapes,
        compiler_params=pltpu.CompilerParams(
            dimension_semantics=("arbitrary",), vmem_limit_bytes=VMEM_LIMIT_BYTES),
        name="merge_route",
    )(x2, y_att, y_ml, ga, gm, w_proj_att.astype(BF16), w_proj_mlstm.astype(BF16),
      w_out.astype(BF16), g_norm_moe.astype(F32)[None, :], wr_hi, wr_lo, br)


def _expert_body(be_ref, tok_cur_ref, tok_nxt_ref, xn_hbm, wg_ref, wu_ref, wd_ref,
                 ys_ref, xbuf, wg_bf, wu_bf, wd_bf, sem, *, rows):
    i = pl.program_id(0)
    nblk = pl.num_programs(0)
    slot = lax.rem(i, 2)
    nslot = 1 - slot

    @pl.when((i == 0) | (be_ref[i] != be_ref[jnp.maximum(i - 1, 0)]))
    def _():
        wg_bf[...] = wg_ref[0].astype(BF16)
        wu_bf[...] = wu_ref[0].astype(BF16)
        wd_bf[...] = wd_ref[0].astype(BF16)

    def row_copy(tok, dst_slot, r):
        return pltpu.make_async_copy(
            xn_hbm.at[pl.ds(tok, 1), :], xbuf.at[dst_slot, pl.ds(r, 1), :], sem.at[dst_slot])

    def all_rows(dst_slot):
        return pltpu.make_async_copy(
            xn_hbm.at[pl.ds(0, rows), :], xbuf.at[dst_slot], sem.at[dst_slot])

    @pl.when(i == 0)
    def _():
        for r in range(rows):
            row_copy(tok_cur_ref[0, 0, r], 0, r).start(priority=r % DMA_PRIORITIES)

    all_rows(slot).wait()

    def prefetch(part):
        for r in range(part * rows // 4, (part + 1) * rows // 4):
            row_copy(tok_nxt_ref[0, 0, r], nslot, r).start(priority=r % DMA_PRIORITIES)

    prefetch(0)
    xb = xbuf[slot].astype(BF16)
    a = jnp.dot(xb, wg_bf[...], preferred_element_type=F32)
    prefetch(1)
    u = jnp.dot(xb, wu_bf[...], preferred_element_type=F32)
    prefetch(2)
    hid = (a * _sigmoid(a) * u).astype(BF16)
    prefetch(3)
    ys_ref[...] = jnp.dot(hid, wd_bf[...], preferred_element_type=F32)

    @pl.when(i == nblk - 1)
    def _():
        all_rows(nslot).wait()


def _experts(xn, buf_tok, block_e, w_gate, w_up, w_down):
    rows = EXPERT_ROWS
    cap = buf_tok.shape[0]
    nblk = cap // rows
    tok3 = buf_tok.reshape(nblk, 1, rows)
    grid_spec = pltpu.PrefetchScalarGridSpec(
        num_scalar_prefetch=1,
        grid=(nblk,),
        in_specs=[
            pl.BlockSpec((1, 1, rows), lambda i, be: (i, 0, 0), memory_space=pltpu.SMEM),
            pl.BlockSpec((1, 1, rows), lambda i, be: (jnp.minimum(i + 1, nblk - 1), 0, 0),
                         memory_space=pltpu.SMEM),
            pl.BlockSpec(memory_space=pl.ANY),
            pl.BlockSpec((1, D_MODEL, D_EXPERT), lambda i, be: (be[i], 0, 0)),
            pl.BlockSpec((1, D_MODEL, D_EXPERT), lambda i, be: (be[i], 0, 0)),
            pl.BlockSpec((1, D_EXPERT, D_MODEL), lambda i, be: (be[i], 0, 0)),
        ],
        out_specs=pl.BlockSpec((rows, D_MODEL), lambda i, be: (i, 0)),
        scratch_shapes=[pltpu.VMEM((2, rows, D_MODEL), F32),
                        pltpu.VMEM((D_MODEL, D_EXPERT), BF16),
                        pltpu.VMEM((D_MODEL, D_EXPERT), BF16),
                        pltpu.VMEM((D_EXPERT, D_MODEL), BF16),
                        pltpu.SemaphoreType.DMA((2,))],
    )
    return pl.pallas_call(
        functools.partial(_expert_body, rows=rows),
        grid_spec=grid_spec,
        out_shape=jax.ShapeDtypeStruct((cap, D_MODEL), F32),
        compiler_params=pltpu.CompilerParams(
            dimension_semantics=("arbitrary",), vmem_limit_bytes=VMEM_LIMIT_BYTES),
        name="moe_experts",
    )(block_e, tok3, tok3, xn, w_gate.astype(F32), w_up.astype(F32), w_down.astype(F32))


def _combine_body(dcur_ref, dnxt_ref, x1_ref, rf_ref, ys_hbm, out_ref, ybuf, sem, *, rows):
    i = pl.program_id(0)
    nt = pl.num_programs(0)
    slot = lax.rem(i, 2)
    nslot = 1 - slot

    def row_copy(src, dst_slot, k, r):
        return pltpu.make_async_copy(
            ys_hbm.at[pl.ds(src, 1), :], ybuf.at[dst_slot, k, pl.ds(r, 1), :], sem.at[dst_slot])

    def wait_all(dst_slot):
        for k in range(TOP_K_EXPERTS):
            pltpu.make_async_copy(
                ys_hbm.at[pl.ds(0, rows), :], ybuf.at[dst_slot, k], sem.at[dst_slot]).wait()

    def gather(dest_ref, dst_slot):
        for r in range(rows):
            for k in range(TOP_K_EXPERTS):
                row_copy(dest_ref[0, 0, TOP_K_EXPERTS * r + k], dst_slot, k, r).start(
                    priority=k % DMA_PRIORITIES)

    @pl.when(i == 0)
    def _():
        gather(dcur_ref, 0)

    wait_all(slot)
    gather(dnxt_ref, nslot)
    gates = rf_ref[...]
    out = x1_ref[...]
    for k in range(TOP_K_EXPERTS):
        out = out + gates[:, k:k + 1] * ybuf[slot, k]
    out_ref[...] = out

    @pl.when(i == nt - 1)
    def _():
        wait_all(nslot)


def _combine(x1, rf, ys, dest):
    t_rows = x1.shape[0]
    rows = min(COMBINE_ROWS, t_rows)
    nt = t_rows // rows
    dest3 = dest.reshape(nt, 1, rows * TOP_K_EXPERTS)
    dspec = functools.partial(pl.BlockSpec, (1, 1, rows * TOP_K_EXPERTS), memory_space=pltpu.SMEM)
    return pl.pallas_call(
        functools.partial(_combine_body, rows=rows),
        grid=(nt,),
        in_specs=[
            dspec(lambda i: (i, 0, 0)),
            dspec(lambda i: (jnp.minimum(i + 1, nt - 1), 0, 0)),
            pl.BlockSpec((rows, D_MODEL), lambda i: (i, 0)),
            pl.BlockSpec((rows, LANES), lambda i: (i, 0)),
            pl.BlockSpec(memory_space=pl.ANY),
        ],
        out_specs=pl.BlockSpec((rows, D_MODEL), lambda i: (i, 0)),
        out_shape=jax.ShapeDtypeStruct((t_rows, D_MODEL), F32),
        scratch_shapes=[pltpu.VMEM((2, TOP_K_EXPERTS, rows, D_MODEL), F32),
                        pltpu.SemaphoreType.DMA((2,))],
        compiler_params=pltpu.CompilerParams(
            dimension_semantics=("arbitrary",), vmem_limit_bytes=VMEM_LIMIT_BYTES),
        name="moe_combine",
    )(dest3, dest3, x1, rf, ys)


def _route_plan(expert_id):
    t_rows = expert_id.shape[0]
    m = t_rows * TOP_K_EXPERTS
    rows = EXPERT_ROWS
    e_flat = expert_id.reshape(m)
    onehot = (e_flat[:, None] == jnp.arange(N_EXPERTS, dtype=I32)[None, :]).astype(I32)
    running = jnp.cumsum(onehot, axis=0)
    counts = running[-1]
    rank = jnp.sum(onehot * running, axis=1) - 1
    padded = ((counts + rows - 1) // rows) * rows
    pends = jnp.cumsum(padded)
    pstarts = pends - padded
    cap = ((m + N_EXPERTS * rows + rows - 1) // rows) * rows
    nblk = cap // rows
    blk_start = jnp.arange(nblk, dtype=I32) * rows
    block_e = jnp.minimum(jnp.sum((pends[None, :] <= blk_start[:, None]).astype(I32), axis=1),
                          N_EXPERTS - 1)
    dest = (jnp.sum(onehot * pstarts[None, :], axis=1) + rank).astype(I32)
    order = jnp.argsort(e_flat, stable=True).astype(I32)
    starts = jnp.cumsum(counts) - counts
    slot = jnp.arange(cap, dtype=I32)
    off = slot - jnp.repeat(pstarts[block_e], rows)
    live = (off < jnp.repeat(counts[block_e], rows)) & (slot < pends[-1])
    src = jnp.clip(jnp.repeat(starts[block_e], rows) + off, 0, m - 1)
    buf_tok = jnp.where(live, order[src] // TOP_K_EXPERTS, slot % t_rows).astype(I32)
    return buf_tok, block_e, dest


def kernel(x, positions, g_norm_mix, w_in, g_q_att, g_k_att, w_conv_mlstm, b_conv_mlstm, b_igate,
           b_fgate, g_norm_mlstm_out, w_proj_att, w_proj_mlstm, w_out, g_norm_moe, w_router_group,
           b_router_group, w_router_expert, b_router_expert, w_exp_gate, w_exp_up, w_exp_down):
    bsz, seq, dm = x.shape
    depth = g_norm_mix.shape[0]
    x2 = x.reshape(bsz * seq, dm)
    pos2 = positions.reshape(bsz * seq, 1).astype(I32)
    for l in range(depth):
        aq, ak, av, iq, ik, small, mqk, mv, og, ga, gm = _in_proj(
            x2, pos2, g_norm_mix[l], w_in[l], g_q_att[l], g_k_att[l])
        y_att = _dsa(aq, iq, small, ak, av, ik, bsz, seq)
        y_ml = _mlstm(mqk, mv, og, small, w_conv_mlstm[l], b_conv_mlstm[l], b_igate[l], b_fgate[l],
                      g_norm_mlstm_out[l], bsz, seq)
        x1, xn, ri, rf = _merge(x2, y_att, y_ml, ga, gm, w_proj_att[l], w_proj_mlstm[l], w_out[l],
                                g_norm_moe[l], w_router_group[l], b_router_group[l],
                                w_router_expert[l], b_router_expert[l])
        buf_tok, block_e, dest = _route_plan(ri[:, :TOP_K_EXPERTS])
        ys = _experts(xn, buf_tok, block_e, w_exp_gate[l], w_exp_up[l], w_exp_down[l])
        x2 = _combine(x1, rf, ys, dest)
    return x2.reshape(bsz, seq, dm)
```

```python
import functools

import numpy as np
import jax
import jax.numpy as jnp
from jax import lax
from jax.experimental import pallas as pl
from jax.experimental.pallas import tpu as pltpu

F32 = jnp.float32
BF16 = jnp.bfloat16
I32 = jnp.int32

D_MODEL = 1024
CHUNK = 64
Q_BLOCK = 128
EPS = 1e-6
NEG_INF = -1e30

ATT_HEADS = 8
ATT_KV_HEADS = 2
ATT_HEAD_DIM = 64
ATT_Q_WIDTH = ATT_HEADS * ATT_HEAD_DIM
ATT_KV_WIDTH = ATT_KV_HEADS * ATT_HEAD_DIM
IDX_HEADS = 8
IDX_DIM = 32
TOPK_MAX = 256
ROPE_THETA = 500000.0
ROPE_FRACTION = 4

MLSTM_HEADS = 4
MLSTM_HEAD_DIM = 128
MLSTM_WIDTH = MLSTM_HEADS * MLSTM_HEAD_DIM
CONV_WIDTH = 4

N_GROUPS = 4
EXPERTS_PER_GROUP = 8
N_EXPERTS = N_GROUPS * EXPERTS_PER_GROUP
TOP_K_EXPERTS = 2
D_EXPERT = 512

LANES = 128
SUBLANES = 8
VMEM_LIMIT_BYTES = 56 * 1024 * 1024

SMALL_IK = 0
SMALL_IW = IDX_DIM
SMALL_MI = SMALL_IW + IDX_HEADS
SMALL_MF = SMALL_MI + MLSTM_HEADS

COL_AQ = 0
COL_AK = COL_AQ + ATT_Q_WIDTH
COL_AV = COL_AK + ATT_KV_WIDTH
COL_IQ = COL_AV + ATT_KV_WIDTH
COL_SMALL = COL_IQ + IDX_HEADS * IDX_DIM
COL_MQK = COL_SMALL + LANES
COL_MV = COL_MQK + 2 * MLSTM_WIDTH
COL_MO = COL_MV + MLSTM_WIDTH
COL_GA = COL_MO + MLSTM_WIDTH
COL_GM = COL_GA + D_MODEL
COL_END = COL_GM + D_MODEL

KEY_TILE = 512
MLSTM_CHUNK = 256
MERGE_ROWS = 512
EXPERT_ROWS = 256
COMBINE_ROWS = 1024
WORD_BITS = 32
PLANE_GROUP = WORD_BITS * SUBLANES
VT_ROWS = ATT_KV_WIDTH + 16

DMA_PRIORITIES = 2

LOG2_E = 1.4426950408889634
_INT_MIN = -(2 ** 31)


def _sortable_key_const(v):
    b = int(np.array(v, np.float32).view(np.int32))
    return b if b >= 0 else b ^ 0x7FFFFFFF


_KEY_NEG_INF = _sortable_key_const(NEG_INF)
_KEY_VALID = _sortable_key_const(NEG_INF * 0.5)


def _split3(a):
    hi = a.astype(BF16)
    r1 = a - hi.astype(F32)
    mid = r1.astype(BF16)
    lo = (r1 - mid.astype(F32)).astype(BF16)
    return hi, mid, lo


def _dot_exact_rhs(a, b_bf16):
    hi, mid, lo = _split3(a)
    d = functools.partial(jnp.dot, preferred_element_type=F32)
    return d(hi, b_bf16) + d(mid, b_bf16) + d(lo, b_bf16)


def _dot_exact_lhs(a_bf16, b):
    hi, mid, lo = _split3(b)
    d = functools.partial(jnp.dot, preferred_element_type=F32)
    return d(a_bf16, hi) + d(a_bf16, mid) + d(a_bf16, lo)


def _dot_nt(a, b):
    return lax.dot_general(a, b, (((1,), (1,)), ((), ())), preferred_element_type=F32)


def _sigmoid(x):
    return 0.5 * jnp.tanh(0.5 * x) + 0.5


def _in_proj_body(x_ref, pos_ref, g_ref, w_ref, rc_ref, bd_ref, gq_ref, gk_ref,
                  aq_ref, ak_ref, av_ref, iq_ref, ik_ref, small_ref, mqk_ref, mv_ref,
                  og_ref, ga_ref, gm_ref):
    x = x_ref[...]
    ms = jnp.mean(x * x, axis=-1, keepdims=True)
    h = (x * lax.rsqrt(ms + EPS) * g_ref[...]).astype(BF16)
    posf = pos_ref[...].astype(F32)

    def proj(lo, hi):
        return jnp.dot(h, w_ref[:, lo:hi], preferred_element_type=F32)

    def rope(t, cos, s_up, s_dn, half):
        n = t.shape[-1]
        return t * cos + pltpu.roll(t, half, 1) * s_up + pltpu.roll(t, n - half, 1) * s_dn

    def head_norm(t, gain):
        ssum = jnp.dot((t * t).astype(BF16), bd_ref[...], preferred_element_type=F32)
        return t * lax.rsqrt(ssum * (1.0 / ATT_HEAD_DIM) + EPS) * gain

    ang_a = posf * rc_ref[0:1, :]
    cos_a, sin_a = jnp.cos(ang_a), jnp.sin(ang_a)
    up_a, dn_a = sin_a * rc_ref[1:2, :], sin_a * rc_ref[2:3, :]
    ang_i = posf * rc_ref[3:4, :]
    cos_i, sin_i = jnp.cos(ang_i), jnp.sin(ang_i)
    up_i, dn_i = sin_i * rc_ref[4:5, :], sin_i * rc_ref[5:6, :]
    small_mask = rc_ref[6:7, :]
    cos_s = 1.0 + (cos_i - 1.0) * small_mask
    up_s, dn_s = up_i * small_mask, dn_i * small_mask

    att_half = ATT_HEAD_DIM // ROPE_FRACTION // 2
    idx_half = IDX_DIM // ROPE_FRACTION // 2
    q_scale = ATT_HEAD_DIM ** -0.5 * LOG2_E
    tm = x.shape[0]

    def store_qblock_t(ref, c, t):
        for r in range(tm // Q_BLOCK):
            ref[r, c * LANES:(c + 1) * LANES, :] = t[r * Q_BLOCK:(r + 1) * Q_BLOCK, :].T.astype(BF16)

    for c in range(ATT_Q_WIDTH // LANES):
        t = proj(COL_AQ + c * LANES, COL_AQ + (c + 1) * LANES)
        t = rope(head_norm(t, gq_ref[...]), cos_a, up_a, dn_a, att_half)
        store_qblock_t(aq_ref, c, t * q_scale)
    t = proj(COL_AK, COL_AK + LANES)
    ak_ref[...] = rope(head_norm(t, gk_ref[...]), cos_a, up_a, dn_a, att_half).astype(BF16)
    av_ref[0, 0:ATT_KV_WIDTH, :] = proj(COL_AV, COL_AV + LANES).T.astype(BF16)
    av_ref[0, ATT_KV_WIDTH:VT_ROWS, :] = jnp.ones((VT_ROWS - ATT_KV_WIDTH, tm), BF16)
    for c in range(IDX_HEADS * IDX_DIM // LANES):
        t = proj(COL_IQ + c * LANES, COL_IQ + (c + 1) * LANES)
        store_qblock_t(iq_ref, c, rope(t, cos_i, up_i, dn_i, idx_half))
    t = rope(proj(COL_SMALL, COL_SMALL + LANES), cos_s, up_s, dn_s, idx_half)
    small_ref[...] = t
    ik_ref[...] = t[:, SMALL_IK:SMALL_IK + IDX_DIM].astype(BF16)
    for c in range(2 * MLSTM_WIDTH // 512):
        mqk_ref[:, c * 512:(c + 1) * 512] = proj(COL_MQK + c * 512, COL_MQK + (c + 1) * 512)
    mv_ref[...] = proj(COL_MV, COL_MV + MLSTM_WIDTH).astype(BF16)
    og_ref[...] = _sigmoid(proj(COL_MO, COL_MO + MLSTM_WIDTH)).astype(BF16)
    for c in range(D_MODEL // 512):
        ga_ref[:, c * 512:(c + 1) * 512] = proj(
            COL_GA + c * 512, COL_GA + (c + 1) * 512).astype(BF16)
        gm_ref[:, c * 512:(c + 1) * 512] = proj(
            COL_GM + c * 512, COL_GM + (c + 1) * 512).astype(BF16)


def _rope_consts():
    lane = np.arange(LANES)

    def inv_freq(rot):
        half = rot // 2
        return jnp.float32(ROPE_THETA) ** (-jnp.arange(half, dtype=F32) * 2.0 / rot)

    rot_a = ATT_HEAD_DIM // ROPE_FRACTION
    rot_i = IDX_DIM // ROPE_FRACTION
    fa, fi = inv_freq(rot_a), inv_freq(rot_i)
    ja, ji = lane % ATT_HEAD_DIM, lane % IDX_DIM
    rc = jnp.zeros((16, LANES), F32)
    rc = rc.at[0].set(jnp.where(ja < rot_a, fa[ja % (rot_a // 2)], 0.0))
    rc = rc.at[1].set(jnp.asarray(((ja >= rot_a // 2) & (ja < rot_a)).astype(np.float32)))
    rc = rc.at[2].set(jnp.asarray(-(ja < rot_a // 2).astype(np.float32)))
    rc = rc.at[3].set(jnp.where(ji < rot_i, fi[ji % (rot_i // 2)], 0.0))
    rc = rc.at[4].set(jnp.asarray(((ji >= rot_i // 2) & (ji < rot_i)).astype(np.float32)))
    rc = rc.at[5].set(jnp.asarray(-(ji < rot_i // 2).astype(np.float32)))
    rc = rc.at[6].set(jnp.asarray((lane < IDX_DIM).astype(np.float32)))
    return rc


def _pack_w_in(w_in):
    sizes = (ATT_Q_WIDTH, ATT_KV_WIDTH, ATT_KV_WIDTH, IDX_HEADS * IDX_DIM, IDX_DIM, IDX_HEADS,
             MLSTM_WIDTH, MLSTM_WIDTH, MLSTM_WIDTH, MLSTM_WIDTH, MLSTM_HEADS, MLSTM_HEADS,
             D_MODEL, D_MODEL)
    pts = np.cumsum(sizes)[:-1].tolist()
    (aq, ak, av, iq, ik, iw, mq, mk, mv, mo, mi, mf, ga, gm) = jnp.split(w_in, pts, axis=-1)
    pad = jnp.zeros((D_MODEL, LANES - (SMALL_MF + MLSTM_HEADS)), w_in.dtype)
    small = jnp.concatenate([ik, iw, mi, mf, pad], axis=-1)
    return jnp.concatenate([aq, ak, av, iq, small, mq, mk, mv, mo, ga, gm], axis=-1).astype(BF16)


def _in_proj(x2, pos2, g_norm_mix, w_in, g_q_att, g_k_att):
    t_rows = x2.shape[0]
    tm = KEY_TILE
    w = _pack_w_in(w_in)
    lane = np.arange(LANES)
    bd = jnp.asarray((lane[:, None] // ATT_HEAD_DIM == lane[None, :] // ATT_HEAD_DIM)
                     .astype(np.float32)).astype(BF16)
    reps = LANES // ATT_HEAD_DIM
    gq = jnp.tile(g_q_att.astype(F32), reps)[None, :]
    gk = jnp.tile(g_k_att.astype(F32), reps)[None, :]

    def row(width):
        return pl.BlockSpec((tm, width), lambda i: (i, 0))

    def const(shape):
        return pl.BlockSpec(shape, lambda i: (0, 0))

    qb = tm // Q_BLOCK

    def qblock_t(width):
        return pl.BlockSpec((qb, width, Q_BLOCK), lambda i: (i, 0, 0))

    out_shapes = (
        jax.ShapeDtypeStruct((t_rows // Q_BLOCK, ATT_Q_WIDTH, Q_BLOCK), BF16),
        jax.ShapeDtypeStruct((t_rows, ATT_KV_WIDTH), BF16),
        jax.ShapeDtypeStruct((t_rows // tm, VT_ROWS, tm), BF16),
        jax.ShapeDtypeStruct((t_rows // Q_BLOCK, IDX_HEADS * IDX_DIM, Q_BLOCK), BF16),
        jax.ShapeDtypeStruct((t_rows, IDX_DIM), BF16),
        jax.ShapeDtypeStruct((t_rows, LANES), F32),
        jax.ShapeDtypeStruct((t_rows, 2 * MLSTM_WIDTH), F32),
        jax.ShapeDtypeStruct((t_rows, MLSTM_WIDTH), BF16),
        jax.ShapeDtypeStruct((t_rows, MLSTM_WIDTH), BF16),
        jax.ShapeDtypeStruct((t_rows, D_MODEL), BF16),
        jax.ShapeDtypeStruct((t_rows, D_MODEL), BF16),
    )
    return pl.pallas_call(
        _in_proj_body,
        grid=(t_rows // tm,),
        in_specs=[row(D_MODEL), row(1), const((1, D_MODEL)), const((D_MODEL, COL_END)),
                  const((16, LANES)), const((LANES, LANES)), const((1, LANES)), const((1, LANES))],
        out_specs=[qblock_t(ATT_Q_WIDTH), row(ATT_KV_WIDTH),
                   pl.BlockSpec((1, VT_ROWS, tm), lambda i: (i, 0, 0)),
                   qblock_t(IDX_HEADS * IDX_DIM)] + [row(s.shape[1]) for s in out_shapes[4:]],
        out_shape=out_shapes,
        compiler_params=pltpu.CompilerParams(
            dimension_semantics=("arbitrary",), vmem_limit_bytes=VMEM_LIMIT_BYTES),
        name="in_proj",
    )(x2, pos2, g_norm_mix.astype(F32)[None, :], w, _rope_consts(), bd, gq, gk)


def _bit_transpose32(words):
    a = list(words)
    j, m = 16, 0x0000FFFF
    while j:
        k = 0
        while k < WORD_BITS:
            t = (a[k] ^ lax.shift_right_logical(a[k + j], jnp.int32(j))) & m
            a[k] = a[k] ^ t
            a[k + j] = a[k + j] ^ lax.shift_left(t, jnp.int32(j))
            k = (k + j + 1) & ~j
        j >>= 1
        m = m ^ (m << j)
    return a


def _dsa_body(aq_ref, iq_ref, small_ref, k_ref, vt_ref, ik_ref, out_ref,
              key_scr, plane_scr, qi_scr, q2_scr, p_scr, acc_scr, *, topk, tk):
    blk = pl.program_id(1)
    nq = Q_BLOCK
    rep = ATT_HEADS // ATT_KV_HEADS
    groups_per_tile = tk // PLANE_GROUP
    n_tiles = lax.shift_right_logical(blk * nq + nq + tk - 1, int(np.log2(tk)))
    idx_scale = (IDX_DIM * IDX_HEADS) ** -0.5
    plane_rows = plane_scr.shape[1]

    @pl.when((pl.program_id(0) == 0) & (blk == 0))
    def _():
        plane_scr[...] = jnp.zeros_like(plane_scr)

    for h in range(IDX_HEADS):
        qi_scr[:, h * nq:(h + 1) * nq] = iq_ref[0, h * IDX_DIM:(h + 1) * IDX_DIM, :]
    w_rows = small_ref[...].T[SMALL_IW:SMALL_IW + IDX_HEADS, :] * idx_scale

    q_chunk = lax.shift_right_logical(
        blk * nq + lax.broadcasted_iota(I32, (1, nq), 1), int(np.log2(CHUNK)))
    key_row = lax.broadcasted_iota(I32, (tk, 1), 0)

    def score_tile(t, carry):
        start = pl.multiple_of(t * tk, tk)
        rel = jnp.dot(ik_ref[pl.ds(start, tk), :], qi_scr[...], preferred_element_type=F32)
        sc = jnp.maximum(rel[:, 0:nq], 0.0) * w_rows[0:1]
        for h in range(1, IDX_HEADS):
            sc = sc + jnp.maximum(rel[:, h * nq:(h + 1) * nq], 0.0) * w_rows[h:h + 1]
        visible = lax.shift_right_logical(start + key_row, int(np.log2(CHUNK))) <= q_chunk
        sc = jnp.where(visible, sc, NEG_INF)
        bits = lax.bitcast_convert_type(sc, I32)
        key = jnp.where(bits < 0, bits ^ 0x7FFFFFFF, bits)
        key_scr[pl.ds(start, tk), :] = key
        biased = key ^ _INT_MIN
        for g in range(groups_per_tile):
            words = [biased[g * PLANE_GROUP + i * SUBLANES:g * PLANE_GROUP + (i + 1) * SUBLANES, :]
                     for i in range(WORD_BITS)]
            planes = _bit_transpose32(words)
            row0 = pl.multiple_of((t * groups_per_tile + g) * SUBLANES, SUBLANES)
            for p in range(WORD_BITS):
                plane_scr[p, pl.ds(row0, SUBLANES), :] = planes[p]
        return carry

    lax.fori_loop(0, n_tiles, score_tile, 0)

    group_of_row = lax.shift_right_logical(
        lax.broadcasted_iota(I32, (plane_rows, nq), 0), int(np.log2(SUBLANES)))
    alive = jnp.where(group_of_row < n_tiles * groups_per_tile, -1, 0)
    k_rem = jnp.full((1, nq), float(topk), F32)
    thr_u = jnp.zeros((1, nq), I32)

    def lane_count(mask_words):
        pc = lax.population_count(mask_words)
        part = jnp.sum(pc.reshape(plane_rows // SUBLANES, SUBLANES, nq), axis=0)
        return jnp.sum(part.astype(F32), axis=0, keepdims=True)

    for p in range(WORD_BITS):
        bit_value = _INT_MIN if p == 0 else 1 << (WORD_BITS - 1 - p)
        ones = alive & plane_scr[p]
        c_ones = lane_count(ones)
        take = c_ones >= k_rem
        alive = jnp.where(take, ones, alive ^ ones)
        k_rem = jnp.where(take, k_rem, k_rem - c_ones)
        thr_u = jnp.where(take, thr_u | bit_value, thr_u)
    thr = thr_u ^ _INT_MIN
    quota = k_rem

    q2_scr[...] = jnp.zeros_like(q2_scr)
    for h in range(ATT_HEADS):
        g = h // rep
        q2_scr[g * ATT_HEAD_DIM:(g + 1) * ATT_HEAD_DIM, h * nq:(h + 1) * nq] = (
            aq_ref[0, h * ATT_HEAD_DIM:(h + 1) * ATT_HEAD_DIM, :])

    def attend(with_ties):
        acc_scr[...] = jnp.zeros_like(acc_scr)
        if with_ties:
            rr = lax.broadcasted_iota(I32, (tk, tk), 0)
            cc = lax.broadcasted_iota(I32, (tk, tk), 1)
            earlier = jnp.where(cc < rr, 1.0, 0.0).astype(BF16)

        def tile(t, carry):
            m_all, eq_seen = carry
            start = pl.multiple_of(t * tk, tk)
            key = key_scr[pl.ds(start, tk), :]
            valid = key > _KEY_VALID
            if with_ties:
                eq = key == thr
                eqf = jnp.where(eq, 1.0, 0.0)
                rank = jnp.dot(earlier, eqf.astype(BF16), preferred_element_type=F32) + eq_seen
                sel = valid & ((key > thr) | (eq & (rank < quota)))
                eq_seen = eq_seen + jnp.sum(eqf, axis=0, keepdims=True)
            else:
                sel = valid & (key >= thr)
            bias = jnp.where(sel, 0.0, NEG_INF)
            s = jnp.dot(k_ref[pl.ds(start, tk), :], q2_scr[...], preferred_element_type=F32)
            m_rows, a_rows = [], []
            for h in range(ATT_HEADS):
                cols = slice(h * nq, (h + 1) * nq)
                sh = s[:, cols] + bias
                m_old = m_all[h:h + 1]
                m_new = jnp.maximum(m_old, jnp.max(sh, axis=0, keepdims=True))
                p_scr[:, cols] = jnp.exp2(sh - m_new).astype(BF16)
                m_rows.append(m_new)
                a_rows.append(jnp.exp2(m_old - m_new))
            acc_scr[...] = (acc_scr[...] * jnp.concatenate(a_rows, axis=1)
                            + jnp.dot(vt_ref[t], p_scr[...], preferred_element_type=F32))
            return jnp.concatenate(m_rows, axis=0), eq_seen

        lax.fori_loop(0, n_tiles, tile,
                      (jnp.full((ATT_HEADS, nq), NEG_INF, F32), jnp.zeros((1, nq), F32)))
        inv_l = 1.0 / acc_scr[ATT_KV_WIDTH:ATT_KV_WIDTH + 1, :]
        for j in range(ATT_HEADS // 2):
            g = (2 * j) // rep
            halves = [acc_scr[g * ATT_HEAD_DIM:(g + 1) * ATT_HEAD_DIM, h * nq:(h + 1) * nq]
                      * inv_l[:, h * nq:(h + 1) * nq] for h in (2 * j, 2 * j + 1)]
            out_ref[:, j * LANES:(j + 1) * LANES] = jnp.concatenate(halves, axis=0).T.astype(BF16)

    tie_lanes = jnp.where((lane_count(alive) > quota) & (thr > _KEY_NEG_INF), 1.0, 0.0)
    need_ties = jnp.max(tie_lanes) > 0.5

    @pl.when(need_ties)
    def _():
        attend(True)

    @pl.when(jnp.logical_not(need_ties))
    def _():
        attend(False)


def _dsa(aq_t, iq_t, small, ak, av_t, ik, bsz, seq):
    nb = seq // Q_BLOCK
    tk = KEY_TILE
    topk = min(TOPK_MAX, seq // 4)
    assert seq % tk == 0 and tk >= topk and tk & (tk - 1) == 0 and tk % PLANE_GROUP == 0
    nkt = seq // tk

    def qspec_t(width):
        return pl.BlockSpec((1, width, Q_BLOCK), lambda b, j: (b * nb + j, 0, 0))

    def qspec(width):
        return pl.BlockSpec((Q_BLOCK, width), lambda b, j: (b * nb + j, 0))

    def kspec(width):
        return pl.BlockSpec((seq, width), lambda b, j: (b, 0))

    cols = ATT_HEADS * Q_BLOCK
    return pl.pallas_call(
        functools.partial(_dsa_body, topk=topk, tk=tk),
        grid=(bsz, nb),
        in_specs=[qspec_t(ATT_Q_WIDTH), qspec_t(IDX_HEADS * IDX_DIM), qspec(LANES),
                  kspec(ATT_KV_WIDTH),
                  pl.BlockSpec((nkt, VT_ROWS, tk), lambda b, j: (b, 0, 0)),
                  kspec(IDX_DIM)],
        out_specs=qspec(ATT_Q_WIDTH),
        out_shape=jax.ShapeDtypeStruct((bsz * seq, ATT_Q_WIDTH), BF16),
        scratch_shapes=[
            pltpu.VMEM((seq, Q_BLOCK), I32),
            pltpu.VMEM((WORD_BITS, seq // WORD_BITS, Q_BLOCK), I32),
            pltpu.VMEM((IDX_DIM, cols), BF16),
            pltpu.VMEM((ATT_KV_WIDTH, cols), BF16),
            pltpu.VMEM((tk, cols), BF16),
            pltpu.VMEM((VT_ROWS, cols), F32),
        ],
        compiler_params=pltpu.CompilerParams(
            dimension_semantics=("arbitrary", "arbitrary"), vmem_limit_bytes=VMEM_LIMIT_BYTES),
        name="dsa_attention",
    )(aq_t, iq_t, small, ak, av_t, ik)


def _mlstm_body(mqk_ref, mv_ref, og_ref, small_ref, cw_ref, cb_ref, gb_ref, gn_ref, y_ref,
                ubuf, c_scr, n_scr, m_scr, *, chunk):
    ln = chunk
    d = MLSTM_HEAD_DIM
    halo = SUBLANES

    @pl.when(pl.program_id(1) == 0)
    def _():
        ubuf[0:halo, :] = jnp.zeros((halo, 2 * MLSTM_WIDTH), F32)
        c_scr[...] = jnp.zeros_like(c_scr)
        n_scr[...] = jnp.zeros_like(n_scr)
        m_scr[...] = jnp.zeros_like(m_scr)

    ubuf[halo:halo + ln, :] = mqk_ref[...]
    conv = cb_ref[...]
    for j in range(CONV_WIDTH):
        off = halo - (CONV_WIDTH - 1) + j
        conv = conv + cw_ref[j:j + 1, :] * ubuf[off:off + ln, :]
    ubuf[0:halo, :] = ubuf[ln:ln + halo, :]
    qk = conv * _sigmoid(conv)
    q_all = qk[:, :MLSTM_WIDTH]
    k_all = qk[:, MLSTM_WIDTH:] * (d ** -0.5)

    sm = small_ref[...] + gb_ref[...]
    lf = jnp.minimum(sm, 0.0) - jnp.log(1.0 + jnp.exp(-jnp.abs(sm)))
    sm_t = sm.T
    lf_t = lf.T
    rr = lax.broadcasted_iota(I32, (ln, ln), 0)
    cc = lax.broadcasted_iota(I32, (ln, ln), 1)
    causal = cc <= rr
    tri = jnp.where(causal, 1.0, 0.0).astype(BF16)
    tri_t = jnp.where(rr <= cc, 1.0, 0.0).astype(BF16)
    b_cols = _dot_exact_lhs(tri, lf)
    b_rows = _dot_exact_rhs(lf_t, tri_t)

    for h in range(MLSTM_HEADS):
        hs = slice(h * d, (h + 1) * d)
        ig_col = sm[:, SMALL_MI + h:SMALL_MI + h + 1]
        ig_row = sm_t[SMALL_MI + h:SMALL_MI + h + 1, :]
        b_col = b_cols[:, SMALL_MF + h:SMALL_MF + h + 1]
        b_row = b_rows[SMALL_MF + h:SMALL_MF + h + 1, :]
        m_st = m_scr[h:h + 1, 0:1]
        c_st = c_scr[h]
        n_st = n_scr[h:h + 1, :]

        dmat = jnp.where(causal, b_col - b_row + ig_row, NEG_INF)
        inter = b_col + m_st
        m_t = jnp.maximum(inter, jnp.max(dmat, axis=-1, keepdims=True))
        w_intra = jnp.exp(dmat - m_t)
        w_inter = jnp.exp(inter - m_t)
        qf = q_all[:, hs]
        kf = k_all[:, hs]
        qh = qf.astype(BF16)
        vh = mv_ref[:, hs]
        s = _dot_nt(qh, kf.astype(BF16)) * w_intra
        num = (w_inter * jnp.dot(qh, c_st.astype(BF16), preferred_element_type=F32)
               + jnp.dot(s.astype(BF16), vh, preferred_element_type=F32))
        den = (w_inter * jnp.sum(qf * n_st, axis=-1, keepdims=True)
               + jnp.sum(s, axis=-1, keepdims=True))
        hh = num / jnp.maximum(jnp.abs(den), jnp.exp(-m_t))

        b_last = b_col[ln - 1:ln, :]
        gdec = b_last - b_col + ig_col
        m_new = jnp.maximum(b_last + m_st, jnp.max(gdec, axis=0, keepdims=True))
        decay = jnp.exp(b_last + m_st - m_new)
        kw = kf * jnp.exp(gdec - m_new)
        c_scr[h] = decay * c_st + jnp.dot(kw.T.astype(BF16), vh, preferred_element_type=F32)
        n_scr[h:h + 1, :] = decay * n_st + jnp.sum(kw, axis=0, keepdims=True)
        m_scr[h:h + 1, :] = jnp.broadcast_to(m_new, (1, LANES))

        ms = jnp.mean(hh * hh, axis=-1, keepdims=True)
        yh = hh * lax.rsqrt(ms + EPS) * gn_ref[:, hs]
        y_ref[:, hs] = (og_ref[:, hs].astype(F32) * yh).astype(BF16)


def _mlstm(mqk, mv, og, small, w_conv, b_conv, b_igate, b_fgate, g_norm_out, bsz, seq):
    ln = min(MLSTM_CHUNK, seq)
    nc = seq // ln
    gbias = jnp.zeros((1, LANES), F32)
    gbias = gbias.at[0, SMALL_MI:SMALL_MI + MLSTM_HEADS].set(b_igate.astype(F32))
    gbias = gbias.at[0, SMALL_MF:SMALL_MF + MLSTM_HEADS].set(b_fgate.astype(F32))

    def row(width):
        return pl.BlockSpec((ln, width), lambda b, c: (b * nc + c, 0))

    def const(shape):
        return pl.BlockSpec(shape, lambda b, c: (0, 0))

    return pl.pallas_call(
        functools.partial(_mlstm_body, chunk=ln),
        grid=(bsz, nc),
        in_specs=[row(2 * MLSTM_WIDTH), row(MLSTM_WIDTH), row(MLSTM_WIDTH), row(LANES),
                  const((CONV_WIDTH, 2 * MLSTM_WIDTH)), const((1, 2 * MLSTM_WIDTH)),
                  const((1, LANES)), const((1, MLSTM_WIDTH))],
        out_specs=row(MLSTM_WIDTH),
        out_shape=jax.ShapeDtypeStruct((bsz * seq, MLSTM_WIDTH), BF16),
        scratch_shapes=[
            pltpu.VMEM((ln + SUBLANES, 2 * MLSTM_WIDTH), F32),
            pltpu.VMEM((MLSTM_HEADS, MLSTM_HEAD_DIM, MLSTM_HEAD_DIM), F32),
            pltpu.VMEM((SUBLANES, MLSTM_HEAD_DIM), F32),
            pltpu.VMEM((SUBLANES, LANES), F32),
        ],
        compiler_params=pltpu.CompilerParams(
            dimension_semantics=("arbitrary", "arbitrary"), vmem_limit_bytes=VMEM_LIMIT_BYTES),
        name="mlstm",
    )(mqk, mv, og, small, w_conv.astype(F32), b_conv.astype(F32)[None, :], gbias,
      g_norm_out.astype(F32)[None, :])


def _merge_body(x_ref, ya_ref, ym_ref, ga_ref, gm_ref, wpa_ref, wpm_ref, wo_ref, gmoe_ref,
                wrh_ref, wrl_ref, br_ref, x1_ref, xn_ref, ri_ref, rf_ref):
    pa = jnp.dot(ya_ref[...], wpa_ref[...], preferred_element_type=F32)
    pm = jnp.dot(ym_ref[...], wpm_ref[...], preferred_element_type=F32)
    merged = (_sigmoid(ga_ref[...].astype(F32)) * pa
              + _sigmoid(gm_ref[...].astype(F32)) * pm)
    x1 = x_ref[...] + jnp.dot(merged.astype(BF16), wo_ref[...], preferred_element_type=F32)
    x1_ref[...] = x1
    ms = jnp.mean(x1 * x1, axis=-1, keepdims=True)
    xn = x1 * lax.rsqrt(ms + EPS) * gmoe_ref[...]
    xn_ref[...] = xn

    xh = xn.astype(BF16)
    xl = (xn - xh.astype(F32)).astype(BF16)
    logits = (jnp.dot(xh, wrh_ref[...], preferred_element_type=F32)
              + jnp.dot(xh, wrl_ref[...], preferred_element_type=F32)
              + jnp.dot(xl, wrh_ref[...], preferred_element_type=F32)) + br_ref[...]
    lane = lax.broadcasted_iota(I32, logits.shape, 1).astype(F32)

    def first_argmax(vals, mask):
        v = jnp.where(mask, vals, -jnp.inf)
        top = jnp.max(v, axis=-1, keepdims=True)
        idx = jnp.min(jnp.where(mask & (v == top), lane, float(LANES)), axis=-1, keepdims=True)
        return top, idx

    is_group = lane < N_GROUPS
    g_max, g_sel = first_argmax(logits, is_group)
    g_den = jnp.sum(jnp.where(is_group, jnp.exp(logits - g_max), 0.0), axis=-1, keepdims=True)
    g_top = 1.0 / g_den
    lo = N_GROUPS + g_sel * EXPERTS_PER_GROUP
    in_group = (lane >= lo) & (lane < lo + EXPERTS_PER_GROUP)
    v1, i1 = first_argmax(logits, in_group)
    v2, i2 = first_argmax(logits, in_group & (lane != i1))
    e2 = jnp.exp(v2 - v1)
    p1 = 1.0 / (1.0 + e2)
    gate1 = p1 * g_top
    gate2 = (e2 * p1) * g_top
    ri_ref[...] = jnp.where(lane == 0, i1 - N_GROUPS,
                            jnp.where(lane == 1, i2 - N_GROUPS, 0.0)).astype(I32)
    rf_ref[...] = jnp.where(lane == 0, gate1, jnp.where(lane == 1, gate2, 0.0))


def _merge(x2, y_att, y_ml, ga, gm, w_proj_att, w_proj_mlstm, w_out, g_norm_moe,
           w_router_group, b_router_group, w_router_expert, b_router_expert):
    t_rows = x2.shape[0]
    tm = min(MERGE_ROWS, t_rows)
    n_r = N_GROUPS + N_EXPERTS
    wr = jnp.concatenate([w_router_group, w_router_expert,
                          jnp.zeros((D_MODEL, LANES - n_r), F32)], axis=-1).astype(F32)
    wr_hi = wr.astype(BF16)
    wr_lo = (wr - wr_hi.astype(F32)).astype(BF16)
    br =jnp.concatenate([b_router_group, b_router_expert,
                          jnp.zeros((LANES - n_r,), F32)]).astype(F32)[None, :]

    def row(width):
        return pl.BlockSpec((tm, width), lambda i: (i, 0))

    def const(shape):
        return pl.BlockSpec(shape, lambda i: (0, 0))

    out_shapes = (
        jax.ShapeDtypeStruct((t_rows, D_MODEL), F32),
        jax.ShapeDtypeStruct((t_rows, D_MODEL), F32),
        jax.ShapeDtypeStruct((t_rows, LANES), I32),
        jax.ShapeDtypeStruct((t_rows, LANES), F32),
    )
    return pl.pallas_call(
        _merge_body,
        grid=(t_rows // tm,),
        in_specs=[row(D_MODEL), row(ATT_Q_WIDTH), row(MLSTM_WIDTH), row(D_MODEL), row(D_MODEL),
                  const((ATT_Q_WIDTH, D_MODEL)), const((MLSTM_WIDTH, D_MODEL)),
                  const((D_MODEL, D_MODEL)), const((1, D_MODEL)), const((D_MODEL, LANES)),
                  const((D_MODEL, LANES)), const((1, LANES))],
        out_specs=[row(s.shape[1]) for s in out_shapes],
        out_shape=out_shapes,
        compiler_params=pltpu.CompilerParams(
            dimension_semantics=("arbitrary",), vmem_limit_bytes=VMEM_LIMIT_BYTES),
        name="merge_route",
    )(x2, y_att, y_ml, ga, gm, w_proj_att.astype(BF16), w_proj_mlstm.astype(BF16),
      w_out.astype(BF16), g_norm_moe.astype(F32)[None, :], wr_hi, wr_lo, br)


def _expert_body(be_ref, tok_cur_ref, tok_nxt_ref, xn_hbm, wg_ref, wu_ref, wd_ref,
                 ys_ref, xbuf, wg_bf, wu_bf, wd_bf, sem, *, rows):
    i = pl.program_id(0)
    nblk = pl.num_programs(0)
    slot = lax.rem(i, 2)
    nslot = 1 - slot

    @pl.when((i == 0) | (be_ref[i] != be_ref[jnp.maximum(i - 1, 0)]))
    def _():
        wg_bf[...] = wg_ref[0].astype(BF16)
        wu_bf[...] = wu_ref[0].astype(BF16)
        wd_bf[...] = wd_ref[0].astype(BF16)

    def row_copy(tok, dst_slot, r):
        return pltpu.make_async_copy(
            xn_hbm.at[pl.ds(tok, 1), :], xbuf.at[dst_slot, pl.ds(r, 1), :], sem.at[dst_slot])

    def all_rows(dst_slot):
        return pltpu.make_async_copy(
            xn_hbm.at[pl.ds(0, rows), :], xbuf.at[dst_slot], sem.at[dst_slot])

    @pl.when(i == 0)
    def _():
        for r in range(rows):
            row_copy(tok_cur_ref[0, 0, r], 0, r).start(priority=r % DMA_PRIORITIES)

    all_rows(slot).wait()

    for r in range(rows):
        row_copy(tok_nxt_ref[0, 0, r], nslot, r).start(priority=r % DMA_PRIORITIES)

    xb = xbuf[slot].astype(BF16)
    a = jnp.dot(xb, wg_bf[...], preferred_element_type=F32)
    u = jnp.dot(xb, wu_bf[...], preferred_element_type=F32)
    hid = (a * _sigmoid(a) * u).astype(BF16)
    ys_ref[...] = jnp.dot(hid, wd_bf[...], preferred_element_type=F32)

    @pl.when(i == nblk - 1)
    def _():
        all_rows(nslot).wait()


def _experts(xn, buf_tok, block_e, w_gate, w_up, w_down):
    rows = EXPERT_ROWS
    cap = buf_tok.shape[0]
    nblk = cap // rows
    tok3 = buf_tok.reshape(nblk, 1, rows)
    grid_spec = pltpu.PrefetchScalarGridSpec(
        num_scalar_prefetch=1,
        grid=(nblk,),
        in_specs=[
            pl.BlockSpec((1, 1, rows), lambda i, be: (i, 0, 0), memory_space=pltpu.SMEM),
            pl.BlockSpec((1, 1, rows), lambda i, be: (jnp.minimum(i + 1, nblk - 1), 0, 0),
                         memory_space=pltpu.SMEM),
            pl.BlockSpec(memory_space=pl.ANY),
            pl.BlockSpec((1, D_MODEL, D_EXPERT), lambda i, be: (be[i], 0, 0)),
            pl.BlockSpec((1, D_MODEL, D_EXPERT), lambda i, be: (be[i], 0, 0)),
            pl.BlockSpec((1, D_EXPERT, D_MODEL), lambda i, be: (be[i], 0, 0)),
        ],
        out_specs=pl.BlockSpec((rows, D_MODEL), lambda i, be: (i, 0)),
        scratch_shapes=[pltpu.VMEM((2, rows, D_MODEL), F32),
                        pltpu.VMEM((D_MODEL, D_EXPERT), BF16),
                        pltpu.VMEM((D_MODEL, D_EXPERT), BF16),
                        pltpu.VMEM((D_EXPERT, D_MODEL), BF16),
                        pltpu.SemaphoreType.DMA((2,))],
    )
    return pl.pallas_call(
        functools.partial(_expert_body, rows=rows),
        grid_spec=grid_spec,
        out_shape=jax.ShapeDtypeStruct((cap, D_MODEL), F32),
        compiler_params=pltpu.CompilerParams(
            dimension_semantics=("arbitrary",), vmem_limit_bytes=VMEM_LIMIT_BYTES),
        name="moe_experts",
    )(block_e, tok3, tok3, xn, w_gate.astype(F32), w_up.astype(F32), w_down.astype(F32))


def _combine_body(dcur_ref, dnxt_ref, x1_ref, rf_ref, ys_hbm, out_ref, ybuf, sem, *, rows):
    i = pl.program_id(0)
    nt = pl.num_programs(0)
    slot = lax.rem(i, 2)
    nslot = 1 - slot

    def row_copy(src, dst_slot, k, r):
        return pltpu.make_async_copy(
            ys_hbm.at[pl.ds(src, 1), :], ybuf.at[dst_slot, k, pl.ds(r, 1), :], sem.at[dst_slot])

    def wait_all(dst_slot):
        for k in range(TOP_K_EXPERTS):
            pltpu.make_async_copy(
                ys_hbm.at[pl.ds(0, rows), :], ybuf.at[dst_slot, k], sem.at[dst_slot]).wait()

    def gather(dest_ref, dst_slot):
        for r in range(rows):
            for k in range(TOP_K_EXPERTS):
                row_copy(dest_ref[0, 0, TOP_K_EXPERTS * r + k], dst_slot, k, r).start(
                    priority=k % DMA_PRIORITIES)

    @pl.when(i == 0)
    def _():
        gather(dcur_ref, 0)

    wait_all(slot)
    gather(dnxt_ref, nslot)
    gates = rf_ref[...]
    out = x1_ref[...]
    for k in range(TOP_K_EXPERTS):
        out = out + gates[:, k:k + 1] * ybuf[slot, k]
    out_ref[...] = out

    @pl.when(i == nt - 1)
    def _():
        wait_all(nslot)


def _combine(x1, rf, ys, dest):
    t_rows = x1.shape[0]
    rows = min(COMBINE_ROWS, t_rows)
    nt = t_rows // rows
    dest3 = dest.reshape(nt, 1, rows * TOP_K_EXPERTS)
    dspec = functools.partial(pl.BlockSpec, (1, 1, rows * TOP_K_EXPERTS), memory_space=pltpu.SMEM)
    return pl.pallas_call(
        functools.partial(_combine_body, rows=rows),
        grid=(nt,),
        in_specs=[
            dspec(lambda i: (i, 0, 0)),
            dspec(lambda i: (jnp.minimum(i + 1, nt - 1), 0, 0)),
            pl.BlockSpec((rows, D_MODEL), lambda i: (i, 0)),
            pl.BlockSpec((rows, LANES), lambda i: (i, 0)),
            pl.BlockSpec(memory_space=pl.ANY),
        ],
        out_specs=pl.BlockSpec((rows, D_MODEL), lambda i: (i, 0)),
        out_shape=jax.ShapeDtypeStruct((t_rows, D_MODEL), F32),
        scratch_shapes=[pltpu.VMEM((2, TOP_K_EXPERTS, rows, D_MODEL), F32),
                        pltpu.SemaphoreType.DMA((2,))],
        compiler_params=pltpu.CompilerParams(
            dimension_semantics=("arbitrary",), vmem_limit_bytes=VMEM_LIMIT_BYTES),
        name="moe_combine",
    )(dest3, dest3, x1, rf, ys)


def _route_plan(expert_id):
    t_rows = expert_id.shape[0]
    m = t_rows * TOP_K_EXPERTS
    rows = EXPERT_ROWS
    e_flat = expert_id.reshape(m)
    onehot = (e_flat[:, None] == jnp.arange(N_EXPERTS, dtype=I32)[None, :]).astype(I32)
    running = jnp.cumsum(onehot, axis=0)
    counts = running[-1]
    rank = jnp.sum(onehot * running, axis=1) - 1
    padded = ((counts + rows - 1) // rows) * rows
    pends = jnp.cumsum(padded)
    pstarts = pends - padded
    cap = ((m + N_EXPERTS * rows + rows - 1) // rows) * rows
    nblk = cap // rows
    blk_start = jnp.arange(nblk, dtype=I32) * rows
    block_e = jnp.minimum(jnp.sum((pends[None, :] <= blk_start[:, None]).astype(I32), axis=1),
                          N_EXPERTS - 1)
    dest = (jnp.sum(onehot * pstarts[None, :], axis=1) + rank).astype(I32)
    order = jnp.argsort(e_flat, stable=True).astype(I32)
    starts = jnp.cumsum(counts) - counts
    slot = jnp.arange(cap, dtype=I32)
    off = slot - jnp.repeat(pstarts[block_e], rows)
    live = (off < jnp.repeat(counts[block_e], rows)) & (slot < pends[-1])
    src = jnp.clip(jnp.repeat(starts[block_e], rows) + off, 0, m - 1)
    buf_tok = jnp.where(live, order[src] // TOP_K_EXPERTS, slot % t_rows).astype(I32)
    return buf_tok, block_e, dest


def kernel(x, positions, g_norm_mix, w_in, g_q_att, g_k_att, w_conv_mlstm, b_conv_mlstm, b_igate,
           b_fgate, g_norm_mlstm_out, w_proj_att, w_proj_mlstm, w_out, g_norm_moe, w_router_group,
           b_router_group, w_router_expert, b_router_expert, w_exp_gate, w_exp_up, w_exp_down):
    bsz, seq, dm = x.shape
    depth = g_norm_mix.shape[0]
    x2 = x.reshape(bsz * seq, dm)
    pos2 = positions.reshape(bsz * seq, 1).astype(I32)
    for l in range(depth):
        aq, ak, av, iq, ik, small, mqk, mv, og, ga, gm = _in_proj(
            x2, pos2, g_norm_mix[l], w_in[l], g_q_att[l], g_k_att[l])
        y_att = _dsa(aq, iq, small, ak, av, ik, bsz, seq)
        y_ml = _mlstm(mqk, mv, og, small, w_conv_mlstm[l], b_conv_mlstm[l], b_igate[l], b_fgate[l],
                      g_norm_mlstm_out[l], bsz, seq)
        x1, xn, ri, rf = _merge(x2, y_att, y_ml, ga, gm, w_proj_att[l], w_proj_mlstm[l], w_out[l],
                                g_norm_moe[l], w_router_group[l], b_router_group[l],
                                w_router_expert[l], b_router_expert[l])
        buf_tok, block_e, dest = _route_plan(ri[:, :TOP_K_EXPERTS])
        ys = _experts(xn, buf_tok, block_e, w_exp_gate[l], w_exp_up[l], w_exp_down[l])
        x2 = _combine(x1, rf, ys, dest)
    return x2.reshape(bsz, seq, dm)
```
